```python
import math
import jax
import jax.numpy as jnp
from jax import lax
import numpy as np

D_MODEL = 2048
BATCH = 1
SEQ = 8192
DEPTH = 2

GRID_W = 64
CTX_LEN = 256
HEAD_DIM = 128
GLA_HEADS = 4
GLA_DK = 64
GLA_DV = 128
GLA_GATE_RANK = 16
GLA_TAU = 16.0
GLA_CHUNK = 64
GQA_HEADS = 8
GQA_KV_HEADS = 2
DIFF_HEADS = 4
DIFF_DQK = 64
DIFF_DV = 128
MIX_WIDTH = GLA_HEADS * GLA_DV + GQA_HEADS * HEAD_DIM + DIFF_HEADS * DIFF_DV
N_EXPERTS = 64
TOP_K = 8
EXPERT_HIDDEN = 512
SHARED_HIDDEN = 512
ROUTE_SCALE = 2.5
Q_BLOCK = 128
ROPE_THETA = 10000.0
NORM_EPS = 1e-6
IN_SIZES = (GLA_HEADS * GLA_DK, GLA_HEADS * GLA_DK, GLA_HEADS * GLA_DV, GLA_HEADS * GLA_DV,
            GLA_GATE_RANK, GLA_GATE_RANK,
            GQA_HEADS * HEAD_DIM, GQA_KV_HEADS * HEAD_DIM, GQA_KV_HEADS * HEAD_DIM,
            DIFF_HEADS * 2 * DIFF_DQK, DIFF_HEADS * 2 * DIFF_DQK, DIFF_HEADS * DIFF_DV)
IN_WIDTH = sum(IN_SIZES)

kernel_name = 'hybrid_gla_gqa_diffattn_moe_dit'


def rms_norm(x, g):
    xf = x.astype(jnp.float32)
    y = xf * lax.rsqrt(jnp.mean(xf * xf, axis=-1, keepdims=True) + NORM_EPS)
    return (y * g.astype(jnp.float32)).astype(x.dtype)


def modulate(x, g, shift, scale):
    return rms_norm(x, g) * (1.0 + scale) + shift


def to_heads(z, n_heads, head_dim):
    b, t, _ = z.shape
    return z.reshape(b, t, n_heads, head_dim).transpose(0, 2, 1, 3)


def merge_heads(o):
    b, n, t, d = o.shape
    return o.transpose(0, 2, 1, 3).reshape(b, t, n * d)


def split_input_columns(z):
    points = np.cumsum(IN_SIZES)[:-1].tolist()
    return jnp.split(z, points, axis=-1)


def axial_rope_tables(row, col, dim):
    half = dim // 2
    inv_freq = ROPE_THETA ** (-jnp.arange(0, half, 2, dtype=jnp.float32) / half)
    ang_r = row.astype(jnp.float32)[:, None] * inv_freq
    ang_c = col.astype(jnp.float32)[:, None] * inv_freq
    return (jnp.cos(ang_r), jnp.sin(ang_r), jnp.cos(ang_c), jnp.sin(ang_c))


def rope_1d(x, cos, sin):
    x1, x2 = jnp.split(x, 2, axis=-1)
    return jnp.concatenate([x1 * cos - x2 * sin, x2 * cos + x1 * sin], axis=-1)


def apply_axial_rope(x, tables):
    cos_r, sin_r, cos_c, sin_c = tables
    x_row, x_col = jnp.split(x.astype(jnp.float32), 2, axis=-1)
    y = jnp.concatenate([rope_1d(x_row, cos_r, sin_r), rope_1d(x_col, cos_c, sin_c)], axis=-1)
    return y.astype(x.dtype)


def sweep_query_blocks(fn, qs):
    t = qs[0].shape[-2]
    nb = t // Q_BLOCK
    blocks = tuple(jnp.moveaxis(q.reshape(q.shape[:-2] + (nb, Q_BLOCK, q.shape[-1])), -3, 0) for q in qs)
    o = lax.map(lambda blk: fn(*blk), blocks)
    o = jnp.moveaxis(o, 0, -3)
    return o.reshape(o.shape[:-3] + (t, o.shape[-1]))


def gqa_attention(q, k, v):
    b, h, tq, d = q.shape
    kv = k.shape[1]
    qg = q.reshape(b, kv, h // kv, tq, d)
    scale = d ** -0.5

    def block(qb):
        s = jnp.einsum('bkgqd,bksd->bkgqs', qb, k, preferred_element_type=jnp.float32) * scale
        p = jax.nn.softmax(s, axis=-1).astype(v.dtype)
        return jnp.einsum('bkgqs,bksd->bkgqd', p, v)

    return sweep_query_blocks(block, (qg,)).reshape(b, h, tq, d)


def diff_attention(q1, q2, k1, k2, v, lam):
    scale = q1.shape[-1] ** -0.5

    def block(a1, a2):
        p1 = jax.nn.softmax(jnp.einsum('bhqd,bhsd->bhqs', a1, k1, preferred_element_type=jnp.float32) * scale, axis=-1)
        p2 = jax.nn.softmax(jnp.einsum('bhqd,bhsd->bhqs', a2, k2, preferred_element_type=jnp.float32) * scale, axis=-1)
        w = (p1 - lam * p2).astype(v.dtype)
        return jnp.einsum('bhqs,bhsd->bhqd', w, v)

    return sweep_query_blocks(block, (q1, q2))


def gla_chunked(q, k, v, log_a, s0):
    b, h, t, dk = q.shape
    dv = v.shape[-1]
    n = t // GLA_CHUNK
    f32 = jnp.float32
    q, k, log_a = (a.astype(f32).reshape(b, h, n, GLA_CHUNK, dk) for a in (q, k, log_a))
    v = v.astype(f32).reshape(b, h, n, GLA_CHUNK, dv)
    cum = jnp.cumsum(log_a, axis=3)
    cum_last = cum[:, :, :, -1:, :]
    q_dec = q * jnp.exp(cum)
    k_inv = k * jnp.exp(-cum)
    k_end = k * jnp.exp(cum_last - cum)
    mask = jnp.tril(jnp.ones((GLA_CHUNK, GLA_CHUNK), f32))
    a_intra = jnp.einsum('bhncd,bhnsd->bhncs', q_dec, k_inv) * mask
    o_intra = jnp.einsum('bhncs,bhnsv->bhncv', a_intra, v)
    u = jnp.einsum('bhncd,bhncv->bhndv', k_end, v)
    g = jnp.exp(cum_last[:, :, :, 0, :])

    def step(s, inp):
        g_n, u_n = inp
        return g_n[..., None] * s + u_n, s

    s_fin, s_in = lax.scan(step, s0, (jnp.moveaxis(g, 2, 0), jnp.moveaxis(u, 2, 0)))
    s_in = jnp.moveaxis(s_in, 0, 2)
    o_inter = jnp.einsum('bhncd,bhndv->bhncv', q_dec, s_in)
    return (o_intra + o_inter).reshape(b, h, t, dv), s_fin


def gla_bidirectional(q, k, v, la_f, la_b, s0_f, s0_b):
    o_f, s_f = gla_chunked(q, k, v, la_f, s0_f)
    flip = lambda a: jnp.flip(a, axis=2)
    o_b, s_b = gla_chunked(flip(q), flip(k), flip(v), flip(la_b), s0_b)
    return o_f + flip(o_b), s_f, s_b


def token_mixers(hc, hx, w_in, gla_wa_f, gla_ba_f, gla_wa_b, gla_ba_b, gla_norm_g,
                 q_norm_g, k_norm_g, lq1, lk1, lq2, lk2, diff_norm_g, w_out,
                 lam_init, rope_gqa, rope_diff, need_ctx):
    zc = split_input_columns(hc @ w_in)
    zx = split_input_columns(hx @ w_in)

    def gla_inputs(z):
        q = to_heads(z[0], GLA_HEADS, GLA_DK) * (GLA_DK ** -0.5)
        k = to_heads(z[1], GLA_HEADS, GLA_DK)
        v = to_heads(z[2], GLA_HEADS, GLA_DV)
        la_f = jax.nn.log_sigmoid((z[4] @ gla_wa_f + gla_ba_f).astype(jnp.float32)) / GLA_TAU
        la_b = jax.nn.log_sigmoid((z[5] @ gla_wa_b + gla_ba_b).astype(jnp.float32)) / GLA_TAU
        return q, k, v, to_heads(la_f, GLA_HEADS, GLA_DK), to_heads(la_b, GLA_HEADS, GLA_DK)

    def gla_output(o, z):
        o = rms_norm(o, gla_norm_g).astype(z[3].dtype)
        return merge_heads(o) * jax.nn.silu(z[3])

    s0 = jnp.zeros((hx.shape[0], GLA_HEADS, GLA_DK, GLA_DV), jnp.float32)
    oc_gla, s_f, s_b = gla_bidirectional(*gla_inputs(zc), s0, s0)
    ox_gla, _, _ = gla_bidirectional(*gla_inputs(zx), s_f, s_b)

    def gqa_inputs(z):
        q = rms_norm(to_heads(z[6], GQA_HEADS, HEAD_DIM), q_norm_g)
        k = rms_norm(to_heads(z[7], GQA_KV_HEADS, HEAD_DIM), k_norm_g)
        v = to_heads(z[8], GQA_KV_HEADS, HEAD_DIM)
        return q, k, v

    qc, kc, vc = gqa_inputs(zc)
    qx, kx, vx = gqa_inputs(zx)
    qx = apply_axial_rope(qx, rope_gqa)
    kx = apply_axial_rope(kx, rope_gqa)
    ox_gqa = gqa_attention(qx, jnp.concatenate([kc, kx], axis=2), jnp.concatenate([vc, vx], axis=2))

    lam = (jnp.exp(jnp.sum(lq1.astype(jnp.float32) * lk1.astype(jnp.float32)))
           - jnp.exp(jnp.sum(lq2.astype(jnp.float32) * lk2.astype(jnp.float32))) + lam_init)

    def diff_inputs(z):
        q1, q2 = jnp.split(to_heads(z[9], DIFF_HEADS, 2 * DIFF_DQK), 2, axis=-1)
        k1, k2 = jnp.split(to_heads(z[10], DIFF_HEADS, 2 * DIFF_DQK), 2, axis=-1)
        v = to_heads(z[11], DIFF_HEADS, DIFF_DV)
        return q1, q2, k1, k2, v

    def diff_output(o):
        return merge_heads(rms_norm(o, diff_norm_g) * (1.0 - lam_init))

    q1c, q2c, k1c, k2c, vdc = diff_inputs(zc)
    q1x, q2x, k1x, k2x, vdx = diff_inputs(zx)
    q1x, q2x, k1x, k2x = (apply_axial_rope(a, rope_diff) for a in (q1x, q2x, k1x, k2x))
    ox_diff = diff_attention(q1x, q2x, jnp.concatenate([k1c, k1x], axis=2),
                             jnp.concatenate([k2c, k2x], axis=2),
                             jnp.concatenate([vdc, vdx], axis=2), lam)

    ox = jnp.concatenate([gla_output(ox_gla, zx), merge_heads(ox_gqa), diff_output(ox_diff)], axis=-1) @ w_out
    if not need_ctx:
        return None, ox
    oc_gqa = gqa_attention(qc, kc, vc)
    oc_diff = diff_attention(q1c, q2c, k1c, k2c, vdc, lam)
    oc = jnp.concatenate([gla_output(oc_gla, zc), merge_heads(oc_gqa), diff_output(oc_diff)], axis=-1) @ w_out
    return oc, ox


def moe_ffn(h, router_w, router_b, w_gate, w_up, w_down, sh_gate, sh_up, sh_down):
    shape = h.shape
    t = h.reshape(-1, shape[-1])
    scores = jax.nn.sigmoid(jnp.dot(t, router_w, preferred_element_type=jnp.float32))
    _, idx = lax.top_k(scores + router_b.astype(jnp.float32), TOP_K)
    sel = jnp.take_along_axis(scores, idx, axis=-1)
    weights = sel / jnp.sum(sel, axis=-1, keepdims=True) * ROUTE_SCALE
    gates = jnp.einsum('nk,nke->en', weights, jax.nn.one_hot(idx, N_EXPERTS, dtype=jnp.float32))

    def expert(acc, p):
        wg, wu, wd, g = p
        y = (jax.nn.silu(t @ wg) * (t @ wu)) @ wd
        return acc + g[:, None] * y, None

    routed, _ = lax.scan(expert, jnp.zeros(t.shape, jnp.float32), (w_gate, w_up, w_down, gates))
    shared = (jax.nn.silu(t @ sh_gate) * (t @ sh_up)) @ sh_down
    return (routed + shared).astype(h.dtype).reshape(shape)


def setup_inputs(seed: int = 0) -> dict:
    key = jax.random.key(seed)
    ks = iter(jax.random.split(key, 32))
    f32 = jnp.float32
    d = D_MODEL

    def nrm(shape, scale):
        return jax.random.normal(next(ks), shape, f32) * scale

    def gain(shape):
        return 1.0 + nrm(shape, 0.02)

    return {
        'x': nrm((BATCH, SEQ, d), 1.0),
        'c': nrm((BATCH, d), 1.0),
        'ctx': nrm((BATCH, CTX_LEN, d), 1.0),
        'c_ctx': nrm((d,), 1.0),
        'norm1_g': gain((DEPTH, d)),
        'norm2_g': gain((DEPTH, d)),
        'w_mod': nrm((DEPTH, d, 6 * d), 0.5 * d ** -0.5),
        'b_mod': nrm((DEPTH, 6 * d), 0.02),
        'w_in': nrm((DEPTH, d, IN_WIDTH), d ** -0.5),
        'gla_wa_f': nrm((DEPTH, GLA_GATE_RANK, GLA_HEADS * GLA_DK), GLA_GATE_RANK ** -0.5),
        'gla_ba_f': nrm((DEPTH, GLA_HEADS * GLA_DK), 0.1),
        'gla_wa_b': nrm((DEPTH, GLA_GATE_RANK, GLA_HEADS * GLA_DK), GLA_GATE_RANK ** -0.5),
        'gla_ba_b': nrm((DEPTH, GLA_HEADS * GLA_DK), 0.1),
        'gla_norm_g': gain((DEPTH, GLA_DV)),
        'q_norm_g': gain((DEPTH, HEAD_DIM)),
        'k_norm_g': gain((DEPTH, HEAD_DIM)),
        'diff_lq1': nrm((DEPTH, DIFF_DQK), 0.1),
        'diff_lk1': nrm((DEPTH, DIFF_DQK), 0.1),
        'diff_lq2': nrm((DEPTH, DIFF_DQK), 0.1),
        'diff_lk2': nrm((DEPTH, DIFF_DQK), 0.1),
        'diff_norm_g': gain((DEPTH, DIFF_DV)),
        'w_out': nrm((DEPTH, MIX_WIDTH, d), MIX_WIDTH ** -0.5),
        'router_w': nrm((DEPTH, d, N_EXPERTS), d ** -0.5),
        'router_b': nrm((DEPTH, N_EXPERTS), 0.01),
        'exp_w_gate': nrm((DEPTH, N_EXPERTS, d, EXPERT_HIDDEN), d ** -0.5),
        'exp_w_up': nrm((DEPTH, N_EXPERTS, d, EXPERT_HIDDEN), d ** -0.5),
        'exp_w_down': nrm((DEPTH, N_EXPERTS, EXPERT_HIDDEN, d), EXPERT_HIDDEN ** -0.5),
        'sh_w_gate': nrm((DEPTH, d, SHARED_HIDDEN), d ** -0.5),
        'sh_w_up': nrm((DEPTH, d, SHARED_HIDDEN), d ** -0.5),
        'sh_w_down': nrm((DEPTH, SHARED_HIDDEN, d), SHARED_HIDDEN ** -0.5),
        'final_g': gain((d,)),
    }


def reference(x, c, ctx, c_ctx, norm1_g, norm2_g, w_mod, b_mod, w_in,
              gla_wa_f, gla_ba_f, gla_wa_b, gla_ba_b, gla_norm_g, q_norm_g, k_norm_g,
              diff_lq1, diff_lk1, diff_lq2, diff_lk2, diff_norm_g, w_out,
              router_w, router_b, exp_w_gate, exp_w_up, exp_w_down,
              sh_w_gate, sh_w_up, sh_w_down, final_g):
    n_tokens = x.shape[1]
    ROWS = n_tokens // GRID_W
    row = jnp.repeat(jnp.arange(ROWS, dtype=jnp.int32), GRID_W)
    col = jnp.tile(jnp.arange(GRID_W, dtype=jnp.int32), ROWS)
    rope_gqa = axial_rope_tables(row, col, HEAD_DIM)
    rope_diff = axial_rope_tables(row, col, DIFF_DQK)

    h = ctx
    for l in range(DEPTH):
        need_ctx = l < DEPTH - 1
        lam_init = 0.8 - 0.6 * math.exp(-0.3 * l)
        mod_x = (jax.nn.silu(c) @ w_mod[l] + b_mod[l])[:, None, :]
        mod_c = jax.nn.silu(c_ctx) @ w_mod[l] + b_mod[l]
        shx1, scx1, gx1, shx2, scx2, gx2 = jnp.split(mod_x, 6, axis=-1)
        shc1, scc1, gc1, shc2, scc2, gc2 = jnp.split(mod_c, 6, axis=-1)

        oc, ox = token_mixers(modulate(h, norm1_g[l], shc1, scc1), modulate(x, norm1_g[l], shx1, scx1),
                              w_in[l], gla_wa_f[l], gla_ba_f[l], gla_wa_b[l], gla_ba_b[l], gla_norm_g[l],
                              q_norm_g[l], k_norm_g[l], diff_lq1[l], diff_lk1[l], diff_lq2[l], diff_lk2[l],
                              diff_norm_g[l], w_out[l], lam_init, rope_gqa, rope_diff, need_ctx)
        x = x + gx1 * ox

        ffn_w = (router_w[l], router_b[l], exp_w_gate[l], exp_w_up[l], exp_w_down[l],
                 sh_w_gate[l], sh_w_up[l], sh_w_down[l])
        if need_ctx:
            h = h + gc1 * oc
            n_ctx = h.shape[1]
            y = moe_ffn(jnp.concatenate([modulate(h, norm2_g[l], shc2, scc2),
                                         modulate(x, norm2_g[l], shx2, scx2)], axis=1), *ffn_w)
            h = h + gc2 * y[:, :n_ctx]
            x = x + gx2 * y[:, n_ctx:]
        else:
            x = x + gx2 * moe_ffn(modulate(x, norm2_g[l], shx2, scx2), *ffn_w)
    return rms_norm(x, final_g)
```

```python
import functools
import math

import jax
import jax.numpy as jnp
from jax import lax
from jax.experimental import pallas as pl
from jax.experimental.pallas import tpu as pltpu

F32 = jnp.float32
BF16 = jnp.bfloat16

GRID_W = 64
HEAD_DIM = 128
GLA_HEADS = 4
GLA_DK = 64
GLA_DV = 128
GLA_GATE_RANK = 16
GLA_TAU = 16.0
GLA_CHUNK = 64
GQA_HEADS = 8
GQA_KV_HEADS = 2
DIFF_HEADS = 4
DIFF_DQK = 64
DIFF_DV = 128
TOP_K = 8
ROUTE_SCALE = 2.5
ROPE_THETA = 10000.0
NORM_EPS = 1e-6

LANES = 128
ROW_TILE = 256
MOE_ROW_TILE = 768
VMEM_LIMIT = 56 * 1024 * 1024

GLA_QK = GLA_HEADS * GLA_DK
GLA_V = GLA_HEADS * GLA_DV
GQA_Q = GQA_HEADS * HEAD_DIM
GQA_KV = GQA_KV_HEADS * HEAD_DIM
DIFF_QK = DIFF_HEADS * 2 * DIFF_DQK
DIFF_V = DIFF_HEADS * DIFF_DV
MIX_WIDTH = GLA_V + GQA_Q + DIFF_V

C_GLAQ = 0
C_GLAK = C_GLAQ + GLA_QK
C_GLAV = C_GLAK + GLA_QK
C_GATE = C_GLAV + GLA_V
C_GQAQ = C_GATE + GLA_V
C_GQAK = C_GQAQ + GQA_Q
C_GQAV = C_GQAK + GQA_KV
C_DQ = C_GQAV + GQA_KV
C_DK = C_DQ + DIFF_QK
C_DV = C_DK + DIFF_QK
C_LR = C_DV + DIFF_V
IN_COLS = C_LR + LANES
ORIG_LR = 2 * GLA_QK + 2 * GLA_V


def _params(*sem):
    return pltpu.CompilerParams(dimension_semantics=sem, vmem_limit_bytes=VMEM_LIMIT)


def _resident(shape):
    nd = len(shape)
    return pl.BlockSpec(shape, lambda *_: (0,) * nd, pipeline_mode=pl.Buffered(1))


def _silu(a):
    return a / (1.0 + jnp.exp(-a))


def _rms(x, g):
    return x * lax.rsqrt(jnp.mean(x * x, axis=-1, keepdims=True) + NORM_EPS) * g


def _row_group(i):
    return jnp.where(i == 0, 1, 0)


def _mod_kernel(a_ref, w_ref, b_ref, o_ref):
    a = _silu(a_ref[...])
    o_ref[...] = jnp.dot(a.astype(BF16), w_ref[...].astype(BF16),
                         preferred_element_type=F32) + b_ref[...]


def _modulation(c, c_ctx, w_mod, b_mod):
    depth, d, six_d = w_mod.shape
    a = jnp.zeros((8, d), F32).at[0].set(c[0]).at[1].set(c_ctx)
    tn = d // 2
    out = pl.pallas_call(
        _mod_kernel,
        grid=(depth, six_d // tn),
        in_specs=[pl.BlockSpec((8, d), lambda l, j: (0, 0)),
                  pl.BlockSpec((None, d, tn), lambda l, j: (l, 0, j)),
                  pl.BlockSpec((None, 1, tn), lambda l, j: (l, 0, j))],
        out_specs=pl.BlockSpec((None, 8, tn), lambda l, j: (l, 0, j)),
        out_shape=jax.ShapeDtypeStruct((depth, 8, six_d), F32),
        compiler_params=_params("parallel", "parallel"),
        name="modulation",
    )(a, w_mod, b_mod.reshape(depth, 1, six_d))
    m = out[:, :2].reshape(depth, 2, 6, d)
    return jnp.pad(m, ((0, 0), (0, 0), (0, 2), (0, 0)))


def _rope(xh, cos, sin, first, shift_first, shift_second):
    partner = jnp.where(first, pltpu.roll(xh, shift_first, 1), pltpu.roll(xh, shift_second, 1))
    return xh * cos + partner * sin


def _inproj_kernel(x_ref, mod_ref, g_ref, w_ref, wa_ref, ba_ref, qg_ref, kg_ref,
                   cg_ref, sg_ref, cd_ref, sd_ref,
                   glaq_ref, glak_ref, glav_ref, gate_ref, la_ref,
                   q_ref, k_ref, v_ref, dq_ref, dk_ref, dv_ref):
    x = x_ref[...]
    h = _rms(x, g_ref[...]) * (1.0 + mod_ref[1:2, :]) + mod_ref[0:1, :]
    hb = h.astype(BF16)

    def proj(start, width):
        return jnp.dot(hb, w_ref[:, start:start + width], preferred_element_type=F32)

    glaq_ref[...] = proj(C_GLAQ, GLA_QK) * (GLA_DK ** -0.5)
    glak_ref[...] = proj(C_GLAK, GLA_QK)
    glav_ref[...] = proj(C_GLAV, GLA_V)
    gate_ref[...] = proj(C_GATE, GLA_V)
    v_ref[...] = proj(C_GQAV, GQA_KV).astype(BF16)
    dv_ref[...] = proj(C_DV, DIFF_V).astype(BF16)

    z_lr = proj(C_LR, LANES).astype(BF16)
    pre = jnp.dot(z_lr, wa_ref[...], preferred_element_type=F32) + ba_ref[...]
    log_sig = -(jnp.maximum(-pre, 0.0) + jnp.log1p(jnp.exp(-jnp.abs(pre))))
    la = log_sig * (1.0 / GLA_TAU)
    la_ref[0] = la[:, :GLA_QK]
    la_ref[1] = la[:, GLA_QK:]

    rows = x.shape[0]
    lane = lax.broadcasted_iota(jnp.int32, (rows, LANES), 1)
    first_g = (lane % 64) < 32
    first_d = (lane % 32) < 16
    cg, sg, cd, sd = cg_ref[...], sg_ref[...], cd_ref[...], sd_ref[...]
    scale_g = HEAD_DIM ** -0.5
    scale_d = DIFF_DQK ** -0.5

    zq = proj(C_GQAQ, GQA_Q)
    for hd in range(GQA_HEADS):
        qh = _rms(zq[:, hd * HEAD_DIM:(hd + 1) * HEAD_DIM], qg_ref[...])
        q_ref[:, hd * HEAD_DIM:(hd + 1) * HEAD_DIM] = (
            _rope(qh, cg, sg, first_g, 96, 32) * scale_g).astype(BF16)
    zk = proj(C_GQAK, GQA_KV)
    for hd in range(GQA_KV_HEADS):
        kh = _rms(zk[:, hd * HEAD_DIM:(hd + 1) * HEAD_DIM], kg_ref[...])
        k_ref[:, hd * HEAD_DIM:(hd + 1) * HEAD_DIM] = _rope(kh, cg, sg, first_g, 96, 32).astype(BF16)
    zdq = proj(C_DQ, DIFF_QK)
    zdk = proj(C_DK, DIFF_QK)
    for hd in range(DIFF_HEADS):
        sl = slice(hd * LANES, (hd + 1) * LANES)
        dq_ref[:, sl] = (_rope(zdq[:, sl], cd, sd, first_d, 112, 16) * scale_d).astype(BF16)
        dk_ref[:, sl] = _rope(zdk[:, sl], cd, sd, first_d, 112, 16).astype(BF16)


def _input_projection(s, mod, norm_g, w_in_b, wa, ba, q_norm_g, k_norm_g, tables):
    n, d = s.shape
    nt = n // ROW_TILE
    row = lambda w: pl.BlockSpec((ROW_TILE, w), lambda i: (i, 0))
    f32o = lambda w: jax.ShapeDtypeStruct((n, w), F32)
    bfo = lambda w: jax.ShapeDtypeStruct((n, w), BF16)
    return pl.pallas_call(
        _inproj_kernel,
        grid=(nt,),
        in_specs=[row(d),
                  pl.BlockSpec((None, 8, d), lambda i: (_row_group(i), 0, 0)),
                  _resident((1, d)),
                  _resident((d, IN_COLS)),
                  _resident((LANES, 2 * GLA_QK)),
                  _resident((1, 2 * GLA_QK)),
                  _resident((1, HEAD_DIM)),
                  _resident((1, HEAD_DIM)),
                  row(LANES), row(LANES), row(LANES), row(LANES)],
        out_specs=[row(GLA_QK), row(GLA_QK), row(GLA_V), row(GLA_V),
                   pl.BlockSpec((2, ROW_TILE, GLA_QK), lambda i: (0, i, 0)),
                   row(GQA_Q), row(GQA_KV), row(GQA_KV), row(DIFF_QK), row(DIFF_QK), row(DIFF_V)],
        out_shape=[f32o(GLA_QK), f32o(GLA_QK), f32o(GLA_V), f32o(GLA_V),
                   jax.ShapeDtypeStruct((2, n, GLA_QK), F32),
                   bfo(GQA_Q), bfo(GQA_KV), bfo(GQA_KV), bfo(DIFF_QK), bfo(DIFF_QK), bfo(DIFF_V)],
        compiler_params=_params("parallel"),
        name="input_projection",
    )(s, mod, norm_g.reshape(1, d), w_in_b, wa, ba, q_norm_g.reshape(1, -1), k_norm_g.reshape(1, -1),
      *tables)


def _gla_block(reverse, q_ref, k_ref, v_ref, la_ref, o_ref, st_ref):
    rows = q_ref.shape[0]
    n_chunks = rows // GLA_CHUNK
    ri = lax.broadcasted_iota(jnp.int32, (rows, rows), 0)
    ci = lax.broadcasted_iota(jnp.int32, (rows, rows), 1)
    same_chunk = (ri // GLA_CHUNK) == (ci // GLA_CHUNK)
    allowed = same_chunk & ((ci >= ri) if reverse else (ci <= ri))
    la = la_ref[...]
    cum = jnp.dot(jnp.where(allowed, 1.0, 0.0).astype(F32), la,
                  preferred_element_type=F32, precision=lax.Precision.HIGHEST)
    tot = jnp.dot(jnp.where(same_chunk, 1.0, 0.0).astype(F32), la,
                  preferred_element_type=F32, precision=lax.Precision.HIGHEST)
    k = k_ref[...]
    q_dec = q_ref[...] * jnp.exp(cum)
    k_inv = (k * jnp.exp(-cum)).astype(BF16)
    k_end = k * jnp.exp(tot - cum)
    v = v_ref[...]
    vb = v.astype(BF16)
    v_t = v.T.astype(BF16)
    lane_head = lax.broadcasted_iota(jnp.int32, (rows, GLA_QK), 1) // GLA_DK
    row_chunk = lax.broadcasted_iota(jnp.int32, (rows, GLA_QK), 0) // GLA_CHUNK
    nt_dims = (((1,), (1,)), ((), ()))

    for hd in range(GLA_HEADS):
        qh = jnp.where(lane_head == hd, q_dec, 0.0).astype(BF16)
        a = lax.dot_general(qh, k_inv, nt_dims, preferred_element_type=F32)
        a = jnp.where(allowed, a, 0.0).astype(BF16)
        o_ref[:, hd * GLA_DV:(hd + 1) * GLA_DV] = jnp.dot(
            a, vb[:, hd * GLA_DV:(hd + 1) * GLA_DV], preferred_element_type=F32)

    chunk_order = range(n_chunks - 1, -1, -1) if reverse else range(n_chunks)
    chunk_lane_head = lax.broadcasted_iota(jnp.int32, (GLA_CHUNK, GLA_QK), 1) // GLA_DK
    for c in chunk_order:
        sl = slice(c * GLA_CHUNK, (c + 1) * GLA_CHUNK)
        state = st_ref[...]
        q4 = jnp.concatenate(
            [jnp.where(chunk_lane_head == hd, q_dec[sl], 0.0) for hd in range(GLA_HEADS)],
            axis=0).astype(BF16)
        r = lax.dot_general(q4, state.astype(BF16), nt_dims, preferred_element_type=F32)
        for hd in range(GLA_HEADS):
            o_ref[sl, hd * GLA_DV:(hd + 1) * GLA_DV] += r[hd * GLA_CHUNK:(hd + 1) * GLA_CHUNK,
                                                          hd * GLA_DV:(hd + 1) * GLA_DV]
        k_end_c = jnp.where(row_chunk == c, k_end, 0.0).astype(BF16)
        u_t = jnp.dot(v_t, k_end_c, preferred_element_type=F32)
        st_ref[...] = state * jnp.exp(tot[c * GLA_CHUNK:c * GLA_CHUNK + 1]) + u_t


def _gla_kernel(q_ref, k_ref, v_ref, la_ref, o_ref, st_ref):
    @pl.when(pl.program_id(1) == 0)
    def _():
        st_ref[...] = jnp.zeros_like(st_ref)

    @pl.when(pl.program_id(0) == 0)
    def _():
        _gla_block(False, q_ref, k_ref, v_ref, la_ref, o_ref, st_ref)

    @pl.when(pl.program_id(0) == 1)
    def _():
        _gla_block(True, q_ref, k_ref, v_ref, la_ref, o_ref, st_ref)


def _gla(glaq, glak, glav, la):
    n = glaq.shape[0]
    nt = n // ROW_TILE

    def blk(dr, j):
        return jnp.where(dr == 0, j, jnp.where(j == 0, 0, nt - j))

    row = lambda w: pl.BlockSpec((ROW_TILE, w), lambda dr, j: (blk(dr, j), 0))
    return pl.pallas_call(
        _gla_kernel,
        grid=(2, nt),
        in_specs=[row(GLA_QK), row(GLA_QK), row(GLA_V),
                  pl.BlockSpec((None, ROW_TILE, GLA_QK), lambda dr, j: (dr, blk(dr, j), 0))],
        out_specs=pl.BlockSpec((None, ROW_TILE, GLA_V), lambda dr, j: (dr, blk(dr, j), 0)),
        out_shape=jax.ShapeDtypeStruct((2, n, GLA_V), F32),
        scratch_shapes=[pltpu.VMEM((GLA_V, GLA_QK), F32)],
        compiler_params=_params("arbitrary", "arbitrary"),
        name="gla_scan",
    )(glaq, glak, glav, la)


def _online_softmax_step(s, v, m_prev, l_prev, acc_prev):
    m_new = jnp.maximum(m_prev, jnp.max(s, axis=-1, keepdims=True))
    alpha = jnp.exp(m_prev - m_new)
    p = jnp.exp(s - m_new)
    l_new = alpha * l_prev + jnp.sum(p, axis=-1, keepdims=True)
    acc_new = alpha * acc_prev + jnp.dot(p.astype(BF16), v, preferred_element_type=F32)
    return m_new, l_new, acc_new


def _n_key_tiles(first_q_tile, n_keys):
    qi = pl.program_id(1) + first_q_tile
    return jnp.where(qi == 0, 1, n_keys // ROW_TILE)


def _gqa_kernel(first_q_tile, q_ref, k_ref, v_ref, o_ref):
    tq = q_ref.shape[0]
    n_tiles = _n_key_tiles(first_q_tile, k_ref.shape[0])
    group = GQA_HEADS // GQA_KV_HEADS
    for hd in range(group):
        q = q_ref[:, hd * HEAD_DIM:(hd + 1) * HEAD_DIM]

        def body(t, carry):
            ks = pl.multiple_of(t * ROW_TILE, ROW_TILE)
            k = k_ref[pl.ds(ks, ROW_TILE), :]
            v = v_ref[pl.ds(ks, ROW_TILE), :]
            s = lax.dot_general(q, k, (((1,), (1,)), ((), ())), preferred_element_type=F32)
            return _online_softmax_step(s, v, *carry)

        init = (jnp.full((tq, 1), -jnp.inf, F32), jnp.zeros((tq, 1), F32), jnp.zeros((tq, HEAD_DIM), F32))
        _, l, acc = lax.fori_loop(0, n_tiles, body, init)
        o_ref[:, hd * HEAD_DIM:(hd + 1) * HEAD_DIM] = (acc / l).astype(o_ref.dtype)


def _gqa_attention(q, k, v, first_q_tile):
    n = q.shape[0]
    nq = n // ROW_TILE - first_q_tile
    gw = GQA_Q // GQA_KV_HEADS
    return pl.pallas_call(
        functools.partial(_gqa_kernel, first_q_tile),
        grid=(GQA_KV_HEADS, nq),
        in_specs=[pl.BlockSpec((ROW_TILE, gw), lambda g, i: (i + first_q_tile, g)),
                  pl.BlockSpec((n, HEAD_DIM), lambda g, i: (0, g)),
                  pl.BlockSpec((n, HEAD_DIM), lambda g, i: (0, g))],
        out_specs=pl.BlockSpec((ROW_TILE, gw), lambda g, i: (i + first_q_tile, g)),
        out_shape=jax.ShapeDtypeStruct((n, GQA_Q), BF16),
        compiler_params=_params("parallel", "parallel"),
        name="gqa_attention",
    )(q, k, v)


def _diff_kernel(first_q_tile, lam_init, q_ref, k_ref, v_ref, lamv_ref, g_ref, o_ref):
    tq = q_ref.shape[0]
    n_tiles = _n_key_tiles(first_q_tile, k_ref.shape[0])
    q = q_ref[...]
    lane = lax.broadcasted_iota(jnp.int32, q.shape, 1)
    zero = jnp.zeros_like(q)
    q1 = jnp.where(lane < DIFF_DQK, q, zero)
    q2 = jnp.where(lane >= DIFF_DQK, q, zero)

    def body(t, carry):
        ks = pl.multiple_of(t * ROW_TILE, ROW_TILE)
        k = k_ref[pl.ds(ks, ROW_TILE), :]
        v = v_ref[pl.ds(ks, ROW_TILE), :]
        s1 = lax.dot_general(q1, k, (((1,), (1,)), ((), ())), preferred_element_type=F32)
        s2 = lax.dot_general(q2, k, (((1,), (1,)), ((), ())), preferred_element_type=F32)
        c1 = _online_softmax_step(s1, v, *carry[:3])
        c2 = _online_softmax_step(s2, v, *carry[3:])
        return c1 + c2

    one = (jnp.full((tq, 1), -jnp.inf, F32), jnp.zeros((tq, 1), F32), jnp.zeros((tq, DIFF_DV), F32))
    _, l1, acc1, _, l2, acc2 = lax.fori_loop(0, n_tiles, body, one + one)
    lv = lamv_ref[...]
    lam = (jnp.exp(jnp.sum(lv[0:1] * lv[1:2], axis=-1, keepdims=True))
           - jnp.exp(jnp.sum(lv[2:3] * lv[3:4], axis=-1, keepdims=True)) + lam_init)
    o = acc1 / l1 - lam * (acc2 / l2)
    o_ref[...] = (_rms(o, g_ref[...]) * (1.0 - lam_init)).astype(o_ref.dtype)


def _diff_attention(q, k, v, lamv, norm_g, lam_init, first_q_tile):
    n = q.shape[0]
    nq = n // ROW_TILE - first_q_tile
    return pl.pallas_call(
        functools.partial(_diff_kernel, first_q_tile, lam_init),
        grid=(DIFF_HEADS, nq),
        in_specs=[pl.BlockSpec((ROW_TILE, LANES), lambda h, i: (i + first_q_tile, h)),
                  pl.BlockSpec((n, LANES), lambda h, i: (0, h)),
                  pl.BlockSpec((n, DIFF_DV), lambda h, i: (0, h)),
                  pl.BlockSpec((8, LANES), lambda h, i: (0, 0)),
                  pl.BlockSpec((1, DIFF_DV), lambda h, i: (0, 0))],
        out_specs=pl.BlockSpec((ROW_TILE, DIFF_DV), lambda h, i: (i + first_q_tile, h)),
        out_shape=jax.ShapeDtypeStruct((n, DIFF_V), BF16),
        compiler_params=_params("parallel", "parallel"),
        name="diff_attention",
    )(q, k, v, lamv, norm_g.reshape(1, -1))


def _outproj_kernel(s_ref, mod_ref, og_ref, gate_ref, oq_ref, od_ref, gg_ref, w_ref, o_ref):
    og = og_ref[0] + og_ref[1]
    gate = gate_ref[...]
    y = jnp.zeros(s_ref.shape, F32)
    for hd in range(GLA_HEADS):
        sl = slice(hd * GLA_DV, (hd + 1) * GLA_DV)
        oh = _rms(og[:, sl], gg_ref[...]) * _silu(gate[:, sl])
        y += jnp.dot(oh.astype(BF16), w_ref[hd * GLA_DV:(hd + 1) * GLA_DV, :], preferred_element_type=F32)
    y += jnp.dot(oq_ref[...], w_ref[GLA_V:GLA_V + GQA_Q, :], preferred_element_type=F32)
    y += jnp.dot(od_ref[...], w_ref[GLA_V + GQA_Q:, :], preferred_element_type=F32)
    o_ref[...] = s_ref[...] + mod_ref[2:3, :] * y


def _output_projection(s, mod, o_gla, gate, o_gqa, o_diff, gla_norm_g, w_out_b):
    n, d = s.shape
    nt = n // ROW_TILE
    row = lambda w: pl.BlockSpec((ROW_TILE, w), lambda i: (i, 0))
    return pl.pallas_call(
        _outproj_kernel,
        grid=(nt,),
        in_specs=[row(d),
                  pl.BlockSpec((None, 8, d), lambda i: (_row_group(i), 0, 0)),
                  pl.BlockSpec((2, ROW_TILE, GLA_V), lambda i: (0, i, 0)),
                  row(GLA_V), row(GQA_Q), row(DIFF_V),
                  _resident((1, GLA_DV)),
                  _resident((MIX_WIDTH, d))],
        out_specs=row(d),
        out_shape=jax.ShapeDtypeStruct((n, d), F32),
        compiler_params=_params("parallel"),
        name="output_projection",
    )(s, mod, o_gla, gate, o_gqa, o_diff, gla_norm_g.reshape(1, -1), w_out_b)


def _router_kernel(n_experts, s_ref, mod_ref, g_ref, rw_ref, rb_ref, h_ref, gates_ref):
    h = _rms(s_ref[...], g_ref[...]) * (1.0 + mod_ref[4:5, :]) + mod_ref[3:4, :]
    hb = h.astype(BF16)
    h_ref[...] = hb
    logits = jnp.dot(hb, rw_ref[...], preferred_element_type=F32)
    scores = 1.0 / (1.0 + jnp.exp(-logits))
    lane = lax.broadcasted_iota(jnp.int32, scores.shape, 1)
    lane_f = lane.astype(F32)
    cand = jnp.where(lane < n_experts, scores + rb_ref[...], -jnp.inf)
    picked = jnp.zeros(scores.shape, F32)
    for _ in range(TOP_K):
        best = jnp.max(cand, axis=-1, keepdims=True)
        first = jnp.min(jnp.where(cand == best, lane_f, float(LANES)), axis=-1, keepdims=True)
        hit = lane_f == first
        picked = jnp.where(hit, scores, picked)
        cand = jnp.where(hit, -jnp.inf, cand)
    gates_ref[...] = picked / jnp.sum(picked, axis=-1, keepdims=True) * ROUTE_SCALE


def _router(s, mod, norm_g, router_w_b, router_b, n_experts):
    n, d = s.shape
    nt = n // ROW_TILE
    row = lambda w: pl.BlockSpec((ROW_TILE, w), lambda i: (i, 0))
    return pl.pallas_call(
        functools.partial(_router_kernel, n_experts),
        grid=(nt,),
        in_specs=[row(d),
                  pl.BlockSpec((None, 8, d), lambda i: (_row_group(i), 0, 0)),
                  _resident((1, d)),
                  _resident((d, LANES)),
                  _resident((1, LANES))],
        out_specs=[row(d), row(LANES)],
        out_shape=[jax.ShapeDtypeStruct((n, d), BF16), jax.ShapeDtypeStruct((n, LANES), F32)],
        compiler_params=_params("parallel"),
        name="router",
    )(s, mod, norm_g.reshape(1, d), router_w_b, router_b)


def _experts_kernel(n_experts, n_ctx_rows, h_ref, gates_ref, wg_ref, wu_ref, wd_ref, s_ref, mod_ref,
                    o_ref, acc_ref):
    e = pl.program_id(1)

    @pl.when(e == 0)
    def _():
        acc_ref[...] = jnp.zeros_like(acc_ref)

    t = h_ref[...]
    a = jnp.dot(t, wg_ref[...], preferred_element_type=F32)
    b = jnp.dot(t, wu_ref[...], preferred_element_type=F32)
    gates = gates_ref[...]
    lane = lax.broadcasted_iota(jnp.int32, gates.shape, 1)
    g = jnp.where(e < n_experts,
                  jnp.sum(jnp.where(lane == e, gates, 0.0), axis=-1, keepdims=True), 1.0)
    mid = (_silu(a) * b * g).astype(BF16)
    acc_ref[...] += jnp.dot(mid, wd_ref[...], preferred_element_type=F32)

    @pl.when(e == n_experts)
    def _():
        rows = s_ref.shape[0]
        r = pl.program_id(0) * rows + lax.broadcasted_iota(jnp.int32, (rows, 1), 0)
        gate2 = jnp.where(r < n_ctx_rows, mod_ref[1, 5:6, :], mod_ref[0, 5:6, :])
        o_ref[...] = s_ref[...] + gate2 * acc_ref[...]


def _experts(h, gates, wg, wu, wd, s, mod, n_ctx_rows):
    n, d = s.shape
    n_all, _, hidden = wg.shape
    tm = MOE_ROW_TILE
    row = lambda w: pl.BlockSpec((tm, w), lambda i, e: (i, 0))
    return pl.pallas_call(
        functools.partial(_experts_kernel, n_all - 1, n_ctx_rows),
        grid=(n // tm, n_all),
        in_specs=[row(d), row(LANES),
                  pl.BlockSpec((None, d, hidden), lambda i, e: (e, 0, 0)),
                  pl.BlockSpec((None, d, hidden), lambda i, e: (e, 0, 0)),
                  pl.BlockSpec((None, hidden, d), lambda i, e: (e, 0, 0)),
                  row(d),
                  pl.BlockSpec((2, 8, d), lambda i, e: (0, 0, 0))],
        out_specs=row(d),
        out_shape=jax.ShapeDtypeStruct((n, d), F32),
        scratch_shapes=[pltpu.VMEM((tm, d), F32)],
        compiler_params=_params("parallel", "arbitrary"),
        name="experts",
    )(h, gates, wg, wu, wd, s, mod)


def _final_norm_kernel(s_ref, g_ref, o_ref):
    o_ref[...] = _rms(s_ref[...], g_ref[...])


def _final_norm(s, g, first_tile):
    n, d = s.shape
    nt = n // ROW_TILE - first_tile
    return pl.pallas_call(
        _final_norm_kernel,
        grid=(nt,),
        in_specs=[pl.BlockSpec((ROW_TILE, d), lambda i: (i + first_tile, 0)), _resident((1, d))],
        out_specs=pl.BlockSpec((ROW_TILE, d), lambda i: (i, 0)),
        out_shape=jax.ShapeDtypeStruct((nt * ROW_TILE, d), F32),
        compiler_params=_params("parallel"),
        name="final_norm",
    )(s, g.reshape(1, d))


def _rope_tables(n_ctx, n_tokens, dim):
    half = dim // 2
    inv_freq = ROPE_THETA ** (-jnp.arange(0, half, 2, dtype=F32) / half)
    t = jnp.arange(n_tokens, dtype=jnp.int32)
    ang_r = (t // GRID_W).astype(F32)[:, None] * inv_freq
    ang_c = (t % GRID_W).astype(F32)[:, None] * inv_freq
    cos = jnp.concatenate([jnp.cos(ang_r)] * 2 + [jnp.cos(ang_c)] * 2, axis=-1)
    sin = jnp.concatenate([-jnp.sin(ang_r), jnp.sin(ang_r), -jnp.sin(ang_c), jnp.sin(ang_c)], axis=-1)
    reps = LANES // dim
    cos = jnp.tile(cos, (1, reps))
    sin = jnp.tile(sin, (1, reps))
    cos = jnp.concatenate([jnp.ones((n_ctx, LANES), F32), cos], axis=0)
    sin = jnp.concatenate([jnp.zeros((n_ctx, LANES), F32), sin], axis=0)
    return cos, sin


def kernel(x, c, ctx, c_ctx, norm1_g, norm2_g, w_mod, b_mod, w_in, gla_wa_f, gla_ba_f, gla_wa_b,
           gla_ba_b, gla_norm_g, q_norm_g, k_norm_g, diff_lq1, diff_lk1, diff_lq2, diff_lk2,
           diff_norm_g, w_out, router_w, router_b, exp_w_gate, exp_w_up, exp_w_down,
           sh_w_gate, sh_w_up, sh_w_down, final_g):
    batch, n_tokens, d = x.shape
    n_ctx = ctx.shape[1]
    depth = w_mod.shape[0]
    n_experts = router_w.shape[-1]
    assert batch == 1 and n_ctx == ROW_TILE and n_tokens % ROW_TILE == 0
    assert (n_ctx + n_tokens) % MOE_ROW_TILE == 0 and TOP_K <= n_experts <= LANES

    s = jnp.concatenate([ctx[0], x[0]], axis=0)
    mod_all = _modulation(c, c_ctx, w_mod, b_mod)
    tables = _rope_tables(n_ctx, n_tokens, HEAD_DIM) + _rope_tables(n_ctx, n_tokens, DIFF_DQK)

    for l in range(depth):
        first_q_tile = 0
        lam_init = 0.8 - 0.6 * math.exp(-0.3 * l)
        mod = mod_all[l]

        w_in_b = jnp.concatenate(
            [w_in[l][:, :ORIG_LR], w_in[l][:, ORIG_LR + 2 * GLA_GATE_RANK:],
             w_in[l][:, ORIG_LR:ORIG_LR + 2 * GLA_GATE_RANK],
             jnp.zeros((d, LANES - 2 * GLA_GATE_RANK), F32)], axis=1).astype(BF16)
        wa = jnp.zeros((LANES, 2 * GLA_QK), F32)
        wa = wa.at[:GLA_GATE_RANK, :GLA_QK].set(gla_wa_f[l])
        wa = wa.at[GLA_GATE_RANK:2 * GLA_GATE_RANK, GLA_QK:].set(gla_wa_b[l]).astype(BF16)
        ba = jnp.concatenate([gla_ba_f[l], gla_ba_b[l]]).reshape(1, -1)
        lamv = jnp.zeros((8, LANES), F32)
        for r, vec in enumerate((diff_lq1[l], diff_lk1[l], diff_lq2[l], diff_lk2[l])):
            lamv = lamv.at[r, :DIFF_DQK].set(vec)
        rw = jnp.pad(router_w[l], ((0, 0), (0, LANES - n_experts))).astype(BF16)
        rb = jnp.pad(router_b[l], (0, LANES - n_experts)).reshape(1, LANES)
        wg = jnp.concatenate([exp_w_gate[l], sh_w_gate[l][None]], axis=0).astype(BF16)
        wu = jnp.concatenate([exp_w_up[l], sh_w_up[l][None]], axis=0).astype(BF16)
        wd = jnp.concatenate([exp_w_down[l], sh_w_down[l][None]], axis=0).astype(BF16)

        (glaq, glak, glav, gate, la, q, k, v, dq, dk, dv) = _input_projection(
            s, mod, norm1_g[l], w_in_b, wa, ba, q_norm_g[l], k_norm_g[l], tables)
        o_gla = _gla(glaq, glak, glav, la)
        o_gqa = _gqa_attention(q, k, v, first_q_tile)
        o_diff = _diff_attention(dq, dk, dv, lamv, diff_norm_g[l], lam_init, first_q_tile)
        s = _output_projection(s, mod, o_gla, gate, o_gqa, o_diff, gla_norm_g[l], w_out[l].astype(BF16))
        h2, gates = _router(s, mod, norm2_g[l], rw, rb, n_experts)
        s = _experts(h2, gates, wg, wu, wd, s, mod, n_ctx)

    out = _final_norm(s, final_g, n_ctx // ROW_TILE)
    return out.reshape(batch, n_tokens, d)
```

```python
import functools
import math

import jax
import jax.numpy as jnp
from jax import lax
from jax.experimental import pallas as pl
from jax.experimental.pallas import tpu as pltpu

F32 = jnp.float32
BF16 = jnp.bfloat16

GRID_W = 64
HEAD_DIM = 128
GLA_HEADS = 4
GLA_DK = 64
GLA_DV = 128
GLA_GATE_RANK = 16
GLA_TAU = 16.0
GLA_CHUNK = 64
GQA_HEADS = 8
GQA_KV_HEADS = 2
DIFF_HEADS = 4
DIFF_DQK = 64
DIFF_DV = 128
TOP_K = 8
ROUTE_SCALE = 2.5
ROPE_THETA = 10000.0
NORM_EPS = 1e-6
LOG2_E = math.log2(math.e)

LANES = 128
ROW_TILE = 256
MOE_ROW_TILE = 768
VMEM_LIMIT = 56 * 1024 * 1024

GLA_QK = GLA_HEADS * GLA_DK
GLA_V = GLA_HEADS * GLA_DV
GQA_Q = GQA_HEADS * HEAD_DIM
GQA_KV = GQA_KV_HEADS * HEAD_DIM
DIFF_QK = DIFF_HEADS * 2 * DIFF_DQK
DIFF_V = DIFF_HEADS * DIFF_DV
MIX_WIDTH = GLA_V + GQA_Q + DIFF_V

C_GLAQ = 0
C_GLAK = C_GLAQ + GLA_QK
C_GLAV = C_GLAK + GLA_QK
C_GATE = C_GLAV + GLA_V
C_GQAQ = C_GATE + GLA_V
C_GQAK = C_GQAQ + GQA_Q
C_GQAV = C_GQAK + GQA_KV
C_DQ = C_GQAV + GQA_KV
C_DK = C_DQ + DIFF_QK
C_DV = C_DK + DIFF_QK
C_LR = C_DV + DIFF_V
IN_COLS = C_LR + LANES
ORIG_LR = 2 * GLA_QK + 2 * GLA_V


def _params(*sem):
    return pltpu.CompilerParams(dimension_semantics=sem, vmem_limit_bytes=VMEM_LIMIT)


def _resident(shape):
    nd = len(shape)
    return pl.BlockSpec(shape, lambda *_: (0,) * nd, pipeline_mode=pl.Buffered(1))


def _silu(a):
    return a / (1.0 + jnp.exp(-a))


def _rms(x, g):
    return x * lax.rsqrt(jnp.mean(x * x, axis=-1, keepdims=True) + NORM_EPS) * g


def _row_group(i):
    return jnp.where(i == 0, 1, 0)


def _mod_kernel(a_ref, w_ref, b_ref, o_ref):
    a = _silu(a_ref[...])
    o_ref[...] = jnp.dot(a.astype(BF16), w_ref[...].astype(BF16),
                         preferred_element_type=F32) + b_ref[...]


def _modulation(c, c_ctx, w_mod, b_mod):
    depth, d, six_d = w_mod.shape
    a = jnp.zeros((8, d), F32).at[0].set(c[0]).at[1].set(c_ctx)
    tn = d // 2
    out = pl.pallas_call(
        _mod_kernel,
        grid=(depth, six_d // tn),
        in_specs=[pl.BlockSpec((8, d), lambda l, j: (0, 0)),
                  pl.BlockSpec((None, d, tn), lambda l, j: (l, 0, j)),
                  pl.BlockSpec((None, 1, tn), lambda l, j: (l, 0, j))],
        out_specs=pl.BlockSpec((None, 8, tn), lambda l, j: (l, 0, j)),
        out_shape=jax.ShapeDtypeStruct((depth, 8, six_d), F32),
        compiler_params=_params("parallel", "parallel"),
        name="modulation",
    )(a, w_mod, b_mod.reshape(depth, 1, six_d))
    m = out[:, :2].reshape(depth, 2, 6, d)
    return jnp.pad(m, ((0, 0), (0, 0), (0, 2), (0, 0)))


def _rope(xh, cos, sin, first, shift_first, shift_second):
    partner = jnp.where(first, pltpu.roll(xh, shift_first, 1), pltpu.roll(xh, shift_second, 1))
    return xh * cos + partner * sin


def _inproj_kernel(x_ref, mod_ref, g_ref, w_ref, wa_ref, ba_ref, qg_ref, kg_ref,
                   cg_ref, sg_ref, cd_ref, sd_ref,
                   glaq_ref, glak_ref, glav_ref, gate_ref, la_ref,
                   qt_ref, k_ref, vt_ref, dqt_ref, dk_ref, dvt_ref):
    x = x_ref[...]
    h = _rms(x, g_ref[...]) * (1.0 + mod_ref[1:2, :]) + mod_ref[0:1, :]
    hb = h.astype(BF16)

    def proj(start, width):
        return jnp.dot(hb, w_ref[:, start:start + width], preferred_element_type=F32)

    glaq_ref[...] = proj(C_GLAQ, GLA_QK) * (GLA_DK ** -0.5)
    glak_ref[...] = proj(C_GLAK, GLA_QK)
    glav_ref[...] = proj(C_GLAV, GLA_V)
    gate_ref[...] = proj(C_GATE, GLA_V)
    zv = proj(C_GQAV, GQA_KV)
    for hd in range(GQA_KV_HEADS):
        sl = slice(hd * HEAD_DIM, (hd + 1) * HEAD_DIM)
        vt_ref[sl, :] = zv[:, sl].T.astype(BF16)
    zdv = proj(C_DV, DIFF_V)
    for hd in range(DIFF_HEADS):
        sl = slice(hd * DIFF_DV, (hd + 1) * DIFF_DV)
        dvt_ref[sl, :] = zdv[:, sl].T.astype(BF16)

    z_lr = proj(C_LR, LANES).astype(BF16)
    pre = jnp.dot(z_lr, wa_ref[...], preferred_element_type=F32) + ba_ref[...]
    log_sig = -(jnp.maximum(-pre, 0.0) + jnp.log1p(jnp.exp(-jnp.abs(pre))))
    la = log_sig * (1.0 / GLA_TAU)
    la_ref[0] = la[:, :GLA_QK]
    la_ref[1] = la[:, GLA_QK:]

    rows = x.shape[0]
    lane = lax.broadcasted_iota(jnp.int32, (rows, LANES), 1)
    first_g = (lane % 64) < 32
    first_d = (lane % 32) < 16
    cg, sg, cd, sd = cg_ref[...], sg_ref[...], cd_ref[...], sd_ref[...]
    scale_g = HEAD_DIM ** -0.5 * LOG2_E
    scale_d = DIFF_DQK ** -0.5 * LOG2_E

    zq = proj(C_GQAQ, GQA_Q)
    for hd in range(GQA_HEADS):
        sl = slice(hd * HEAD_DIM, (hd + 1) * HEAD_DIM)
        qh = _rope(_rms(zq[:, sl], qg_ref[...]), cg, sg, first_g, 96, 32) * scale_g
        qt_ref[sl, :] = qh.T.astype(BF16)
    zk = proj(C_GQAK, GQA_KV)
    for hd in range(GQA_KV_HEADS):
        kh = _rms(zk[:, hd * HEAD_DIM:(hd + 1) * HEAD_DIM], kg_ref[...])
        k_ref[:, hd * HEAD_DIM:(hd + 1) * HEAD_DIM] = _rope(kh, cg, sg, first_g, 96, 32).astype(BF16)
    zdq = proj(C_DQ, DIFF_QK)
    zdk = proj(C_DK, DIFF_QK)
    for hd in range(DIFF_HEADS):
        sl = slice(hd * LANES, (hd + 1) * LANES)
        dqt_ref[sl, :] = (_rope(zdq[:, sl], cd, sd, first_d, 112, 16) * scale_d).T.astype(BF16)
        dk_ref[:, sl] = _rope(zdk[:, sl], cd, sd, first_d, 112, 16).astype(BF16)


def _input_projection(s, mod, norm_g, w_in_b, wa, ba, q_norm_g, k_norm_g, tables):
    n, d = s.shape
    nt = n // ROW_TILE
    row = lambda w: pl.BlockSpec((ROW_TILE, w), lambda i: (i, 0))
    col = lambda w: pl.BlockSpec((w, ROW_TILE), lambda i: (0, i))
    tile_t = lambda w: pl.BlockSpec((None, w, ROW_TILE), lambda i: (i, 0, 0))
    f32o = lambda w: jax.ShapeDtypeStruct((n, w), F32)
    bfo = lambda w: jax.ShapeDtypeStruct((n, w), BF16)
    return pl.pallas_call(
        _inproj_kernel,
        grid=(nt,),
        in_specs=[row(d),
                  pl.BlockSpec((None, 8, d), lambda i: (_row_group(i), 0, 0)),
                  _resident((1, d)),
                  _resident((d, IN_COLS)),
                  _resident((LANES, 2 * GLA_QK)),
                  _resident((1, 2 * GLA_QK)),
                  _resident((1, HEAD_DIM)),
                  _resident((1, HEAD_DIM)),
                  row(LANES), row(LANES), row(LANES), row(LANES)],
        out_specs=[row(GLA_QK), row(GLA_QK), row(GLA_V), row(GLA_V),
                   pl.BlockSpec((2, ROW_TILE, GLA_QK), lambda i: (0, i, 0)),
                   col(GQA_Q), row(GQA_KV), tile_t(GQA_KV), col(DIFF_QK), row(DIFF_QK), tile_t(DIFF_V)],
        out_shape=[f32o(GLA_QK), f32o(GLA_QK), f32o(GLA_V), f32o(GLA_V),
                   jax.ShapeDtypeStruct((2, n, GLA_QK), F32),
                   jax.ShapeDtypeStruct((GQA_Q, n), BF16), bfo(GQA_KV),
                   jax.ShapeDtypeStruct((nt, GQA_KV, ROW_TILE), BF16),
                   jax.ShapeDtypeStruct((DIFF_QK, n), BF16), bfo(DIFF_QK),
                   jax.ShapeDtypeStruct((nt, DIFF_V, ROW_TILE), BF16)],
        compiler_params=_params("parallel"),
        name="input_projection",
    )(s, mod, norm_g.reshape(1, d), w_in_b, wa, ba, q_norm_g.reshape(1, -1), k_norm_g.reshape(1, -1),
      *tables)


def _gla_block(reverse, q_ref, k_ref, v_ref, la_ref, o_ref, st_ref):
    rows = q_ref.shape[0]
    n_chunks = rows // GLA_CHUNK
    ri = lax.broadcasted_iota(jnp.int32, (rows, rows), 0)
    ci = lax.broadcasted_iota(jnp.int32, (rows, rows), 1)
    same_chunk = (ri // GLA_CHUNK) == (ci // GLA_CHUNK)
    allowed = same_chunk & ((ci >= ri) if reverse else (ci <= ri))
    la = la_ref[...]
    cum = jnp.dot(jnp.where(allowed, 1.0, 0.0).astype(F32), la,
                  preferred_element_type=F32, precision=lax.Precision.HIGHEST)
    tot = jnp.dot(jnp.where(same_chunk, 1.0, 0.0).astype(F32), la,
                  preferred_element_type=F32, precision=lax.Precision.HIGHEST)
    k = k_ref[...]
    q_dec = q_ref[...] * jnp.exp(cum)
    k_inv = (k * jnp.exp(-cum)).astype(BF16)
    k_end = k * jnp.exp(tot - cum)
    v = v_ref[...]
    vb = v.astype(BF16)
    v_t = v.T.astype(BF16)
    lane_head = lax.broadcasted_iota(jnp.int32, (rows, GLA_QK), 1) // GLA_DK
    row_chunk = lax.broadcasted_iota(jnp.int32, (rows, GLA_QK), 0) // GLA_CHUNK
    nt_dims = (((1,), (1,)), ((), ()))

    for hd in range(GLA_HEADS):
        qh = jnp.where(lane_head == hd, q_dec, 0.0).astype(BF16)
        a = lax.dot_general(qh, k_inv, nt_dims, preferred_element_type=F32)
        a = jnp.where(allowed, a, 0.0).astype(BF16)
        o_ref[:, hd * GLA_DV:(hd + 1) * GLA_DV] = jnp.dot(
            a, vb[:, hd * GLA_DV:(hd + 1) * GLA_DV], preferred_element_type=F32)

    chunk_order = range(n_chunks - 1, -1, -1) if reverse else range(n_chunks)
    chunk_lane_head = lax.broadcasted_iota(jnp.int32, (GLA_CHUNK, GLA_QK), 1) // GLA_DK
    for c in chunk_order:
        sl = slice(c * GLA_CHUNK, (c + 1) * GLA_CHUNK)
        state = st_ref[...]
        q4 = jnp.concatenate(
            [jnp.where(chunk_lane_head == hd, q_dec[sl], 0.0) for hd in range(GLA_HEADS)],
            axis=0).astype(BF16)
        r = lax.dot_general(q4, state.astype(BF16), nt_dims, preferred_element_type=F32)
        for hd in range(GLA_HEADS):
            o_ref[sl, hd * GLA_DV:(hd + 1) * GLA_DV] += r[hd * GLA_CHUNK:(hd + 1) * GLA_CHUNK,
                                                          hd * GLA_DV:(hd + 1) * GLA_DV]
        k_end_c = jnp.where(row_chunk == c, k_end, 0.0).astype(BF16)
        u_t = jnp.dot(v_t, k_end_c, preferred_element_type=F32)
        st_ref[...] = state * jnp.exp(tot[c * GLA_CHUNK:c * GLA_CHUNK + 1]) + u_t


def _gla_kernel(q_ref, k_ref, v_ref, la_ref, o_ref, st_ref):
    @pl.when(pl.program_id(1) == 0)
    def _():
        st_ref[...] = jnp.zeros_like(st_ref)

    @pl.when(pl.program_id(0) == 0)
    def _():
        _gla_block(False, q_ref, k_ref, v_ref, la_ref, o_ref, st_ref)

    @pl.when(pl.program_id(0) == 1)
    def _():
        _gla_block(True, q_ref, k_ref, v_ref, la_ref, o_ref, st_ref)


def _gla(glaq, glak, glav, la):
    n = glaq.shape[0]
    nt = n // ROW_TILE

    def blk(dr, j):
        return jnp.where(dr == 0, j, jnp.where(j == 0, 0, nt - j))

    row = lambda w: pl.BlockSpec((ROW_TILE, w), lambda dr, j: (blk(dr, j), 0))
    return pl.pallas_call(
        _gla_kernel,
        grid=(2, nt),
        in_specs=[row(GLA_QK), row(GLA_QK), row(GLA_V),
                  pl.BlockSpec((None, ROW_TILE, GLA_QK), lambda dr, j: (dr, blk(dr, j), 0))],
        out_specs=pl.BlockSpec((None, ROW_TILE, GLA_V), lambda dr, j: (dr, blk(dr, j), 0)),
        out_shape=jax.ShapeDtypeStruct((2, n, GLA_V), F32),
        scratch_shapes=[pltpu.VMEM((GLA_V, GLA_QK), F32)],
        compiler_params=_params("arbitrary", "arbitrary"),
        name="gla_scan",
    )(glaq, glak, glav, la)


class _FlashMaps:
    def __init__(self, q_maps, k_of_map, load_k, load_vt):
        self.q_maps, self.k_of_map, self.load_k, self.load_vt = q_maps, k_of_map, load_k, load_vt


def _flash_steps(maps, first_tile, tiles_per_step, n_steps, s_ref, m_ref, l_ref, acc_ref):
    n_maps = len(maps.q_maps)
    rows = tiles_per_step * ROW_TILE
    for u in range(n_steps):
        first_row = pl.multiple_of((first_tile + u * tiles_per_step) * ROW_TILE, ROW_TILE)
        keys = {src: maps.load_k(src, first_row, rows) for src in sorted(set(maps.k_of_map))}
        for j in range(n_maps):
            s_ref[u * n_maps + j, :rows] = jnp.dot(keys[maps.k_of_map[j]], maps.q_maps[j],
                                                   preferred_element_type=F32)
    for u in range(n_steps):
        tile = first_tile + u * tiles_per_step
        for j in range(n_maps):
            s_t = s_ref[u * n_maps + j, :rows]
            m_prev = m_ref[j]
            m_new = jnp.maximum(m_prev, jnp.max(s_t, axis=0, keepdims=True))
            alpha = jnp.exp2(m_prev - m_new)
            p = jnp.exp2(s_t - m_new)
            l_ref[j] = alpha * l_ref[j] + jnp.sum(p, axis=0, keepdims=True)
            pb = p.astype(BF16)
            pv = jnp.dot(maps.load_vt(maps.k_of_map[j], tile), pb[:ROW_TILE], preferred_element_type=F32)
            for r in range(1, tiles_per_step):
                pv += jnp.dot(maps.load_vt(maps.k_of_map[j], tile + r), pb[r * ROW_TILE:(r + 1) * ROW_TILE],
                              preferred_element_type=F32)
            acc_ref[j] = alpha * acc_ref[j] + pv
            m_ref[j] = m_new


def _flash_attend(maps, first_q_tile, n_key_tiles, tiles_per_step, n_steps, s_ref, m_ref, l_ref, acc_ref):
    m_ref[...] = jnp.full(m_ref.shape, -jnp.inf, F32)
    l_ref[...] = jnp.zeros(l_ref.shape, F32)
    acc_ref[...] = jnp.zeros(acc_ref.shape, F32)
    _flash_steps(maps, 0, 1, 1, s_ref, m_ref, l_ref, acc_ref)
    per_iter = tiles_per_step * n_steps
    qi = pl.program_id(1) + first_q_tile
    n_iter = jnp.where(qi == 0, 0, (n_key_tiles - 1) // per_iter)

    def body(it, carry):
        _flash_steps(maps, 1 + it * per_iter, tiles_per_step, n_steps, s_ref, m_ref, l_ref, acc_ref)
        return carry

    lax.fori_loop(0, n_iter, body, 0)


def _flash_plan(n_key_tiles):
    latent = n_key_tiles - 1
    tiles_per_step = 2 if latent % 2 == 0 else 1
    n_steps = 2 if latent % (2 * tiles_per_step) == 0 else 1
    return tiles_per_step, n_steps


def _flash_scratch(n_maps, dv, plan):
    tiles_per_step, n_steps = plan
    return [pltpu.VMEM((n_steps * n_maps, tiles_per_step * ROW_TILE, ROW_TILE), F32),
            pltpu.VMEM((n_maps, 1, ROW_TILE), F32), pltpu.VMEM((n_maps, 1, ROW_TILE), F32),
            pltpu.VMEM((n_maps, dv, ROW_TILE), F32)]


GQA_GROUP = GQA_HEADS // GQA_KV_HEADS


def _gqa_kernel(first_q_tile, plan, qt_ref, k_ref, vt_ref, o_ref, s_ref, m_ref, l_ref, acc_ref):
    maps = _FlashMaps(
        q_maps=[qt_ref[hd * HEAD_DIM:(hd + 1) * HEAD_DIM, :] for hd in range(GQA_GROUP)],
        k_of_map=[0] * GQA_GROUP,
        load_k=lambda src, first_row, rows: k_ref[pl.ds(first_row, rows), :],
        load_vt=lambda src, tile: vt_ref[tile])
    _flash_attend(maps, first_q_tile, vt_ref.shape[0], *plan, s_ref, m_ref, l_ref, acc_ref)
    for hd in range(GQA_GROUP):
        o_t = acc_ref[hd] / l_ref[hd]
        o_ref[:, hd * HEAD_DIM:(hd + 1) * HEAD_DIM] = o_t.T.astype(o_ref.dtype)


def _gqa_attention(q_t, k, v_t, first_q_tile):
    n = k.shape[0]
    nt = n // ROW_TILE
    nq = nt - first_q_tile
    gw = GQA_GROUP * HEAD_DIM
    plan = _flash_plan(nt)
    return pl.pallas_call(
        functools.partial(_gqa_kernel, first_q_tile, plan),
        grid=(GQA_KV_HEADS, nq),
        in_specs=[pl.BlockSpec((gw, ROW_TILE), lambda g, i: (g, i + first_q_tile)),
                  pl.BlockSpec((n, HEAD_DIM), lambda g, i: (0, g)),
                  pl.BlockSpec((nt, HEAD_DIM, ROW_TILE), lambda g, i: (0, g, 0))],
        out_specs=pl.BlockSpec((ROW_TILE, gw), lambda g, i: (i + first_q_tile, g)),
        out_shape=jax.ShapeDtypeStruct((n, GQA_Q), BF16),
        scratch_shapes=_flash_scratch(GQA_GROUP, HEAD_DIM, plan),
        compiler_params=_params("parallel", "arbitrary"),
        name="gqa_attention",
    )(q_t, k, v_t)


DIFF_PAIR = 2


def _diff_kernel(first_q_tile, plan, lam_init, qt_ref, k_ref, vt_ref, lamv_ref, g_ref, o_ref,
                 s_ref, m_ref, l_ref, acc_ref):
    row = lax.broadcasted_iota(jnp.int32, (LANES, ROW_TILE), 0)
    q_maps = []
    for hd in range(DIFF_PAIR):
        q_t = qt_ref[hd * LANES:(hd + 1) * LANES, :]
        zero = jnp.zeros_like(q_t)
        q_maps += [jnp.where(row < DIFF_DQK, q_t, zero), jnp.where(row >= DIFF_DQK, q_t, zero)]
    maps = _FlashMaps(
        q_maps=q_maps,
        k_of_map=[hd for hd in range(DIFF_PAIR) for _ in range(2)],
        load_k=lambda src, first_row, rows: k_ref[pl.ds(first_row, rows), src * LANES:(src + 1) * LANES],
        load_vt=lambda src, tile: vt_ref[tile, src * DIFF_DV:(src + 1) * DIFF_DV, :])
    _flash_attend(maps, first_q_tile, vt_ref.shape[0], *plan, s_ref, m_ref, l_ref, acc_ref)
    lv = lamv_ref[...]
    lam = (jnp.exp(jnp.sum(lv[0:1] * lv[1:2], axis=-1, keepdims=True))
           - jnp.exp(jnp.sum(lv[2:3] * lv[3:4], axis=-1, keepdims=True)) + lam_init)
    for hd in range(DIFF_PAIR):
        o_t = acc_ref[2 * hd] / l_ref[2 * hd] - lam * (acc_ref[2 * hd + 1] / l_ref[2 * hd + 1])
        o = _rms(o_t.T, g_ref[...]) * (1.0 - lam_init)
        o_ref[:, hd * DIFF_DV:(hd + 1) * DIFF_DV] = o.astype(o_ref.dtype)


def _diff_attention(q_t, k, v_t, lamv, norm_g, lam_init, first_q_tile):
    n = k.shape[0]
    nt = n // ROW_TILE
    nq = nt - first_q_tile
    pw = DIFF_PAIR * LANES
    plan = _flash_plan(nt)
    return pl.pallas_call(
        functools.partial(_diff_kernel, first_q_tile, plan, lam_init),
        grid=(DIFF_HEADS // DIFF_PAIR, nq),
        in_specs=[pl.BlockSpec((pw, ROW_TILE), lambda h, i: (h, i + first_q_tile)),
                  pl.BlockSpec((n, pw), lambda h, i: (0, h)),
                  pl.BlockSpec((nt, DIFF_PAIR * DIFF_DV, ROW_TILE), lambda h, i: (0, h, 0)),
                  pl.BlockSpec((8, LANES), lambda h, i: (0, 0)),
                  pl.BlockSpec((1, DIFF_DV), lambda h, i: (0, 0))],
        out_specs=pl.BlockSpec((ROW_TILE, DIFF_PAIR * DIFF_DV), lambda h, i: (i + first_q_tile, h)),
        out_shape=jax.ShapeDtypeStruct((n, DIFF_V), BF16),
        scratch_shapes=_flash_scratch(2 * DIFF_PAIR, DIFF_DV, plan),
        compiler_params=_params("parallel", "arbitrary"),
        name="diff_attention",
    )(q_t, k, v_t, lamv, norm_g.reshape(1, -1))


def _outproj_kernel(s_ref, mod_ref, og_ref, gate_ref, oq_ref, od_ref, gg_ref, w_ref, o_ref):
    og = og_ref[0] + og_ref[1]
    gate = gate_ref[...]
    y = jnp.zeros(s_ref.shape, F32)
    for hd in range(GLA_HEADS):
        sl = slice(hd * GLA_DV, (hd + 1) * GLA_DV)
        oh = _rms(og[:, sl], gg_ref[...]) * _silu(gate[:, sl])
        y += jnp.dot(oh.astype(BF16), w_ref[hd * GLA_DV:(hd + 1) * GLA_DV, :], preferred_element_type=F32)
    y += jnp.dot(oq_ref[...], w_ref[GLA_V:GLA_V + GQA_Q, :], preferred_element_type=F32)
    y += jnp.dot(od_ref[...], w_ref[GLA_V + GQA_Q:, :], preferred_element_type=F32)
    o_ref[...] = s_ref[...] + mod_ref[2:3, :] * y


def _output_projection(s, mod, o_gla, gate, o_gqa, o_diff, gla_norm_g, w_out_b):
    n, d = s.shape
    nt = n // ROW_TILE
    row = lambda w: pl.BlockSpec((ROW_TILE, w), lambda i: (i, 0))
    return pl.pallas_call(
        _outproj_kernel,
        grid=(nt,),
        in_specs=[row(d),
                  pl.BlockSpec((None, 8, d), lambda i: (_row_group(i), 0, 0)),
                  pl.BlockSpec((2, ROW_TILE, GLA_V), lambda i: (0, i, 0)),
                  row(GLA_V), row(GQA_Q), row(DIFF_V),
                  _resident((1, GLA_DV)),
                  _resident((MIX_WIDTH, d))],
        out_specs=row(d),
        out_shape=jax.ShapeDtypeStruct((n, d), F32),
        compiler_params=_params("parallel"),
        name="output_projection",
    )(s, mod, o_gla, gate, o_gqa, o_diff, gla_norm_g.reshape(1, -1), w_out_b)


def _router_kernel(n_experts, s_ref, mod_ref, g_ref, rw_ref, rb_ref, h_ref, gates_ref):
    h = _rms(s_ref[...], g_ref[...]) * (1.0 + mod_ref[4:5, :]) + mod_ref[3:4, :]
    hb = h.astype(BF16)
    h_ref[...] = hb
    logits = jnp.dot(hb, rw_ref[...], preferred_element_type=F32)
    scores = 1.0 / (1.0 + jnp.exp(-logits))
    lane = lax.broadcasted_iota(jnp.int32, scores.shape, 1)
    lane_f = lane.astype(F32)
    cand = jnp.where(lane < n_experts, scores + rb_ref[...], -jnp.inf)
    picked = jnp.zeros(scores.shape, F32)
    for _ in range(TOP_K):
        best = jnp.max(cand, axis=-1, keepdims=True)
        first = jnp.min(jnp.where(cand == best, lane_f, float(LANES)), axis=-1, keepdims=True)
        hit = lane_f == first
        picked = jnp.where(hit, scores, picked)
        cand = jnp.where(hit, -jnp.inf, cand)
    gates_ref[...] = picked / jnp.sum(picked, axis=-1, keepdims=True) * ROUTE_SCALE


def _router(s, mod, norm_g, router_w_b, router_b, n_experts):
    n, d = s.shape
    nt = n // ROW_TILE
    row = lambda w: pl.BlockSpec((ROW_TILE, w), lambda i: (i, 0))
    return pl.pallas_call(
        functools.partial(_router_kernel, n_experts),
        grid=(nt,),
        in_specs=[row(d),
                  pl.BlockSpec((None, 8, d), lambda i: (_row_group(i), 0, 0)),
                  _resident((1, d)),
                  _resident((d, LANES)),
                  _resident((1, LANES))],
        out_specs=[row(d), row(LANES)],
        out_shape=[jax.ShapeDtypeStruct((n, d), BF16), jax.ShapeDtypeStruct((n, LANES), F32)],
        compiler_params=_params("parallel"),
        name="router",
    )(s, mod, norm_g.reshape(1, d), router_w_b, router_b)


def _experts_kernel(n_experts, n_ctx_rows, h_ref, gates_ref, wg_ref, wu_ref, wd_ref, s_ref, mod_ref,
                    o_ref, acc_ref):
    e = pl.program_id(1)

    @pl.when(e == 0)
    def _():
        acc_ref[...] = jnp.zeros_like(acc_ref)

    t = h_ref[...]
    a = jnp.dot(t, wg_ref[...], preferred_element_type=F32)
    b = jnp.dot(t, wu_ref[...], preferred_element_type=F32)
    gates = gates_ref[...]
    lane = lax.broadcasted_iota(jnp.int32, gates.shape, 1)
    g = jnp.where(e < n_experts,
                  jnp.sum(jnp.where(lane == e, gates, 0.0), axis=-1, keepdims=True), 1.0)
    mid = (_silu(a) * b * g).astype(BF16)
    acc_ref[...] += jnp.dot(mid, wd_ref[...], preferred_element_type=F32)

    @pl.when(e == n_experts)
    def _():
        rows = s_ref.shape[0]
        r = pl.program_id(0) * rows + lax.broadcasted_iota(jnp.int32, (rows, 1), 0)
        gate2 = jnp.where(r < n_ctx_rows, mod_ref[1, 5:6, :], mod_ref[0, 5:6, :])
        o_ref[...] = s_ref[...] + gate2 * acc_ref[...]


def _experts(h, gates, wg, wu, wd, s, mod, n_ctx_rows):
    n, d = s.shape
    n_all, _, hidden = wg.shape
    tm = MOE_ROW_TILE
    row = lambda w: pl.BlockSpec((tm, w), lambda i, e: (i, 0))
    return pl.pallas_call(
        functools.partial(_experts_kernel, n_all - 1, n_ctx_rows),
        grid=(n // tm, n_all),
        in_specs=[row(d), row(LANES),
                  pl.BlockSpec((None, d, hidden), lambda i, e: (e, 0, 0)),
                  pl.BlockSpec((None, d, hidden), lambda i, e: (e, 0, 0)),
                  pl.BlockSpec((None, hidden, d), lambda i, e: (e, 0, 0)),
                  row(d),
                  pl.BlockSpec((2, 8, d), lambda i, e: (0, 0, 0))],
        out_specs=row(d),
        out_shape=jax.ShapeDtypeStruct((n, d), F32),
        scratch_shapes=[pltpu.VMEM((tm, d), F32)],
        compiler_params=_params("parallel", "arbitrary"),
        name="experts",
    )(h, gates, wg, wu, wd, s, mod)


def _final_norm_kernel(s_ref, g_ref, o_ref):
    o_ref[...] = _rms(s_ref[...], g_ref[...])


def _final_norm(s, g, first_tile):
    n, d = s.shape
    nt = n // ROW_TILE - first_tile
    return pl.pallas_call(
        _final_norm_kernel,
        grid=(nt,),
        in_specs=[pl.BlockSpec((ROW_TILE, d), lambda i: (i + first_tile, 0)), _resident((1, d))],
        out_specs=pl.BlockSpec((ROW_TILE, d), lambda i: (i, 0)),
        out_shape=jax.ShapeDtypeStruct((nt * ROW_TILE, d), F32),
        compiler_params=_params("parallel"),
        name="final_norm",
    )(s, g.reshape(1, d))


def _rope_tables(n_ctx, n_tokens, dim):
    half = dim // 2
    inv_freq = ROPE_THETA ** (-jnp.arange(0, half, 2, dtype=F32) / half)
    t = jnp.arange(n_tokens, dtype=jnp.int32)
    ang_r = (t // GRID_W).astype(F32)[:, None] * inv_freq
    ang_c = (t % GRID_W).astype(F32)[:, None] * inv_freq
    cos = jnp.concatenate([jnp.cos(ang_r)] * 2 + [jnp.cos(ang_c)] * 2, axis=-1)
    sin = jnp.concatenate([-jnp.sin(ang_r), jnp.sin(ang_r), -jnp.sin(ang_c), jnp.sin(ang_c)], axis=-1)
    reps = LANES // dim
    cos = jnp.tile(cos, (1, reps))
    sin = jnp.tile(sin, (1, reps))
    cos = jnp.concatenate([jnp.ones((n_ctx, LANES), F32), cos], axis=0)
    sin = jnp.concatenate([jnp.zeros((n_ctx, LANES), F32), sin], axis=0)
    return cos, sin


def kernel(x, c, ctx, c_ctx, norm1_g, norm2_g, w_mod, b_mod, w_in, gla_wa_f, gla_ba_f, gla_wa_b,
           gla_ba_b, gla_norm_g, q_norm_g, k_norm_g, diff_lq1, diff_lk1, diff_lq2, diff_lk2,
           diff_norm_g, w_out, router_w, router_b, exp_w_gate, exp_w_up, exp_w_down,
           sh_w_gate, sh_w_up, sh_w_down, final_g):
    batch, n_tokens, d = x.shape
    n_ctx = ctx.shape[1]
    depth = w_mod.shape[0]
    n_experts = router_w.shape[-1]
    assert batch == 1 and n_ctx == ROW_TILE and n_tokens % ROW_TILE == 0
    assert (n_ctx + n_tokens) % MOE_ROW_TILE == 0 and TOP_K <= n_experts <= LANES

    s = jnp.concatenate([ctx[0], x[0]], axis=0)
    mod_all = _modulation(c, c_ctx, w_mod, b_mod)
    tables = _rope_tables(n_ctx, n_tokens, HEAD_DIM) + _rope_tables(n_ctx, n_tokens, DIFF_DQK)

    for l in range(depth):
        first_q_tile = 0
        lam_init = 0.8 - 0.6 * math.exp(-0.3 * l)
        mod = mod_all[l]

        w_in_b = jnp.concatenate(
            [w_in[l][:, :ORIG_LR], w_in[l][:, ORIG_LR + 2 * GLA_GATE_RANK:],
             w_in[l][:, ORIG_LR:ORIG_LR + 2 * GLA_GATE_RANK],
             jnp.zeros((d, LANES - 2 * GLA_GATE_RANK), F32)], axis=1).astype(BF16)
        wa = jnp.zeros((LANES, 2 * GLA_QK), F32)
        wa = wa.at[:GLA_GATE_RANK, :GLA_QK].set(gla_wa_f[l])
        wa = wa.at[GLA_GATE_RANK:2 * GLA_GATE_RANK, GLA_QK:].set(gla_wa_b[l]).astype(BF16)
        ba = jnp.concatenate([gla_ba_f[l], gla_ba_b[l]]).reshape(1, -1)
        lamv = jnp.zeros((8, LANES), F32)
        for r, vec in enumerate((diff_lq1[l], diff_lk1[l], diff_lq2[l], diff_lk2[l])):
            lamv = lamv.at[r, :DIFF_DQK].set(vec)
        rw = jnp.pad(router_w[l], ((0, 0), (0, LANES - n_experts))).astype(BF16)
        rb = jnp.pad(router_b[l], (0, LANES - n_experts)).reshape(1, LANES)
        wg = jnp.concatenate([exp_w_gate[l], sh_w_gate[l][None]], axis=0).astype(BF16)
        wu = jnp.concatenate([exp_w_up[l], sh_w_up[l][None]], axis=0).astype(BF16)
        wd = jnp.concatenate([exp_w_down[l], sh_w_down[l][None]], axis=0).astype(BF16)

        (glaq, glak, glav, gate, la, q_t, k, v_t, dq_t, dk, dv_t) = _input_projection(
            s, mod, norm1_g[l], w_in_b, wa, ba, q_norm_g[l], k_norm_g[l], tables)
        o_gla = _gla(glaq, glak, glav, la)
        o_gqa = _gqa_attention(q_t, k, v_t, first_q_tile)
        o_diff = _diff_attention(dq_t, dk, dv_t, lamv, diff_norm_g[l], lam_init, first_q_tile)
        s = _output_projection(s, mod, o_gla, gate, o_gqa, o_diff, gla_norm_g[l], w_out[l].astype(BF16))
        h2, gates = _router(s, mod, norm2_g[l], rw, rb, n_experts)
        s = _experts(h2, gates, wg, wu, wd, s, mod, n_ctx)

    out = _final_norm(s, final_g, n_ctx // ROW_TILE)
    return out.reshape(batch, n_tokens, d)
```

```python
import functools
import math

import jax
import jax.numpy as jnp
from jax import lax
from jax.experimental import pallas as pl
from jax.experimental.pallas import tpu as pltpu

F32 = jnp.float32
BF16 = jnp.bfloat16

GRID_W = 64
HEAD_DIM = 128
GLA_HEADS = 4
GLA_DK = 64
GLA_DV = 128
GLA_GATE_RANK = 16
GLA_TAU = 16.0
GLA_CHUNK = 64
GQA_HEADS = 8
GQA_KV_HEADS = 2
DIFF_HEADS = 4
DIFF_DQK = 64
DIFF_DV = 128
TOP_K = 8
ROUTE_SCALE = 2.5
ROPE_THETA = 10000.0
NORM_EPS = 1e-6
LOG2_E = math.log2(math.e)

LANES = 128
ROW_TILE = 256
EXPERT_ROW_TILE = 256
COMBINE_ROW_TILE = 128
VMEM_LIMIT = 56 * 1024 * 1024

GLA_QK = GLA_HEADS * GLA_DK
GLA_V = GLA_HEADS * GLA_DV
GQA_Q = GQA_HEADS * HEAD_DIM
GQA_KV = GQA_KV_HEADS * HEAD_DIM
DIFF_QK = DIFF_HEADS * 2 * DIFF_DQK
DIFF_V = DIFF_HEADS * DIFF_DV
MIX_WIDTH = GLA_V + GQA_Q + DIFF_V

C_GLAQ = 0
C_GLAK = C_GLAQ + GLA_QK
C_GLAV = C_GLAK + GLA_QK
C_GATE = C_GLAV + GLA_V
C_GQAQ = C_GATE + GLA_V
C_GQAK = C_GQAQ + GQA_Q
C_GQAV = C_GQAK + GQA_KV
C_DQ = C_GQAV + GQA_KV
C_DK = C_DQ + DIFF_QK
C_DV = C_DK + DIFF_QK
C_LR = C_DV + DIFF_V
IN_COLS = C_LR + LANES
ORIG_LR = 2 * GLA_QK + 2 * GLA_V


def _params(*sem):
    return pltpu.CompilerParams(dimension_semantics=sem, vmem_limit_bytes=VMEM_LIMIT)


def _resident(shape):
    nd = len(shape)
    return pl.BlockSpec(shape, lambda *_: (0,) * nd, pipeline_mode=pl.Buffered(1))


def _silu(a):
    return a / (1.0 + jnp.exp(-a))


def _rms(x, g):
    return x * lax.rsqrt(jnp.mean(x * x, axis=-1, keepdims=True) + NORM_EPS) * g


def _row_group(i):
    return jnp.where(i == 0, 1, 0)


def _mod_kernel(a_ref, w_ref, b_ref, o_ref):
    a = _silu(a_ref[...])
    o_ref[...] = jnp.dot(a.astype(BF16), w_ref[...].astype(BF16),
                         preferred_element_type=F32) + b_ref[...]


def _modulation(c, c_ctx, w_mod, b_mod):
    depth, d, six_d = w_mod.shape
    a = jnp.zeros((8, d), F32).at[0].set(c[0]).at[1].set(c_ctx)
    tn = d // 2
    out = pl.pallas_call(
        _mod_kernel,
        grid=(depth, six_d // tn),
        in_specs=[pl.BlockSpec((8, d), lambda l, j: (0, 0)),
                  pl.BlockSpec((None, d, tn), lambda l, j: (l, 0, j)),
                  pl.BlockSpec((None, 1, tn), lambda l, j: (l, 0, j))],
        out_specs=pl.BlockSpec((None, 8, tn), lambda l, j: (l, 0, j)),
        out_shape=jax.ShapeDtypeStruct((depth, 8, six_d), F32),
        compiler_params=_params("parallel", "parallel"),
        name="modulation",
    )(a, w_mod, b_mod.reshape(depth, 1, six_d))
    m = out[:, :2].reshape(depth, 2, 6, d)
    return jnp.pad(m, ((0, 0), (0, 0), (0, 2), (0, 0)))


def _rope(xh, cos, sin, first, shift_first, shift_second):
    partner = jnp.where(first, pltpu.roll(xh, shift_first, 1), pltpu.roll(xh, shift_second, 1))
    return xh * cos + partner * sin


def _inproj_kernel(x_ref, mod_ref, g_ref, w_ref, wa_ref, ba_ref, qg_ref, kg_ref,
                   cg_ref, sg_ref, cd_ref, sd_ref,
                   glaq_ref, glak_ref, glav_ref, gate_ref, la_ref,
                   qt_ref, k_ref, vt_ref, dqt_ref, dk_ref, dvt_ref):
    x = x_ref[...]
    h = _rms(x, g_ref[...]) * (1.0 + mod_ref[1:2, :]) + mod_ref[0:1, :]
    hb = h.astype(BF16)

    def proj(start, width):
        return jnp.dot(hb, w_ref[:, start:start + width], preferred_element_type=F32)

    glaq_ref[...] = proj(C_GLAQ, GLA_QK) * (GLA_DK ** -0.5)
    glak_ref[...] = proj(C_GLAK, GLA_QK)
    glav_ref[...] = proj(C_GLAV, GLA_V)
    gate_ref[...] = proj(C_GATE, GLA_V)
    zv = proj(C_GQAV, GQA_KV)
    for hd in range(GQA_KV_HEADS):
        sl = slice(hd * HEAD_DIM, (hd + 1) * HEAD_DIM)
        vt_ref[sl, :] = zv[:, sl].T.astype(BF16)
    zdv = proj(C_DV, DIFF_V)
    for hd in range(DIFF_HEADS):
        sl = slice(hd * DIFF_DV, (hd + 1) * DIFF_DV)
        dvt_ref[sl, :] = zdv[:, sl].T.astype(BF16)

    z_lr = proj(C_LR, LANES).astype(BF16)
    pre = jnp.dot(z_lr, wa_ref[...], preferred_element_type=F32) + ba_ref[...]
    log_sig = -(jnp.maximum(-pre, 0.0) + jnp.log1p(jnp.exp(-jnp.abs(pre))))
    la = log_sig * (1.0 / GLA_TAU)
    la_ref[0] = la[:, :GLA_QK]
    la_ref[1] = la[:, GLA_QK:]

    rows = x.shape[0]
    lane = lax.broadcasted_iota(jnp.int32, (rows, LANES), 1)
    first_g = (lane % 64) < 32
    first_d = (lane % 32) < 16
    cg, sg, cd, sd = cg_ref[...], sg_ref[...], cd_ref[...], sd_ref[...]
    scale_g = HEAD_DIM ** -0.5 * LOG2_E
    scale_d = DIFF_DQK ** -0.5 * LOG2_E

    zq = proj(C_GQAQ, GQA_Q)
    for hd in range(GQA_HEADS):
        sl = slice(hd * HEAD_DIM, (hd + 1) * HEAD_DIM)
        qh = _rope(_rms(zq[:, sl], qg_ref[...]), cg, sg, first_g, 96, 32) * scale_g
        qt_ref[sl, :] = qh.T.astype(BF16)
    zk = proj(C_GQAK, GQA_KV)
    for hd in range(GQA_KV_HEADS):
        kh = _rms(zk[:, hd * HEAD_DIM:(hd + 1) * HEAD_DIM], kg_ref[...])
        k_ref[:, hd * HEAD_DIM:(hd + 1) * HEAD_DIM] = _rope(kh, cg, sg, first_g, 96, 32).astype(BF16)
    zdq = proj(C_DQ, DIFF_QK)
    zdk = proj(C_DK, DIFF_QK)
    for hd in range(DIFF_HEADS):
        sl = slice(hd * LANES, (hd + 1) * LANES)
        dqt_ref[sl, :] = (_rope(zdq[:, sl], cd, sd, first_d, 112, 16) * scale_d).T.astype(BF16)
        dk_ref[:, sl] = _rope(zdk[:, sl], cd, sd, first_d, 112, 16).astype(BF16)


def _input_projection(s, mod, norm_g, w_in_b, wa, ba, q_norm_g, k_norm_g, tables):
    n, d = s.shape
    nt = n // ROW_TILE
    row = lambda w: pl.BlockSpec((ROW_TILE, w), lambda i: (i, 0))
    col = lambda w: pl.BlockSpec((w, ROW_TILE), lambda i: (0, i))
    tile_t = lambda w: pl.BlockSpec((None, w, ROW_TILE), lambda i: (i, 0, 0))
    f32o = lambda w: jax.ShapeDtypeStruct((n, w), F32)
    bfo = lambda w: jax.ShapeDtypeStruct((n, w), BF16)
    return pl.pallas_call(
        _inproj_kernel,
        grid=(nt,),
        in_specs=[row(d),
                  pl.BlockSpec((None, 8, d), lambda i: (_row_group(i), 0, 0)),
                  _resident((1, d)),
                  _resident((d, IN_COLS)),
                  _resident((LANES, 2 * GLA_QK)),
                  _resident((1, 2 * GLA_QK)),
                  _resident((1, HEAD_DIM)),
                  _resident((1, HEAD_DIM)),
                  row(LANES), row(LANES), row(LANES), row(LANES)],
        out_specs=[row(GLA_QK), row(GLA_QK), row(GLA_V), row(GLA_V),
                   pl.BlockSpec((2, ROW_TILE, GLA_QK), lambda i: (0, i, 0)),
                   col(GQA_Q), row(GQA_KV), tile_t(GQA_KV), col(DIFF_QK), row(DIFF_QK), tile_t(DIFF_V)],
        out_shape=[f32o(GLA_QK), f32o(GLA_QK), f32o(GLA_V), f32o(GLA_V),
                   jax.ShapeDtypeStruct((2, n, GLA_QK), F32),
                   jax.ShapeDtypeStruct((GQA_Q, n), BF16), bfo(GQA_KV),
                   jax.ShapeDtypeStruct((nt, GQA_KV, ROW_TILE), BF16),
                   jax.ShapeDtypeStruct((DIFF_QK, n), BF16), bfo(DIFF_QK),
                   jax.ShapeDtypeStruct((nt, DIFF_V, ROW_TILE), BF16)],
        compiler_params=_params("parallel"),
        name="input_projection",
    )(s, mod, norm_g.reshape(1, d), w_in_b, wa, ba, q_norm_g.reshape(1, -1), k_norm_g.reshape(1, -1),
      *tables)


def _gla_block(reverse, q_ref, k_ref, v_ref, la_ref, o_ref, st_ref):
    rows = q_ref.shape[0]
    n_chunks = rows // GLA_CHUNK
    ri = lax.broadcasted_iota(jnp.int32, (rows, rows), 0)
    ci = lax.broadcasted_iota(jnp.int32, (rows, rows), 1)
    same_chunk = (ri // GLA_CHUNK) == (ci // GLA_CHUNK)
    allowed = same_chunk & ((ci >= ri) if reverse else (ci <= ri))
    la = la_ref[...]
    cum = jnp.dot(jnp.where(allowed, 1.0, 0.0).astype(F32), la,
                  preferred_element_type=F32, precision=lax.Precision.HIGHEST)
    tot = jnp.dot(jnp.where(same_chunk, 1.0, 0.0).astype(F32), la,
                  preferred_element_type=F32, precision=lax.Precision.HIGHEST)
    k = k_ref[...]
    q_dec = q_ref[...] * jnp.exp(cum)
    k_inv = (k * jnp.exp(-cum)).astype(BF16)
    k_end = k * jnp.exp(tot - cum)
    v = v_ref[...]
    vb = v.astype(BF16)
    v_t = v.T.astype(BF16)
    lane_head = lax.broadcasted_iota(jnp.int32, (rows, GLA_QK), 1) // GLA_DK
    row_chunk = lax.broadcasted_iota(jnp.int32, (rows, GLA_QK), 0) // GLA_CHUNK
    nt_dims = (((1,), (1,)), ((), ()))

    for hd in range(GLA_HEADS):
        qh = jnp.where(lane_head == hd, q_dec, 0.0).astype(BF16)
        a = lax.dot_general(qh, k_inv, nt_dims, preferred_element_type=F32)
        a = jnp.where(allowed, a, 0.0).astype(BF16)
        o_ref[:, hd * GLA_DV:(hd + 1) * GLA_DV] = jnp.dot(
            a, vb[:, hd * GLA_DV:(hd + 1) * GLA_DV], preferred_element_type=F32)

    chunk_order = range(n_chunks - 1, -1, -1) if reverse else range(n_chunks)
    chunk_lane_head = lax.broadcasted_iota(jnp.int32, (GLA_CHUNK, GLA_QK), 1) // GLA_DK
    for c in chunk_order:
        sl = slice(c * GLA_CHUNK, (c + 1) * GLA_CHUNK)
        state = st_ref[...]
        q4 = jnp.concatenate(
            [jnp.where(chunk_lane_head == hd, q_dec[sl], 0.0) for hd in range(GLA_HEADS)],
            axis=0).astype(BF16)
        r = lax.dot_general(q4, state.astype(BF16), nt_dims, preferred_element_type=F32)
        for hd in range(GLA_HEADS):
            o_ref[sl, hd * GLA_DV:(hd + 1) * GLA_DV] += r[hd * GLA_CHUNK:(hd + 1) * GLA_CHUNK,
                                                          hd * GLA_DV:(hd + 1) * GLA_DV]
        k_end_c = jnp.where(row_chunk == c, k_end, 0.0).astype(BF16)
        u_t = jnp.dot(v_t, k_end_c, preferred_element_type=F32)
        st_ref[...] = state * jnp.exp(tot[c * GLA_CHUNK:c * GLA_CHUNK + 1]) + u_t


def _gla_kernel(q_ref, k_ref, v_ref, la_ref, o_ref, st_ref):
    @pl.when(pl.program_id(1) == 0)
    def _():
        st_ref[...] = jnp.zeros_like(st_ref)

    @pl.when(pl.program_id(0) == 0)
    def _():
        _gla_block(False, q_ref, k_ref, v_ref, la_ref, o_ref, st_ref)

    @pl.when(pl.program_id(0) == 1)
    def _():
        _gla_block(True, q_ref, k_ref, v_ref, la_ref, o_ref, st_ref)


def _gla(glaq, glak, glav, la):
    n = glaq.shape[0]
    nt = n // ROW_TILE

    def blk(dr, j):
        return jnp.where(dr == 0, j, jnp.where(j == 0, 0, nt - j))

    row = lambda w: pl.BlockSpec((ROW_TILE, w), lambda dr, j: (blk(dr, j), 0))
    return pl.pallas_call(
        _gla_kernel,
        grid=(2, nt),
        in_specs=[row(GLA_QK), row(GLA_QK), row(GLA_V),
                  pl.BlockSpec((None, ROW_TILE, GLA_QK), lambda dr, j: (dr, blk(dr, j), 0))],
        out_specs=pl.BlockSpec((None, ROW_TILE, GLA_V), lambda dr, j: (dr, blk(dr, j), 0)),
        out_shape=jax.ShapeDtypeStruct((2, n, GLA_V), F32),
        scratch_shapes=[pltpu.VMEM((GLA_V, GLA_QK), F32)],
        compiler_params=_params("arbitrary", "arbitrary"),
        name="gla_scan",
    )(glaq, glak, glav, la)


class _FlashMaps:
    def __init__(self, q_maps, k_of_map, load_k, load_vt):
        self.q_maps, self.k_of_map, self.load_k, self.load_vt = q_maps, k_of_map, load_k, load_vt


def _flash_steps(maps, first_tile, tiles_per_step, n_steps, s_ref, m_ref, l_ref, acc_ref):
    n_maps = len(maps.q_maps)
    rows = tiles_per_step * ROW_TILE
    for u in range(n_steps):
        first_row = pl.multiple_of((first_tile + u * tiles_per_step) * ROW_TILE, ROW_TILE)
        keys = {src: maps.load_k(src, first_row, rows) for src in sorted(set(maps.k_of_map))}
        for j in range(n_maps):
            s_ref[u * n_maps + j, :rows] = jnp.dot(keys[maps.k_of_map[j]], maps.q_maps[j],
                                                   preferred_element_type=F32)
    for u in range(n_steps):
        tile = first_tile + u * tiles_per_step
        for j in range(n_maps):
            s_t = s_ref[u * n_maps + j, :rows]
            m_prev = m_ref[j]
            m_new = jnp.maximum(m_prev, jnp.max(s_t, axis=0, keepdims=True))
            alpha = jnp.exp2(m_prev - m_new)
            p = jnp.exp2(s_t - m_new)
            l_ref[j] = alpha * l_ref[j] + jnp.sum(p, axis=0, keepdims=True)
            pb = p.astype(BF16)
            pv = jnp.dot(maps.load_vt(maps.k_of_map[j], tile), pb[:ROW_TILE], preferred_element_type=F32)
            for r in range(1, tiles_per_step):
                pv += jnp.dot(maps.load_vt(maps.k_of_map[j], tile + r), pb[r * ROW_TILE:(r + 1) * ROW_TILE],
                              preferred_element_type=F32)
            acc_ref[j] = alpha * acc_ref[j] + pv
            m_ref[j] = m_new


def _flash_attend(maps, first_q_tile, n_key_tiles, tiles_per_step, n_steps, s_ref, m_ref, l_ref, acc_ref):
    m_ref[...] = jnp.full(m_ref.shape, -jnp.inf, F32)
    l_ref[...] = jnp.zeros(l_ref.shape, F32)
    acc_ref[...] = jnp.zeros(acc_ref.shape, F32)
    _flash_steps(maps, 0, 1, 1, s_ref, m_ref, l_ref, acc_ref)
    per_iter = tiles_per_step * n_steps
    qi = pl.program_id(1) + first_q_tile
    n_iter = jnp.where(qi == 0, 0, (n_key_tiles - 1) // per_iter)

    def body(it, carry):
        _flash_steps(maps, 1 + it * per_iter, tiles_per_step, n_steps, s_ref, m_ref, l_ref, acc_ref)
        return carry

    lax.fori_loop(0, n_iter, body, 0)


def _flash_plan(n_key_tiles):
    latent = n_key_tiles - 1
    tiles_per_step = 2 if latent % 2 == 0 else 1
    n_steps = 2 if latent % (2 * tiles_per_step) == 0 else 1
    return tiles_per_step, n_steps


def _flash_scratch(n_maps, dv, plan):
    tiles_per_step, n_steps = plan
    return [pltpu.VMEM((n_steps * n_maps, tiles_per_step * ROW_TILE, ROW_TILE), F32),
            pltpu.VMEM((n_maps, 1, ROW_TILE), F32), pltpu.VMEM((n_maps, 1, ROW_TILE), F32),
            pltpu.VMEM((n_maps, dv, ROW_TILE), F32)]


GQA_GROUP = GQA_HEADS // GQA_KV_HEADS


def _gqa_kernel(first_q_tile, plan, qt_ref, k_ref, vt_ref, o_ref, s_ref, m_ref, l_ref, acc_ref):
    maps = _FlashMaps(
        q_maps=[qt_ref[hd * HEAD_DIM:(hd + 1) * HEAD_DIM, :] for hd in range(GQA_GROUP)],
        k_of_map=[0] * GQA_GROUP,
        load_k=lambda src, first_row, rows: k_ref[pl.ds(first_row, rows), :],
        load_vt=lambda src, tile: vt_ref[tile])
    _flash_attend(maps, first_q_tile, vt_ref.shape[0], *plan, s_ref, m_ref, l_ref, acc_ref)
    for hd in range(GQA_GROUP):
        o_t = acc_ref[hd] / l_ref[hd]
        o_ref[:, hd * HEAD_DIM:(hd + 1) * HEAD_DIM] = o_t.T.astype(o_ref.dtype)


def _gqa_attention(q_t, k, v_t, first_q_tile):
    n = k.shape[0]
    nt = n // ROW_TILE
    nq = nt - first_q_tile
    gw = GQA_GROUP * HEAD_DIM
    plan = _flash_plan(nt)
    return pl.pallas_call(
        functools.partial(_gqa_kernel, first_q_tile, plan),
        grid=(GQA_KV_HEADS, nq),
        in_specs=[pl.BlockSpec((gw, ROW_TILE), lambda g, i: (g, i + first_q_tile)),
                  pl.BlockSpec((n, HEAD_DIM), lambda g, i: (0, g)),
                  pl.BlockSpec((nt, HEAD_DIM, ROW_TILE), lambda g, i: (0, g, 0))],
        out_specs=pl.BlockSpec((ROW_TILE, gw), lambda g, i: (i + first_q_tile, g)),
        out_shape=jax.ShapeDtypeStruct((n, GQA_Q), BF16),
        scratch_shapes=_flash_scratch(GQA_GROUP, HEAD_DIM, plan),
        compiler_params=_params("parallel", "arbitrary"),
        name="gqa_attention",
    )(q_t, k, v_t)


DIFF_PAIR = 2


def _diff_kernel(first_q_tile, plan, lam_init, qt_ref, k_ref, vt_ref, lamv_ref, g_ref, o_ref,
                 s_ref, m_ref, l_ref, acc_ref):
    row = lax.broadcasted_iota(jnp.int32, (LANES, ROW_TILE), 0)
    q_maps = []
    for hd in range(DIFF_PAIR):
        q_t = qt_ref[hd * LANES:(hd + 1) * LANES, :]
        zero = jnp.zeros_like(q_t)
        q_maps += [jnp.where(row < DIFF_DQK, q_t, zero), jnp.where(row >= DIFF_DQK, q_t, zero)]
    maps = _FlashMaps(
        q_maps=q_maps,
        k_of_map=[hd for hd in range(DIFF_PAIR) for _ in range(2)],
        load_k=lambda src, first_row, rows: k_ref[pl.ds(first_row, rows), src * LANES:(src + 1) * LANES],
        load_vt=lambda src, tile: vt_ref[tile, src * DIFF_DV:(src + 1) * DIFF_DV, :])
    _flash_attend(maps, first_q_tile, vt_ref.shape[0], *plan, s_ref, m_ref, l_ref, acc_ref)
    lv = lamv_ref[...]
    lam = (jnp.exp(jnp.sum(lv[0:1] * lv[1:2], axis=-1, keepdims=True))
           - jnp.exp(jnp.sum(lv[2:3] * lv[3:4], axis=-1, keepdims=True)) + lam_init)
    for hd in range(DIFF_PAIR):
        o_t = acc_ref[2 * hd] / l_ref[2 * hd] - lam * (acc_ref[2 * hd + 1] / l_ref[2 * hd + 1])
        o = _rms(o_t.T, g_ref[...]) * (1.0 - lam_init)
        o_ref[:, hd * DIFF_DV:(hd + 1) * DIFF_DV] = o.astype(o_ref.dtype)


def _diff_attention(q_t, k, v_t, lamv, norm_g, lam_init, first_q_tile):
    n = k.shape[0]
    nt = n // ROW_TILE
    nq = nt - first_q_tile
    pw = DIFF_PAIR * LANES
    plan = _flash_plan(nt)
    return pl.pallas_call(
        functools.partial(_diff_kernel, first_q_tile, plan, lam_init),
        grid=(DIFF_HEADS // DIFF_PAIR, nq),
        in_specs=[pl.BlockSpec((pw, ROW_TILE), lambda h, i: (h, i + first_q_tile)),
                  pl.BlockSpec((n, pw), lambda h, i: (0, h)),
                  pl.BlockSpec((nt, DIFF_PAIR * DIFF_DV, ROW_TILE), lambda h, i: (0, h, 0)),
                  pl.BlockSpec((8, LANES), lambda h, i: (0, 0)),
                  pl.BlockSpec((1, DIFF_DV), lambda h, i: (0, 0))],
        out_specs=pl.BlockSpec((ROW_TILE, DIFF_PAIR * DIFF_DV), lambda h, i: (i + first_q_tile, h)),
        out_shape=jax.ShapeDtypeStruct((n, DIFF_V), BF16),
        scratch_shapes=_flash_scratch(2 * DIFF_PAIR, DIFF_DV, plan),
        compiler_params=_params("parallel", "arbitrary"),
        name="diff_attention",
    )(q_t, k, v_t, lamv, norm_g.reshape(1, -1))


def _outproj_kernel(s_ref, mod_ref, og_ref, gate_ref, oq_ref, od_ref, gg_ref, w_ref, o_ref):
    og = og_ref[0] + og_ref[1]
    gate = gate_ref[...]
    y = jnp.zeros(s_ref.shape, F32)
    for hd in range(GLA_HEADS):
        sl = slice(hd * GLA_DV, (hd + 1) * GLA_DV)
        oh = _rms(og[:, sl], gg_ref[...]) * _silu(gate[:, sl])
        y += jnp.dot(oh.astype(BF16), w_ref[hd * GLA_DV:(hd + 1) * GLA_DV, :], preferred_element_type=F32)
    y += jnp.dot(oq_ref[...], w_ref[GLA_V:GLA_V + GQA_Q, :], preferred_element_type=F32)
    y += jnp.dot(od_ref[...], w_ref[GLA_V + GQA_Q:, :], preferred_element_type=F32)
    o_ref[...] = s_ref[...] + mod_ref[2:3, :] * y


def _output_projection(s, mod, o_gla, gate, o_gqa, o_diff, gla_norm_g, w_out_b):
    n, d = s.shape
    nt = n // ROW_TILE
    row = lambda w: pl.BlockSpec((ROW_TILE, w), lambda i: (i, 0))
    return pl.pallas_call(
        _outproj_kernel,
        grid=(nt,),
        in_specs=[row(d),
                  pl.BlockSpec((None, 8, d), lambda i: (_row_group(i), 0, 0)),
                  pl.BlockSpec((2, ROW_TILE, GLA_V), lambda i: (0, i, 0)),
                  row(GLA_V), row(GQA_Q), row(DIFF_V),
                  _resident((1, GLA_DV)),
                  _resident((MIX_WIDTH, d))],
        out_specs=row(d),
        out_shape=jax.ShapeDtypeStruct((n, d), F32),
        compiler_params=_params("parallel"),
        name="output_projection",
    )(s, mod, o_gla, gate, o_gqa, o_diff, gla_norm_g.reshape(1, -1), w_out_b)


def _router_kernel(n_experts, s_ref, mod_ref, g_ref, rw_ref, rb_ref,
                   h_ref, eid_ref, rank_ref, w_ref, count_ref, carry_ref):
    @pl.when(pl.program_id(0) == 0)
    def _():
        carry_ref[...] = jnp.zeros_like(carry_ref)

    h = _rms(s_ref[...], g_ref[...]) * (1.0 + mod_ref[4:5, :]) + mod_ref[3:4, :]
    h_ref[...] = h
    logits = jnp.dot(h.astype(BF16), rw_ref[...], preferred_element_type=F32)
    scores = 1.0 / (1.0 + jnp.exp(-logits))
    rows = scores.shape[0]
    lane = lax.broadcasted_iota(jnp.int32, scores.shape, 1)
    lane_f = lane.astype(F32)
    cand = jnp.where(lane < n_experts, scores + rb_ref[...], -jnp.inf)
    hits = []
    for _ in range(TOP_K):
        best = jnp.max(cand, axis=-1, keepdims=True)
        first = jnp.min(jnp.where(cand == best, lane_f, float(LANES)), axis=-1, keepdims=True)
        hit = lane_f == first
        hits.append(hit)
        cand = jnp.where(hit, -jnp.inf, cand)
    chosen = functools.reduce(jnp.logical_or, hits)
    total = jnp.sum(jnp.where(chosen, scores, 0.0), axis=-1, keepdims=True)
    ri = lax.broadcasted_iota(jnp.int32, (rows, rows), 0)
    ci = lax.broadcasted_iota(jnp.int32, (rows, rows), 1)
    chosen_b = jnp.where(chosen, 1.0, 0.0).astype(BF16)
    before = jnp.dot(jnp.where(ci < ri, 1.0, 0.0).astype(BF16), chosen_b, preferred_element_type=F32)
    rank_all = before + carry_ref[...]
    for k, hit in enumerate(hits):
        pick = lambda a: jnp.sum(jnp.where(hit, a, 0.0), axis=-1, keepdims=True)
        eid_ref[:, k:k + 1] = pick(lane_f).astype(jnp.int32)
        rank_ref[:, k:k + 1] = pick(rank_all).astype(jnp.int32)
        w_ref[:, k:k + 1] = pick(scores) / total * ROUTE_SCALE
    carry_ref[...] += jnp.sum(chosen_b.astype(F32), axis=0, keepdims=True)
    count_ref[...] = carry_ref[...]


def _router(s, mod, norm_g, router_w_b, router_b, n_experts):
    n, d = s.shape
    nt = n // ROW_TILE
    row = lambda w: pl.BlockSpec((ROW_TILE, w), lambda i: (i, 0))
    return pl.pallas_call(
        functools.partial(_router_kernel, n_experts),
        grid=(nt,),
        in_specs=[row(d),
                  pl.BlockSpec((None, 8, d), lambda i: (_row_group(i), 0, 0)),
                  _resident((1, d)),
                  _resident((d, LANES)),
                  _resident((1, LANES))],
        out_specs=[row(d), row(TOP_K), row(TOP_K), row(TOP_K), pl.BlockSpec((1, LANES), lambda i: (0, 0))],
        out_shape=[jax.ShapeDtypeStruct((n, d), F32), jax.ShapeDtypeStruct((n, TOP_K), jnp.int32),
                   jax.ShapeDtypeStruct((n, TOP_K), jnp.int32), jax.ShapeDtypeStruct((n, TOP_K), F32),
                   jax.ShapeDtypeStruct((1, LANES), F32)],
        scratch_shapes=[pltpu.VMEM((1, LANES), F32)],
        compiler_params=_params("arbitrary"),
        name="router",
    )(s, mod, norm_g.reshape(1, d), router_w_b, router_b)


def _routing_plan(eid, rank, counts_f, n_experts, n_tiles):
    counts = counts_f[0, :n_experts].astype(jnp.int32)
    starts = jnp.concatenate([jnp.zeros((1,), jnp.int32), jnp.cumsum(counts)])
    dest = jnp.take(starts, eid) + rank
    first_tile = starts[:-1] // EXPERT_ROW_TILE
    last_tile = (starts[1:] - 1) // EXPERT_ROW_TILE
    visits_per_expert = jnp.where(counts > 0, last_tile - first_tile + 1, 0)
    visit_start = jnp.concatenate([jnp.zeros((1,), jnp.int32), jnp.cumsum(visits_per_expert)])
    n_visits = visit_start[-1]
    v = jnp.minimum(jnp.arange(n_tiles + n_experts - 1, dtype=jnp.int32), n_visits - 1)
    visit_expert = jnp.searchsorted(visit_start[1:], v, side='right').astype(jnp.int32)
    visit_tile = jnp.take(first_tile, visit_expert) + v - jnp.take(visit_start, visit_expert)
    return dest, starts, visit_tile.astype(jnp.int32), visit_expert, n_visits.reshape(1)


def _dispatch_kernel(dest_ref, h_hbm, xs_hbm, sem):
    i = pl.program_id(0)

    def row_copy(n, d):
        return pltpu.make_async_copy(h_hbm.at[pl.ds(n, 1)], xs_hbm.at[pl.ds(d, 1)], sem)

    def body(j, carry):
        for k in range(TOP_K):
            row_copy(i * ROW_TILE + j, dest_ref[k, j]).start()
        return carry

    lax.fori_loop(0, ROW_TILE, body, 0)
    all_rows = xs_hbm.at[pl.ds(0, ROW_TILE * TOP_K)]
    pltpu.make_async_copy(all_rows, all_rows, sem).wait()


def _dispatch(h, dest_tiles):
    n, d = h.shape
    nt = n // ROW_TILE
    return pl.pallas_call(
        _dispatch_kernel,
        grid=(nt,),
        in_specs=[pl.BlockSpec((None, TOP_K, ROW_TILE), lambda i: (i, 0, 0), memory_space=pltpu.SMEM),
                  pl.BlockSpec(memory_space=pl.ANY)],
        out_specs=pl.BlockSpec(memory_space=pl.ANY),
        out_shape=jax.ShapeDtypeStruct((n * TOP_K, d), F32),
        scratch_shapes=[pltpu.SemaphoreType.DMA(())],
        compiler_params=_params("arbitrary"),
        name="moe_dispatch",
    )(dest_tiles, h)


def _grouped_kernel(vt_ref, ve_ref, starts_ref, nv_ref, x_ref, wg_ref, wu_ref, wd_ref, y_ref,
                    wgb_ref, wub_ref, wdb_ref):
    v = pl.program_id(0)
    e = ve_ref[v]
    t = vt_ref[v]
    prev = jnp.maximum(v - 1, 0)

    @pl.when(jnp.logical_or(v == 0, ve_ref[prev] != e))
    def _():
        wgb_ref[...] = wg_ref[...].astype(BF16)
        wub_ref[...] = wu_ref[...].astype(BF16)
        wdb_ref[...] = wd_ref[...].astype(BF16)

    @pl.when(v < nv_ref[0])
    def _():
        x = x_ref[...].astype(BF16)
        a = jnp.dot(x, wgb_ref[...], preferred_element_type=F32)
        b = jnp.dot(x, wub_ref[...], preferred_element_type=F32)
        y = jnp.dot((_silu(a) * b).astype(BF16), wdb_ref[...], preferred_element_type=F32)
        rows = x.shape[0]
        r = t * rows + lax.broadcasted_iota(jnp.int32, (rows, 1), 0)
        y = jnp.where(jnp.logical_and(r >= starts_ref[e], r < starts_ref[e + 1]), y, 0.0)
        first_visit_of_tile = jnp.logical_or(v == 0, vt_ref[prev] != t)

        @pl.when(first_visit_of_tile)
        def _():
            y_ref[...] = y

        @pl.when(jnp.logical_not(first_visit_of_tile))
        def _():
            y_ref[...] += y


def _grouped_experts(xs, w_gate, w_up, w_down, visit_tile, visit_expert, starts, n_visits):
    p, d = xs.shape
    hidden = w_gate.shape[-1]
    tm = EXPERT_ROW_TILE
    grid_spec = pltpu.PrefetchScalarGridSpec(
        num_scalar_prefetch=4,
        grid=(visit_tile.shape[0],),
        in_specs=[pl.BlockSpec((tm, d), lambda v, vt, ve, st, nv: (vt[v], 0)),
                  pl.BlockSpec((None, d, hidden), lambda v, vt, ve, st, nv: (ve[v], 0, 0)),
                  pl.BlockSpec((None, d, hidden), lambda v, vt, ve, st, nv: (ve[v], 0, 0)),
                  pl.BlockSpec((None, hidden, d), lambda v, vt, ve, st, nv: (ve[v], 0, 0))],
        out_specs=pl.BlockSpec((tm, d), lambda v, vt, ve, st, nv: (vt[v], 0)),
        scratch_shapes=[pltpu.VMEM((d, hidden), BF16), pltpu.VMEM((d, hidden), BF16),
                        pltpu.VMEM((hidden, d), BF16)])
    return pl.pallas_call(
        _grouped_kernel,
        grid_spec=grid_spec,
        out_shape=jax.ShapeDtypeStruct((p, d), F32),
        compiler_params=_params("arbitrary"),
        name="moe_grouped_experts",
    )(visit_tile, visit_expert, starts, n_visits, xs, w_gate, w_up, w_down)


def _combine_kernel(n_ctx_rows, dest_ref, dest_next_ref, w_ref, h_ref, s_ref, mod_ref,
                    sg_ref, su_ref, sd_ref, ys_hbm, o_ref, ybuf_ref, sems):
    i = pl.program_id(0)
    n_steps = pl.num_programs(0)
    rows = h_ref.shape[0]
    slot = i % 2

    def gather(dref, to_slot):
        def body(j, carry):
            for k in range(TOP_K):
                pltpu.make_async_copy(ys_hbm.at[pl.ds(dref[k, j], 1)],
                                      ybuf_ref.at[to_slot, k, pl.ds(j, 1)], sems.at[to_slot]).start()
            return carry
        lax.fori_loop(0, rows, body, 0)

    @pl.when(i == 0)
    def _():
        gather(dest_ref, 0)

    @pl.when(i + 1 < n_steps)
    def _():
        gather(dest_next_ref, 1 - slot)

    hb = h_ref[...].astype(BF16)
    a = jnp.dot(hb, sg_ref[...], preferred_element_type=F32)
    b = jnp.dot(hb, su_ref[...], preferred_element_type=F32)
    y = jnp.dot((_silu(a) * b).astype(BF16), sd_ref[...], preferred_element_type=F32)

    pltpu.make_async_copy(ybuf_ref.at[slot], ybuf_ref.at[slot], sems.at[slot]).wait()
    w = w_ref[...]
    for k in range(TOP_K):
        y += w[:, k:k + 1] * ybuf_ref[slot, k]
    r = i * rows + lax.broadcasted_iota(jnp.int32, (rows, 1), 0)
    gate2 = jnp.where(r < n_ctx_rows, mod_ref[1, 5:6, :], mod_ref[0, 5:6, :])
    o_ref[...] = s_ref[...] + gate2 * y


def _combine(ys, dest_tiles, w, h, s, mod, sh_gate_b, sh_up_b, sh_down_b, n_ctx_rows):
    n, d = s.shape
    tc = COMBINE_ROW_TILE
    steps = n // tc
    hidden = sh_gate_b.shape[-1]
    row = lambda width: pl.BlockSpec((tc, width), lambda i: (i, 0))
    idx = lambda shift: pl.BlockSpec((None, TOP_K, tc), lambda i: (jnp.minimum(i + shift, steps - 1), 0, 0),
                                     memory_space=pltpu.SMEM)
    return pl.pallas_call(
        functools.partial(_combine_kernel, n_ctx_rows),
        grid=(steps,),
        in_specs=[idx(0), idx(1), row(TOP_K), row(d), row(d),
                  pl.BlockSpec((2, 8, d), lambda i: (0, 0, 0)),
                  _resident((d, hidden)), _resident((d, hidden)), _resident((hidden, d)),
                  pl.BlockSpec(memory_space=pl.ANY)],
        out_specs=row(d),
        out_shape=jax.ShapeDtypeStruct((n, d), F32),
        scratch_shapes=[pltpu.VMEM((2, TOP_K, tc, d), F32), pltpu.SemaphoreType.DMA((2,))],
        compiler_params=_params("arbitrary"),
        name="moe_combine",
    )(dest_tiles, dest_tiles, w, h, s, mod, sh_gate_b, sh_up_b, sh_down_b, ys)


def _tile_major(dest, tile):
    n, k = dest.shape
    return dest.reshape(n // tile, tile, k).transpose(0, 2, 1)


def _final_norm_kernel(s_ref, g_ref, o_ref):
    o_ref[...] = _rms(s_ref[...], g_ref[...])


def _final_norm(s, g, first_tile):
    n, d = s.shape
    nt = n // ROW_TILE - first_tile
    return pl.pallas_call(
        _final_norm_kernel,
        grid=(nt,),
        in_specs=[pl.BlockSpec((ROW_TILE, d), lambda i: (i + first_tile, 0)), _resident((1, d))],
        out_specs=pl.BlockSpec((ROW_TILE, d), lambda i: (i, 0)),
        out_shape=jax.ShapeDtypeStruct((nt * ROW_TILE, d), F32),
        compiler_params=_params("parallel"),
        name="final_norm",
    )(s, g.reshape(1, d))


def _rope_tables(n_ctx, n_tokens, dim):
    half = dim // 2
    inv_freq = ROPE_THETA ** (-jnp.arange(0, half, 2, dtype=F32) / half)
    t = jnp.arange(n_tokens, dtype=jnp.int32)
    ang_r = (t // GRID_W).astype(F32)[:, None] * inv_freq
    ang_c = (t % GRID_W).astype(F32)[:, None] * inv_freq
    cos = jnp.concatenate([jnp.cos(ang_r)] * 2 + [jnp.cos(ang_c)] * 2, axis=-1)
    sin = jnp.concatenate([-jnp.sin(ang_r), jnp.sin(ang_r), -jnp.sin(ang_c), jnp.sin(ang_c)], axis=-1)
    reps = LANES // dim
    cos = jnp.tile(cos, (1, reps))
    sin = jnp.tile(sin, (1, reps))
    cos = jnp.concatenate([jnp.ones((n_ctx, LANES), F32), cos], axis=0)
    sin = jnp.concatenate([jnp.zeros((n_ctx, LANES), F32), sin], axis=0)
    return cos, sin


def kernel(x, c, ctx, c_ctx, norm1_g, norm2_g, w_mod, b_mod, w_in, gla_wa_f, gla_ba_f, gla_wa_b,
           gla_ba_b, gla_norm_g, q_norm_g, k_norm_g, diff_lq1, diff_lk1, diff_lq2, diff_lk2,
           diff_norm_g, w_out, router_w, router_b, exp_w_gate, exp_w_up, exp_w_down,
           sh_w_gate, sh_w_up, sh_w_down, final_g):
    batch, n_tokens, d = x.shape
    n_ctx = ctx.shape[1]
    depth = w_mod.shape[0]
    n_experts = router_w.shape[-1]
    n_rows = n_ctx + n_tokens
    assert batch == 1 and n_ctx == ROW_TILE and n_tokens % ROW_TILE == 0
    assert TOP_K <= n_experts <= LANES

    s = jnp.concatenate([ctx[0], x[0]], axis=0)
    mod_all = _modulation(c, c_ctx, w_mod, b_mod)
    tables = _rope_tables(n_ctx, n_tokens, HEAD_DIM) + _rope_tables(n_ctx, n_tokens, DIFF_DQK)

    for l in range(depth):
        first_q_tile = 0
        lam_init = 0.8 - 0.6 * math.exp(-0.3 * l)
        mod = mod_all[l]

        w_in_b = jnp.concatenate(
            [w_in[l][:, :ORIG_LR], w_in[l][:, ORIG_LR + 2 * GLA_GATE_RANK:],
             w_in[l][:, ORIG_LR:ORIG_LR + 2 * GLA_GATE_RANK],
             jnp.zeros((d, LANES - 2 * GLA_GATE_RANK), F32)], axis=1).astype(BF16)
        wa = jnp.zeros((LANES, 2 * GLA_QK), F32)
        wa = wa.at[:GLA_GATE_RANK, :GLA_QK].set(gla_wa_f[l])
        wa = wa.at[GLA_GATE_RANK:2 * GLA_GATE_RANK, GLA_QK:].set(gla_wa_b[l]).astype(BF16)
        ba = jnp.concatenate([gla_ba_f[l], gla_ba_b[l]]).reshape(1, -1)
        lamv = jnp.zeros((8, LANES), F32)
        for r, vec in enumerate((diff_lq1[l], diff_lk1[l], diff_lq2[l], diff_lk2[l])):
            lamv = lamv.at[r, :DIFF_DQK].set(vec)
        rw = jnp.pad(router_w[l], ((0, 0), (0, LANES - n_experts))).astype(BF16)
        rb = jnp.pad(router_b[l], (0, LANES - n_experts)).reshape(1, LANES)

        (glaq, glak, glav, gate, la, q_t, k, v_t, dq_t, dk, dv_t) = _input_projection(
            s, mod, norm1_g[l], w_in_b, wa, ba, q_norm_g[l], k_norm_g[l], tables)
        o_gla = _gla(glaq, glak, glav, la)
        o_gqa = _gqa_attention(q_t, k, v_t, first_q_tile)
        o_diff = _diff_attention(dq_t, dk, dv_t, lamv, diff_norm_g[l], lam_init, first_q_tile)
        s = _output_projection(s, mod, o_gla, gate, o_gqa, o_diff, gla_norm_g[l], w_out[l].astype(BF16))
        h2, eid, rank, w_route, counts = _router(s, mod, norm2_g[l], rw, rb, n_experts)
        dest, starts, visit_tile, visit_expert, n_visits = _routing_plan(
            eid, rank, counts, n_experts, n_rows * TOP_K // EXPERT_ROW_TILE)
        xs = _dispatch(h2, _tile_major(dest, ROW_TILE))
        ys = _grouped_experts(xs, exp_w_gate[l], exp_w_up[l], exp_w_down[l],
                              visit_tile, visit_expert, starts, n_visits)
        s = _combine(ys, _tile_major(dest, COMBINE_ROW_TILE), w_route, h2, s, mod,
                     sh_w_gate[l].astype(BF16), sh_w_up[l].astype(BF16), sh_w_down[l].astype(BF16), n_ctx)

    out = _final_norm(s, final_g, n_ctx // ROW_TILE)
    return out.reshape(batch, n_tokens, d)
```

```python
import functools
import math

import jax
import jax.numpy as jnp
from jax import lax
from jax.experimental import pallas as pl
from jax.experimental.pallas import tpu as pltpu

F32 = jnp.float32
BF16 = jnp.bfloat16

GRID_W = 64
HEAD_DIM = 128
GLA_HEADS = 4
GLA_DK = 64
GLA_DV = 128
GLA_GATE_RANK = 16
GLA_TAU = 16.0
GLA_CHUNK = 64
GQA_HEADS = 8
GQA_KV_HEADS = 2
DIFF_HEADS = 4
DIFF_DQK = 64
DIFF_DV = 128
TOP_K = 8
ROUTE_SCALE = 2.5
ROPE_THETA = 10000.0
NORM_EPS = 1e-6
LOG2_E = math.log2(math.e)

LANES = 128
ROW_TILE = 256
EXPERT_ROW_TILE = 256
COMBINE_ROW_TILE = 128
VMEM_LIMIT = 56 * 1024 * 1024

GLA_QK = GLA_HEADS * GLA_DK
GLA_V = GLA_HEADS * GLA_DV
GQA_Q = GQA_HEADS * HEAD_DIM
GQA_KV = GQA_KV_HEADS * HEAD_DIM
DIFF_QK = DIFF_HEADS * 2 * DIFF_DQK
DIFF_V = DIFF_HEADS * DIFF_DV
MIX_WIDTH = GLA_V + GQA_Q + DIFF_V

C_GLAQ = 0
C_GLAK = C_GLAQ + GLA_QK
C_GLAV = C_GLAK + GLA_QK
C_GATE = C_GLAV + GLA_V
C_GQAQ = C_GATE + GLA_V
C_GQAK = C_GQAQ + GQA_Q
C_GQAV = C_GQAK + GQA_KV
C_DQ = C_GQAV + GQA_KV
C_DK = C_DQ + DIFF_QK
C_DV = C_DK + DIFF_QK
C_LR = C_DV + DIFF_V
IN_COLS = C_LR + LANES
ORIG_LR = 2 * GLA_QK + 2 * GLA_V


def _params(*sem):
    return pltpu.CompilerParams(dimension_semantics=sem, vmem_limit_bytes=VMEM_LIMIT)


def _resident(shape):
    nd = len(shape)
    return pl.BlockSpec(shape, lambda *_: (0,) * nd, pipeline_mode=pl.Buffered(1))


def _silu(a):
    return a / (1.0 + jnp.exp(-a))


def _rms(x, g):
    return x * lax.rsqrt(jnp.mean(x * x, axis=-1, keepdims=True) + NORM_EPS) * g


def _row_group(i):
    return jnp.where(i == 0, 1, 0)


def _mod_kernel(a_ref, w_ref, b_ref, o_ref):
    a = _silu(a_ref[...])
    o_ref[...] = jnp.dot(a.astype(BF16), w_ref[...].astype(BF16),
                         preferred_element_type=F32) + b_ref[...]


def _modulation(c, c_ctx, w_mod, b_mod):
    depth, d, six_d = w_mod.shape
    a = jnp.zeros((8, d), F32).at[0].set(c[0]).at[1].set(c_ctx)
    tn = d // 2
    out = pl.pallas_call(
        _mod_kernel,
        grid=(depth, six_d // tn),
        in_specs=[pl.BlockSpec((8, d), lambda l, j: (0, 0)),
                  pl.BlockSpec((None, d, tn), lambda l, j: (l, 0, j)),
                  pl.BlockSpec((None, 1, tn), lambda l, j: (l, 0, j))],
        out_specs=pl.BlockSpec((None, 8, tn), lambda l, j: (l, 0, j)),
        out_shape=jax.ShapeDtypeStruct((depth, 8, six_d), F32),
        compiler_params=_params("parallel", "parallel"),
        name="modulation",
    )(a, w_mod, b_mod.reshape(depth, 1, six_d))
    m = out[:, :2].reshape(depth, 2, 6, d)
    return jnp.pad(m, ((0, 0), (0, 0), (0, 2), (0, 0)))


def _rope(xh, cos, sin, first, shift_first, shift_second):
    partner = jnp.where(first, pltpu.roll(xh, shift_first, 1), pltpu.roll(xh, shift_second, 1))
    return xh * cos + partner * sin


def _inproj_kernel(x_ref, mod_ref, g_ref, w_ref, wa_ref, ba_ref, qg_ref, kg_ref,
                   cg_ref, sg_ref, cd_ref, sd_ref,
                   glaq_ref, glak_ref, glav_ref, gate_ref, la_ref,
                   qt_ref, k_ref, vt_ref, dqt_ref, dk_ref, dvt_ref):
    x = x_ref[...]
    h = _rms(x, g_ref[...]) * (1.0 + mod_ref[1:2, :]) + mod_ref[0:1, :]
    hb = h.astype(BF16)

    def proj(start, width):
        return jnp.dot(hb, w_ref[:, start:start + width], preferred_element_type=F32)

    glaq_ref[...] = proj(C_GLAQ, GLA_QK) * (GLA_DK ** -0.5)
    glak_ref[...] = proj(C_GLAK, GLA_QK)
    glav_ref[...] = proj(C_GLAV, GLA_V)
    gate_ref[...] = proj(C_GATE, GLA_V)
    zv = proj(C_GQAV, GQA_KV)
    for hd in range(GQA_KV_HEADS):
        sl = slice(hd * HEAD_DIM, (hd + 1) * HEAD_DIM)
        vt_ref[sl, :] = zv[:, sl].T.astype(BF16)
    zdv = proj(C_DV, DIFF_V)
    for hd in range(DIFF_HEADS):
        sl = slice(hd * DIFF_DV, (hd + 1) * DIFF_DV)
        dvt_ref[sl, :] = zdv[:, sl].T.astype(BF16)

    z_lr = proj(C_LR, LANES).astype(BF16)
    pre = jnp.dot(z_lr, wa_ref[...], preferred_element_type=F32) + ba_ref[...]
    log_sig = -(jnp.maximum(-pre, 0.0) + jnp.log1p(jnp.exp(-jnp.abs(pre))))
    la = log_sig * (1.0 / GLA_TAU)
    la_ref[0] = la[:, :GLA_QK]
    la_ref[1] = la[:, GLA_QK:]

    rows = x.shape[0]
    lane = lax.broadcasted_iota(jnp.int32, (rows, LANES), 1)
    first_g = (lane % 64) < 32
    first_d = (lane % 32) < 16
    cg, sg, cd, sd = cg_ref[...], sg_ref[...], cd_ref[...], sd_ref[...]
    scale_g = HEAD_DIM ** -0.5 * LOG2_E
    scale_d = DIFF_DQK ** -0.5 * LOG2_E

    zq = proj(C_GQAQ, GQA_Q)
    for hd in range(GQA_HEADS):
        sl = slice(hd * HEAD_DIM, (hd + 1) * HEAD_DIM)
        qh = _rope(_rms(zq[:, sl], qg_ref[...]), cg, sg, first_g, 96, 32) * scale_g
        qt_ref[sl, :] = qh.T.astype(BF16)
    zk = proj(C_GQAK, GQA_KV)
    for hd in range(GQA_KV_HEADS):
        kh = _rms(zk[:, hd * HEAD_DIM:(hd + 1) * HEAD_DIM], kg_ref[...])
        k_ref[:, hd * HEAD_DIM:(hd + 1) * HEAD_DIM] = _rope(kh, cg, sg, first_g, 96, 32).astype(BF16)
    zdq = proj(C_DQ, DIFF_QK)
    zdk = proj(C_DK, DIFF_QK)
    for hd in range(DIFF_HEADS):
        sl = slice(hd * LANES, (hd + 1) * LANES)
        dqt_ref[sl, :] = (_rope(zdq[:, sl], cd, sd, first_d, 112, 16) * scale_d).T.astype(BF16)
        dk_ref[:, sl] = _rope(zdk[:, sl], cd, sd, first_d, 112, 16).astype(BF16)


def _input_projection(s, mod, norm_g, w_in_b, wa, ba, q_norm_g, k_norm_g, tables):
    n, d = s.shape
    nt = n // ROW_TILE
    row = lambda w: pl.BlockSpec((ROW_TILE, w), lambda i: (i, 0))
    col = lambda w: pl.BlockSpec((w, ROW_TILE), lambda i: (0, i))
    tile_t = lambda w: pl.BlockSpec((None, w, ROW_TILE), lambda i: (i, 0, 0))
    f32o = lambda w: jax.ShapeDtypeStruct((n, w), F32)
    bfo = lambda w: jax.ShapeDtypeStruct((n, w), BF16)
    return pl.pallas_call(
        _inproj_kernel,
        grid=(nt,),
        in_specs=[row(d),
                  pl.BlockSpec((None, 8, d), lambda i: (_row_group(i), 0, 0)),
                  _resident((1, d)),
                  _resident((d, IN_COLS)),
                  _resident((LANES, 2 * GLA_QK)),
                  _resident((1, 2 * GLA_QK)),
                  _resident((1, HEAD_DIM)),
                  _resident((1, HEAD_DIM)),
                  row(LANES), row(LANES), row(LANES), row(LANES)],
        out_specs=[row(GLA_QK), row(GLA_QK), row(GLA_V), row(GLA_V),
                   pl.BlockSpec((2, ROW_TILE, GLA_QK), lambda i: (0, i, 0)),
                   col(GQA_Q), row(GQA_KV), tile_t(GQA_KV), col(DIFF_QK), row(DIFF_QK), tile_t(DIFF_V)],
        out_shape=[f32o(GLA_QK), f32o(GLA_QK), f32o(GLA_V), f32o(GLA_V),
                   jax.ShapeDtypeStruct((2, n, GLA_QK), F32),
                   jax.ShapeDtypeStruct((GQA_Q, n), BF16), bfo(GQA_KV),
                   jax.ShapeDtypeStruct((nt, GQA_KV, ROW_TILE), BF16),
                   jax.ShapeDtypeStruct((DIFF_QK, n), BF16), bfo(DIFF_QK),
                   jax.ShapeDtypeStruct((nt, DIFF_V, ROW_TILE), BF16)],
        compiler_params=_params("parallel"),
        name="input_projection",
    )(s, mod, norm_g.reshape(1, d), w_in_b, wa, ba, q_norm_g.reshape(1, -1), k_norm_g.reshape(1, -1),
      *tables)


def _gla_block(reverse, q_ref, k_ref, v_ref, la_ref, o_ref, st_ref):
    rows = q_ref.shape[0]
    n_chunks = rows // GLA_CHUNK
    ri = lax.broadcasted_iota(jnp.int32, (rows, rows), 0)
    ci = lax.broadcasted_iota(jnp.int32, (rows, rows), 1)
    same_chunk = (ri // GLA_CHUNK) == (ci // GLA_CHUNK)
    allowed = same_chunk & ((ci >= ri) if reverse else (ci <= ri))
    la = la_ref[...]
    cum = jnp.dot(jnp.where(allowed, 1.0, 0.0).astype(F32), la,
                  preferred_element_type=F32, precision=lax.Precision.HIGHEST)
    tot = jnp.dot(jnp.where(same_chunk, 1.0, 0.0).astype(F32), la,
                  preferred_element_type=F32, precision=lax.Precision.HIGHEST)
    k = k_ref[...]
    q_dec = q_ref[...] * jnp.exp(cum)
    k_inv = (k * jnp.exp(-cum)).astype(BF16)
    k_end = k * jnp.exp(tot - cum)
    v = v_ref[...]
    vb = v.astype(BF16)
    v_t = v.T.astype(BF16)
    lane_head = lax.broadcasted_iota(jnp.int32, (rows, GLA_QK), 1) // GLA_DK
    row_chunk = lax.broadcasted_iota(jnp.int32, (rows, GLA_QK), 0) // GLA_CHUNK
    nt_dims = (((1,), (1,)), ((), ()))

    for hd in range(GLA_HEADS):
        qh = jnp.where(lane_head == hd, q_dec, 0.0).astype(BF16)
        a = lax.dot_general(qh, k_inv, nt_dims, preferred_element_type=F32)
        a = jnp.where(allowed, a, 0.0).astype(BF16)
        o_ref[:, hd * GLA_DV:(hd + 1) * GLA_DV] = jnp.dot(
            a, vb[:, hd * GLA_DV:(hd + 1) * GLA_DV], preferred_element_type=F32)

    chunk_order = range(n_chunks - 1, -1, -1) if reverse else range(n_chunks)
    chunk_lane_head = lax.broadcasted_iota(jnp.int32, (GLA_CHUNK, GLA_QK), 1) // GLA_DK
    for c in chunk_order:
        sl = slice(c * GLA_CHUNK, (c + 1) * GLA_CHUNK)
        state = st_ref[...]
        q4 = jnp.concatenate(
            [jnp.where(chunk_lane_head == hd, q_dec[sl], 0.0) for hd in range(GLA_HEADS)],
            axis=0).astype(BF16)
        r = lax.dot_general(q4, state.astype(BF16), nt_dims, preferred_element_type=F32)
        for hd in range(GLA_HEADS):
            o_ref[sl, hd * GLA_DV:(hd + 1) * GLA_DV] += r[hd * GLA_CHUNK:(hd + 1) * GLA_CHUNK,
                                                          hd * GLA_DV:(hd + 1) * GLA_DV]
        k_end_c = jnp.where(row_chunk == c, k_end, 0.0).astype(BF16)
        u_t = jnp.dot(v_t, k_end_c, preferred_element_type=F32)
        st_ref[...] = state * jnp.exp(tot[c * GLA_CHUNK:c * GLA_CHUNK + 1]) + u_t


def _gla_kernel(q_ref, k_ref, v_ref, la_ref, o_ref, st_ref):
    @pl.when(pl.program_id(1) == 0)
    def _():
        st_ref[...] = jnp.zeros_like(st_ref)

    @pl.when(pl.program_id(0) == 0)
    def _():
        _gla_block(False, q_ref, k_ref, v_ref, la_ref, o_ref, st_ref)

    @pl.when(pl.program_id(0) == 1)
    def _():
        _gla_block(True, q_ref, k_ref, v_ref, la_ref, o_ref, st_ref)


def _gla(glaq, glak, glav, la):
    n = glaq.shape[0]
    nt = n // ROW_TILE

    def blk(dr, j):
        return jnp.where(dr == 0, j, jnp.where(j == 0, 0, nt - j))

    row = lambda w: pl.BlockSpec((ROW_TILE, w), lambda dr, j: (blk(dr, j), 0))
    return pl.pallas_call(
        _gla_kernel,
        grid=(2, nt),
        in_specs=[row(GLA_QK), row(GLA_QK), row(GLA_V),
                  pl.BlockSpec((None, ROW_TILE, GLA_QK), lambda dr, j: (dr, blk(dr, j), 0))],
        out_specs=pl.BlockSpec((None, ROW_TILE, GLA_V), lambda dr, j: (dr, blk(dr, j), 0)),
        out_shape=jax.ShapeDtypeStruct((2, n, GLA_V), F32),
        scratch_shapes=[pltpu.VMEM((GLA_V, GLA_QK), F32)],
        compiler_params=_params("arbitrary", "arbitrary"),
        name="gla_scan",
    )(glaq, glak, glav, la)


class _FlashMaps:
    def __init__(self, q_maps, k_of_map, load_k, load_vt):
        self.q_maps, self.k_of_map, self.load_k, self.load_vt = q_maps, k_of_map, load_k, load_vt


def _flash_steps(maps, first_tile, tiles_per_step, n_steps, s_ref, m_ref, l_ref, acc_ref):
    n_maps = len(maps.q_maps)
    rows = tiles_per_step * ROW_TILE
    for u in range(n_steps):
        first_row = pl.multiple_of((first_tile + u * tiles_per_step) * ROW_TILE, ROW_TILE)
        keys = {src: maps.load_k(src, first_row, rows) for src in sorted(set(maps.k_of_map))}
        for j in range(n_maps):
            s_ref[u * n_maps + j, :rows] = jnp.dot(keys[maps.k_of_map[j]], maps.q_maps[j],
                                                   preferred_element_type=F32)
    for u in range(n_steps):
        tile = first_tile + u * tiles_per_step
        for j in range(n_maps):
            s_t = s_ref[u * n_maps + j, :rows]
            m_prev = m_ref[j]
            m_new = jnp.maximum(m_prev, jnp.max(s_t, axis=0, keepdims=True))
            alpha = jnp.exp2(m_prev - m_new)
            p = jnp.exp2(s_t - m_new)
            l_ref[j] = alpha * l_ref[j] + jnp.sum(p, axis=0, keepdims=True)
            pb = p.astype(BF16)
            pv = jnp.dot(maps.load_vt(maps.k_of_map[j], tile), pb[:ROW_TILE], preferred_element_type=F32)
            for r in range(1, tiles_per_step):
                pv += jnp.dot(maps.load_vt(maps.k_of_map[j], tile + r), pb[r * ROW_TILE:(r + 1) * ROW_TILE],
                              preferred_element_type=F32)
            acc_ref[j] = alpha * acc_ref[j] + pv
            m_ref[j] = m_new


def _flash_attend(maps, first_q_tile, n_key_tiles, tiles_per_step, n_steps, s_ref, m_ref, l_ref, acc_ref):
    m_ref[...] = jnp.full(m_ref.shape, -jnp.inf, F32)
    l_ref[...] = jnp.zeros(l_ref.shape, F32)
    acc_ref[...] = jnp.zeros(acc_ref.shape, F32)
    _flash_steps(maps, 0, 1, 1, s_ref, m_ref, l_ref, acc_ref)
    per_iter = tiles_per_step * n_steps
    qi = pl.program_id(1) + first_q_tile
    n_iter = jnp.where(qi == 0, 0, (n_key_tiles - 1) // per_iter)

    def body(it, carry):
        _flash_steps(maps, 1 + it * per_iter, tiles_per_step, n_steps, s_ref, m_ref, l_ref, acc_ref)
        return carry

    lax.fori_loop(0, n_iter, body, 0)


def _flash_plan(n_key_tiles):
    latent = n_key_tiles - 1
    tiles_per_step = 2 if latent % 2 == 0 else 1
    n_steps = 2 if latent % (2 * tiles_per_step) == 0 else 1
    return tiles_per_step, n_steps


def _flash_scratch(n_maps, dv, plan):
    tiles_per_step, n_steps = plan
    return [pltpu.VMEM((n_steps * n_maps, tiles_per_step * ROW_TILE, ROW_TILE), F32),
            pltpu.VMEM((n_maps, 1, ROW_TILE), F32), pltpu.VMEM((n_maps, 1, ROW_TILE), F32),
            pltpu.VMEM((n_maps, dv, ROW_TILE), F32)]


GQA_GROUP = GQA_HEADS // GQA_KV_HEADS


def _gqa_kernel(first_q_tile, plan, qt_ref, k_ref, vt_ref, o_ref, s_ref, m_ref, l_ref, acc_ref):
    maps = _FlashMaps(
        q_maps=[qt_ref[hd * HEAD_DIM:(hd + 1) * HEAD_DIM, :] for hd in range(GQA_GROUP)],
        k_of_map=[0] * GQA_GROUP,
        load_k=lambda src, first_row, rows: k_ref[pl.ds(first_row, rows), :],
        load_vt=lambda src, tile: vt_ref[tile])
    _flash_attend(maps, first_q_tile, vt_ref.shape[0], *plan, s_ref, m_ref, l_ref, acc_ref)
    for hd in range(GQA_GROUP):
        o_t = acc_ref[hd] / l_ref[hd]
        o_ref[:, hd * HEAD_DIM:(hd + 1) * HEAD_DIM] = o_t.T.astype(o_ref.dtype)


def _gqa_attention(q_t, k, v_t, first_q_tile):
    n = k.shape[0]
    nt = n // ROW_TILE
    nq = nt - first_q_tile
    gw = GQA_GROUP * HEAD_DIM
    plan = _flash_plan(nt)
    return pl.pallas_call(
        functools.partial(_gqa_kernel, first_q_tile, plan),
        grid=(GQA_KV_HEADS, nq),
        in_specs=[pl.BlockSpec((gw, ROW_TILE), lambda g, i: (g, i + first_q_tile)),
                  pl.BlockSpec((n, HEAD_DIM), lambda g, i: (0, g)),
                  pl.BlockSpec((nt, HEAD_DIM, ROW_TILE), lambda g, i: (0, g, 0))],
        out_specs=pl.BlockSpec((ROW_TILE, gw), lambda g, i: (i + first_q_tile, g)),
        out_shape=jax.ShapeDtypeStruct((n, GQA_Q), BF16),
        scratch_shapes=_flash_scratch(GQA_GROUP, HEAD_DIM, plan),
        compiler_params=_params("parallel", "arbitrary"),
        name="gqa_attention",
    )(q_t, k, v_t)


DIFF_PAIR = 2


def _diff_kernel(first_q_tile, plan, lam_init, qt_ref, k_ref, vt_ref, lamv_ref, g_ref, o_ref,
                 s_ref, m_ref, l_ref, acc_ref):
    row = lax.broadcasted_iota(jnp.int32, (LANES, ROW_TILE), 0)
    q_maps = []
    for hd in range(DIFF_PAIR):
        q_t = qt_ref[hd * LANES:(hd + 1) * LANES, :]
        zero = jnp.zeros_like(q_t)
        q_maps += [jnp.where(row < DIFF_DQK, q_t, zero), jnp.where(row >= DIFF_DQK, q_t, zero)]
    maps = _FlashMaps(
        q_maps=q_maps,
        k_of_map=[hd for hd in range(DIFF_PAIR) for _ in range(2)],
        load_k=lambda src, first_row, rows: k_ref[pl.ds(first_row, rows), src * LANES:(src + 1) * LANES],
        load_vt=lambda src, tile: vt_ref[tile, src * DIFF_DV:(src + 1) * DIFF_DV, :])
    _flash_attend(maps, first_q_tile, vt_ref.shape[0], *plan, s_ref, m_ref, l_ref, acc_ref)
    lv = lamv_ref[...]
    lam = (jnp.exp(jnp.sum(lv[0:1] * lv[1:2], axis=-1, keepdims=True))
           - jnp.exp(jnp.sum(lv[2:3] * lv[3:4], axis=-1, keepdims=True)) + lam_init)
    for hd in range(DIFF_PAIR):
        o_t = acc_ref[2 * hd] / l_ref[2 * hd] - lam * (acc_ref[2 * hd + 1] / l_ref[2 * hd + 1])
        o = _rms(o_t.T, g_ref[...]) * (1.0 - lam_init)
        o_ref[:, hd * DIFF_DV:(hd + 1) * DIFF_DV] = o.astype(o_ref.dtype)


def _diff_attention(q_t, k, v_t, lamv, norm_g, lam_init, first_q_tile):
    n = k.shape[0]
    nt = n // ROW_TILE
    nq = nt - first_q_tile
    pw = DIFF_PAIR * LANES
    plan = _flash_plan(nt)
    return pl.pallas_call(
        functools.partial(_diff_kernel, first_q_tile, plan, lam_init),
        grid=(DIFF_HEADS // DIFF_PAIR, nq),
        in_specs=[pl.BlockSpec((pw, ROW_TILE), lambda h, i: (h, i + first_q_tile)),
                  pl.BlockSpec((n, pw), lambda h, i: (0, h)),
                  pl.BlockSpec((nt, DIFF_PAIR * DIFF_DV, ROW_TILE), lambda h, i: (0, h, 0)),
                  pl.BlockSpec((8, LANES), lambda h, i: (0, 0)),
                  pl.BlockSpec((1, DIFF_DV), lambda h, i: (0, 0))],
        out_specs=pl.BlockSpec((ROW_TILE, DIFF_PAIR * DIFF_DV), lambda h, i: (i + first_q_tile, h)),
        out_shape=jax.ShapeDtypeStruct((n, DIFF_V), BF16),
        scratch_shapes=_flash_scratch(2 * DIFF_PAIR, DIFF_DV, plan),
        compiler_params=_params("parallel", "arbitrary"),
        name="diff_attention",
    )(q_t, k, v_t, lamv, norm_g.reshape(1, -1))


def _outproj_kernel(s_ref, mod_ref, og_ref, gate_ref, oq_ref, od_ref, gg_ref, w_ref, o_ref):
    og = og_ref[0] + og_ref[1]
    gate = gate_ref[...]
    y = jnp.zeros(s_ref.shape, F32)
    for hd in range(GLA_HEADS):
        sl = slice(hd * GLA_DV, (hd + 1) * GLA_DV)
        oh = _rms(og[:, sl], gg_ref[...]) * _silu(gate[:, sl])
        y += jnp.dot(oh.astype(BF16), w_ref[hd * GLA_DV:(hd + 1) * GLA_DV, :], preferred_element_type=F32)
    y += jnp.dot(oq_ref[...], w_ref[GLA_V:GLA_V + GQA_Q, :], preferred_element_type=F32)
    y += jnp.dot(od_ref[...], w_ref[GLA_V + GQA_Q:, :], preferred_element_type=F32)
    o_ref[...] = s_ref[...] + mod_ref[2:3, :] * y


def _output_projection(s, mod, o_gla, gate, o_gqa, o_diff, gla_norm_g, w_out_b):
    n, d = s.shape
    nt = n // ROW_TILE
    row = lambda w: pl.BlockSpec((ROW_TILE, w), lambda i: (i, 0))
    return pl.pallas_call(
        _outproj_kernel,
        grid=(nt,),
        in_specs=[row(d),
                  pl.BlockSpec((None, 8, d), lambda i: (_row_group(i), 0, 0)),
                  pl.BlockSpec((2, ROW_TILE, GLA_V), lambda i: (0, i, 0)),
                  row(GLA_V), row(GQA_Q), row(DIFF_V),
                  _resident((1, GLA_DV)),
                  _resident((MIX_WIDTH, d))],
        out_specs=row(d),
        out_shape=jax.ShapeDtypeStruct((n, d), F32),
        compiler_params=_params("parallel"),
        name="output_projection",
    )(s, mod, o_gla, gate, o_gqa, o_diff, gla_norm_g.reshape(1, -1), w_out_b)


def _router_kernel(n_experts, s_ref, mod_ref, g_ref, rw_ref, rb_ref,
                   h_ref, eid_ref, rank_ref, w_ref, count_ref, carry_ref):
    @pl.when(pl.program_id(0) == 0)
    def _():
        carry_ref[...] = jnp.zeros_like(carry_ref)

    h = _rms(s_ref[...], g_ref[...]) * (1.0 + mod_ref[4:5, :]) + mod_ref[3:4, :]
    h_ref[...] = h
    logits = jnp.dot(h.astype(BF16), rw_ref[...], preferred_element_type=F32)
    scores = 1.0 / (1.0 + jnp.exp(-logits))
    rows = scores.shape[0]
    lane = lax.broadcasted_iota(jnp.int32, scores.shape, 1)
    lane_f = lane.astype(F32)
    cand = jnp.where(lane < n_experts, scores + rb_ref[...], -jnp.inf)
    hits = []
    for _ in range(TOP_K):
        best = jnp.max(cand, axis=-1, keepdims=True)
        first = jnp.min(jnp.where(cand == best, lane_f, float(LANES)), axis=-1, keepdims=True)
        hit = lane_f == first
        hits.append(hit)
        cand = jnp.where(hit, -jnp.inf, cand)
    chosen = functools.reduce(jnp.logical_or, hits)
    total = jnp.sum(jnp.where(chosen, scores, 0.0), axis=-1, keepdims=True)
    ri = lax.broadcasted_iota(jnp.int32, (rows, rows), 0)
    ci = lax.broadcasted_iota(jnp.int32, (rows, rows), 1)
    chosen_b = jnp.where(chosen, 1.0, 0.0).astype(BF16)
    before = jnp.dot(jnp.where(ci < ri, 1.0, 0.0).astype(BF16), chosen_b, preferred_element_type=F32)
    rank_all = before + carry_ref[...]
    for k, hit in enumerate(hits):
        pick = lambda a: jnp.sum(jnp.where(hit, a, 0.0), axis=-1, keepdims=True)
        eid_ref[:, k:k + 1] = pick(lane_f).astype(jnp.int32)
        rank_ref[:, k:k + 1] = pick(rank_all).astype(jnp.int32)
        w_ref[:, k:k + 1] = pick(scores) / total * ROUTE_SCALE
    carry_ref[...] += jnp.sum(chosen_b.astype(F32), axis=0, keepdims=True)
    count_ref[...] = carry_ref[...]


def _router(s, mod, norm_g, router_w_b, router_b, n_experts):
    n, d = s.shape
    nt = n // ROW_TILE
    row = lambda w: pl.BlockSpec((ROW_TILE, w), lambda i: (i, 0))
    return pl.pallas_call(
        functools.partial(_router_kernel, n_experts),
        grid=(nt,),
        in_specs=[row(d),
                  pl.BlockSpec((None, 8, d), lambda i: (_row_group(i), 0, 0)),
                  _resident((1, d)),
                  _resident((d, LANES)),
                  _resident((1, LANES))],
        out_specs=[row(d), row(TOP_K), row(TOP_K), row(TOP_K), pl.BlockSpec((1, LANES), lambda i: (0, 0))],
        out_shape=[jax.ShapeDtypeStruct((n, d), F32), jax.ShapeDtypeStruct((n, TOP_K), jnp.int32),
                   jax.ShapeDtypeStruct((n, TOP_K), jnp.int32), jax.ShapeDtypeStruct((n, TOP_K), F32),
                   jax.ShapeDtypeStruct((1, LANES), F32)],
        scratch_shapes=[pltpu.VMEM((1, LANES), F32)],
        compiler_params=_params("arbitrary"),
        name="router",
    )(s, mod, norm_g.reshape(1, d), router_w_b, router_b)


def _routing_plan(eid, rank, counts_f, n_experts, n_tiles):
    counts = counts_f[0, :n_experts].astype(jnp.int32)
    starts = jnp.concatenate([jnp.zeros((1,), jnp.int32), jnp.cumsum(counts)])
    experts = jnp.arange(n_experts, dtype=jnp.int32)
    dest = jnp.sum(jnp.where(eid[..., None] == experts, starts[:-1], 0), axis=-1) + rank
    first_tile = starts[:-1] // EXPERT_ROW_TILE
    last_tile = (starts[1:] - 1) // EXPERT_ROW_TILE
    visits_per_expert = jnp.where(counts > 0, last_tile - first_tile + 1, 0)
    visit_end = jnp.cumsum(visits_per_expert)
    visit_start = visit_end - visits_per_expert
    n_visits = visit_end[-1]
    v = jnp.minimum(jnp.arange(n_tiles + n_experts - 1, dtype=jnp.int32), n_visits - 1)
    visit_expert = jnp.sum((v[:, None] >= visit_end[None, :]).astype(jnp.int32), axis=-1)
    own = visit_expert[:, None] == experts[None, :]
    visit_tile = jnp.sum(jnp.where(own, (first_tile - visit_start)[None, :], 0), axis=-1) + v
    return dest, starts, visit_tile, visit_expert, n_visits.reshape(1)


def _dispatch_kernel(dest_ref, h_ref, xs_hbm, sem):
    def body(j, carry):
        for k in range(TOP_K):
            pltpu.make_async_copy(h_ref.at[pl.ds(j, 1)], xs_hbm.at[pl.ds(dest_ref[k, j], 1)], sem).start()
        return carry

    lax.fori_loop(0, ROW_TILE, body, 0)
    all_rows = xs_hbm.at[pl.ds(0, ROW_TILE * TOP_K)]
    pltpu.make_async_copy(all_rows, all_rows, sem).wait()


def _dispatch(h, dest_tiles):
    n, d = h.shape
    nt = n // ROW_TILE
    return pl.pallas_call(
        _dispatch_kernel,
        grid=(nt,),
        in_specs=[pl.BlockSpec((None, TOP_K, ROW_TILE), lambda i: (i, 0, 0), memory_space=pltpu.SMEM),
                  pl.BlockSpec((ROW_TILE, d), lambda i: (i, 0))],
        out_specs=pl.BlockSpec(memory_space=pl.ANY),
        out_shape=jax.ShapeDtypeStruct((n * TOP_K, d), F32),
        scratch_shapes=[pltpu.SemaphoreType.DMA(())],
        compiler_params=_params("arbitrary"),
        name="moe_dispatch",
    )(dest_tiles, h)


def _grouped_kernel(vt_ref, ve_ref, starts_ref, nv_ref, x_ref, wg_ref, wu_ref, wd_ref, y_ref,
                    wgb_ref, wub_ref, wdb_ref):
    v = pl.program_id(0)
    e = ve_ref[v]
    t = vt_ref[v]
    prev = jnp.maximum(v - 1, 0)

    @pl.when(jnp.logical_or(v == 0, ve_ref[prev] != e))
    def _():
        wgb_ref[...] = wg_ref[...].astype(BF16)
        wub_ref[...] = wu_ref[...].astype(BF16)
        wdb_ref[...] = wd_ref[...].astype(BF16)

    @pl.when(v < nv_ref[0])
    def _():
        x = x_ref[...].astype(BF16)
        a = jnp.dot(x, wgb_ref[...], preferred_element_type=F32)
        b = jnp.dot(x, wub_ref[...], preferred_element_type=F32)
        y = jnp.dot((_silu(a) * b).astype(BF16), wdb_ref[...], preferred_element_type=F32)
        rows = x.shape[0]
        r = t * rows + lax.broadcasted_iota(jnp.int32, (rows, 1), 0)
        y = jnp.where(jnp.logical_and(r >= starts_ref[e], r < starts_ref[e + 1]), y, 0.0)
        first_visit_of_tile = jnp.logical_or(v == 0, vt_ref[prev] != t)

        @pl.when(first_visit_of_tile)
        def _():
            y_ref[...] = y

        @pl.when(jnp.logical_not(first_visit_of_tile))
        def _():
            y_ref[...] += y


def _grouped_experts(xs, layer, w_gate, w_up, w_down, visit_tile, visit_expert, starts, n_visits):
    p, d = xs.shape
    hidden = w_gate.shape[-1]
    tm = EXPERT_ROW_TILE
    grid_spec = pltpu.PrefetchScalarGridSpec(
        num_scalar_prefetch=4,
        grid=(visit_tile.shape[0],),
        in_specs=[pl.BlockSpec((tm, d), lambda v, vt, ve, st, nv: (vt[v], 0)),
                  pl.BlockSpec((None, None, d, hidden), lambda v, vt, ve, st, nv: (layer, ve[v], 0, 0)),
                  pl.BlockSpec((None, None, d, hidden), lambda v, vt, ve, st, nv: (layer, ve[v], 0, 0)),
                  pl.BlockSpec((None, None, hidden, d), lambda v, vt, ve, st, nv: (layer, ve[v], 0, 0))],
        out_specs=pl.BlockSpec((tm, d), lambda v, vt, ve, st, nv: (vt[v], 0)),
        scratch_shapes=[pltpu.VMEM((d, hidden), BF16), pltpu.VMEM((d, hidden), BF16),
                        pltpu.VMEM((hidden, d), BF16)])
    return pl.pallas_call(
        _grouped_kernel,
        grid_spec=grid_spec,
        out_shape=jax.ShapeDtypeStruct((p, d), F32),
        compiler_params=_params("arbitrary"),
        name="moe_grouped_experts",
    )(visit_tile, visit_expert, starts, n_visits, xs, w_gate, w_up, w_down)


def _combine_kernel(n_ctx_rows, dest_ref, dest_next_ref, w_ref, h_ref, s_ref, mod_ref,
                    sg_ref, su_ref, sd_ref, ys_hbm, o_ref, ybuf_ref, sems):
    i = pl.program_id(0)
    n_steps = pl.num_programs(0)
    rows = h_ref.shape[0]
    slot = i % 2

    def gather(dref, to_slot):
        def body(j, carry):
            for k in range(TOP_K):
                pltpu.make_async_copy(ys_hbm.at[pl.ds(dref[k, j], 1)],
                                      ybuf_ref.at[to_slot, k, pl.ds(j, 1)], sems.at[to_slot]).start()
            return carry
        lax.fori_loop(0, rows, body, 0)

    @pl.when(i == 0)
    def _():
        gather(dest_ref, 0)

    @pl.when(i + 1 < n_steps)
    def _():
        gather(dest_next_ref, 1 - slot)

    hb = h_ref[...].astype(BF16)
    a = jnp.dot(hb, sg_ref[...], preferred_element_type=F32)
    b = jnp.dot(hb, su_ref[...], preferred_element_type=F32)
    y = jnp.dot((_silu(a) * b).astype(BF16), sd_ref[...], preferred_element_type=F32)

    pltpu.make_async_copy(ybuf_ref.at[slot], ybuf_ref.at[slot], sems.at[slot]).wait()
    w = w_ref[...]
    for k in range(TOP_K):
        y += w[:, k:k + 1] * ybuf_ref[slot, k]
    r = i * rows + lax.broadcasted_iota(jnp.int32, (rows, 1), 0)
    gate2 = jnp.where(r < n_ctx_rows, mod_ref[1, 5:6, :], mod_ref[0, 5:6, :])
    o_ref[...] = s_ref[...] + gate2 * y


def _combine(ys, dest_tiles, w, h, s, mod, sh_gate_b, sh_up_b, sh_down_b, n_ctx_rows):
    n, d = s.shape
    tc = COMBINE_ROW_TILE
    steps = n // tc
    hidden = sh_gate_b.shape[-1]
    row = lambda width: pl.BlockSpec((tc, width), lambda i: (i, 0))
    idx = lambda shift: pl.BlockSpec((None, TOP_K, tc), lambda i: (jnp.minimum(i + shift, steps - 1), 0, 0),
                                     memory_space=pltpu.SMEM)
    return pl.pallas_call(
        functools.partial(_combine_kernel, n_ctx_rows),
        grid=(steps,),
        in_specs=[idx(0), idx(1), row(TOP_K), row(d), row(d),
                  pl.BlockSpec((2, 8, d), lambda i: (0, 0, 0)),
                  _resident((d, hidden)), _resident((d, hidden)), _resident((hidden, d)),
                  pl.BlockSpec(memory_space=pl.ANY)],
        out_specs=row(d),
        out_shape=jax.ShapeDtypeStruct((n, d), F32),
        scratch_shapes=[pltpu.VMEM((2, TOP_K, tc, d), F32), pltpu.SemaphoreType.DMA((2,))],
        compiler_params=_params("arbitrary"),
        name="moe_combine",
    )(dest_tiles, dest_tiles, w, h, s, mod, sh_gate_b, sh_up_b, sh_down_b, ys)


def _tile_major(dest, tile):
    n, k = dest.shape
    return dest.reshape(n // tile, tile, k).transpose(0, 2, 1)


def _final_norm_kernel(s_ref, g_ref, o_ref):
    o_ref[...] = _rms(s_ref[...], g_ref[...])


def _final_norm(s, g, first_tile):
    n, d = s.shape
    nt = n // ROW_TILE - first_tile
    return pl.pallas_call(
        _final_norm_kernel,
        grid=(nt,),
        in_specs=[pl.BlockSpec((ROW_TILE, d), lambda i: (i + first_tile, 0)), _resident((1, d))],
        out_specs=pl.BlockSpec((ROW_TILE, d), lambda i: (i, 0)),
        out_shape=jax.ShapeDtypeStruct((nt * ROW_TILE, d), F32),
        compiler_params=_params("parallel"),
        name="final_norm",
    )(s, g.reshape(1, d))


def _rope_tables(n_ctx, n_tokens, dim):
    half = dim // 2
    inv_freq = ROPE_THETA ** (-jnp.arange(0, half, 2, dtype=F32) / half)
    t = jnp.arange(n_tokens, dtype=jnp.int32)
    ang_r = (t // GRID_W).astype(F32)[:, None] * inv_freq
    ang_c = (t % GRID_W).astype(F32)[:, None] * inv_freq
    cos = jnp.concatenate([jnp.cos(ang_r)] * 2 + [jnp.cos(ang_c)] * 2, axis=-1)
    sin = jnp.concatenate([-jnp.sin(ang_r), jnp.sin(ang_r), -jnp.sin(ang_c), jnp.sin(ang_c)], axis=-1)
    reps = LANES // dim
    cos = jnp.tile(cos, (1, reps))
    sin = jnp.tile(sin, (1, reps))
    cos = jnp.concatenate([jnp.ones((n_ctx, LANES), F32), cos], axis=0)
    sin = jnp.concatenate([jnp.zeros((n_ctx, LANES), F32), sin], axis=0)
    return cos, sin


def kernel(x, c, ctx, c_ctx, norm1_g, norm2_g, w_mod, b_mod, w_in, gla_wa_f, gla_ba_f, gla_wa_b,
           gla_ba_b, gla_norm_g, q_norm_g, k_norm_g, diff_lq1, diff_lk1, diff_lq2, diff_lk2,
           diff_norm_g, w_out, router_w, router_b, exp_w_gate, exp_w_up, exp_w_down,
           sh_w_gate, sh_w_up, sh_w_down, final_g):
    batch, n_tokens, d = x.shape
    n_ctx = ctx.shape[1]
    depth = w_mod.shape[0]
    n_experts = router_w.shape[-1]
    n_rows = n_ctx + n_tokens
    assert batch == 1 and n_ctx == ROW_TILE and n_tokens % ROW_TILE == 0
    assert TOP_K <= n_experts <= LANES

    s = jnp.concatenate([ctx[0], x[0]], axis=0)
    mod_all = _modulation(c, c_ctx, w_mod, b_mod)
    tables = _rope_tables(n_ctx, n_tokens, HEAD_DIM) + _rope_tables(n_ctx, n_tokens, DIFF_DQK)

    for l in range(depth):
        first_q_tile = 0
        lam_init = 0.8 - 0.6 * math.exp(-0.3 * l)
        mod = mod_all[l]

        w_in_b = jnp.concatenate(
            [w_in[l][:, :ORIG_LR], w_in[l][:, ORIG_LR + 2 * GLA_GATE_RANK:],
             w_in[l][:, ORIG_LR:ORIG_LR + 2 * GLA_GATE_RANK],
             jnp.zeros((d, LANES - 2 * GLA_GATE_RANK), F32)], axis=1).astype(BF16)
        wa = jnp.zeros((LANES, 2 * GLA_QK), F32)
        wa = wa.at[:GLA_GATE_RANK, :GLA_QK].set(gla_wa_f[l])
        wa = wa.at[GLA_GATE_RANK:2 * GLA_GATE_RANK, GLA_QK:].set(gla_wa_b[l]).astype(BF16)
        ba = jnp.concatenate([gla_ba_f[l], gla_ba_b[l]]).reshape(1, -1)
        lamv = jnp.zeros((8, LANES), F32)
        for r, vec in enumerate((diff_lq1[l], diff_lk1[l], diff_lq2[l], diff_lk2[l])):
            lamv = lamv.at[r, :DIFF_DQK].set(vec)
        rw = jnp.pad(router_w[l], ((0, 0), (0, LANES - n_experts))).astype(BF16)
        rb = jnp.pad(router_b[l], (0, LANES - n_experts)).reshape(1, LANES)

        (glaq, glak, glav, gate, la, q_t, k, v_t, dq_t, dk, dv_t) = _input_projection(
            s, mod, norm1_g[l], w_in_b, wa, ba, q_norm_g[l], k_norm_g[l], tables)
        o_gla = _gla(glaq, glak, glav, la)
        o_gqa = _gqa_attention(q_t, k, v_t, first_q_tile)
        o_diff = _diff_attention(dq_t, dk, dv_t, lamv, diff_norm_g[l], lam_init, first_q_tile)
        s = _output_projection(s, mod, o_gla, gate, o_gqa, o_diff, gla_norm_g[l], w_out[l].astype(BF16))
        h2, eid, rank, w_route, counts = _router(s, mod, norm2_g[l], rw, rb, n_experts)
        dest, starts, visit_tile, visit_expert, n_visits = _routing_plan(
            eid, rank, counts, n_experts, n_rows * TOP_K // EXPERT_ROW_TILE)
        xs = _dispatch(h2, _tile_major(dest, ROW_TILE))
        ys = _grouped_experts(xs, l, exp_w_gate, exp_w_up, exp_w_down,
                              visit_tile, visit_expert, starts, n_visits)
        s = _combine(ys, _tile_major(dest, COMBINE_ROW_TILE), w_route, h2, s, mod,
                     sh_w_gate[l].astype(BF16), sh_w_up[l].astype(BF16), sh_w_down[l].astype(BF16), n_ctx)

    out = _final_norm(s, final_g, n_ctx // ROW_TILE)
    return out.reshape(batch, n_tokens, d)
```

```python
import functools
import math

import jax
import jax.numpy as jnp
from jax import lax
from jax.experimental import pallas as pl
from jax.experimental.pallas import tpu as pltpu

F32 = jnp.float32
BF16 = jnp.bfloat16

GRID_W = 64
HEAD_DIM = 128
GLA_HEADS = 4
GLA_DK = 64
GLA_DV = 128
GLA_GATE_RANK = 16
GLA_TAU = 16.0
GLA_CHUNK = 64
GQA_HEADS = 8
GQA_KV_HEADS = 2
DIFF_HEADS = 4
DIFF_DQK = 64
DIFF_DV = 128
TOP_K = 8
ROUTE_SCALE = 2.5
ROPE_THETA = 10000.0
NORM_EPS = 1e-6
LOG2_E = math.log2(math.e)

LANES = 128
SUBLANES = 8
ROW_TILE = 256
EXPERT_ROW_TILE = 256
COMBINE_ROW_TILE = 128
VMEM_LIMIT = 56 * 1024 * 1024

GLA_QK = GLA_HEADS * GLA_DK
GLA_V = GLA_HEADS * GLA_DV
GQA_Q = GQA_HEADS * HEAD_DIM
GQA_KV = GQA_KV_HEADS * HEAD_DIM
DIFF_QK = DIFF_HEADS * 2 * DIFF_DQK
DIFF_V = DIFF_HEADS * DIFF_DV
MIX_WIDTH = GLA_V + GQA_Q + DIFF_V

C_GLAQ = 0
C_GLAK = C_GLAQ + GLA_QK
C_GLAV = C_GLAK + GLA_QK
C_GATE = C_GLAV + GLA_V
C_GQAQ = C_GATE + GLA_V
C_GQAK = C_GQAQ + GQA_Q
C_GQAV = C_GQAK + GQA_KV
C_DQ = C_GQAV + GQA_KV
C_DK = C_DQ + DIFF_QK
C_DV = C_DK + DIFF_QK
C_LR = C_DV + DIFF_V
IN_COLS = C_LR + LANES
ORIG_LR = 2 * GLA_QK + 2 * GLA_V


def _params(*sem):
    return pltpu.CompilerParams(dimension_semantics=sem, vmem_limit_bytes=VMEM_LIMIT)


def _resident(shape):
    nd = len(shape)
    return pl.BlockSpec(shape, lambda *_: (0,) * nd, pipeline_mode=pl.Buffered(1))


def _silu(a):
    return a / (1.0 + jnp.exp(-a))


def _rms(x, g):
    return x * lax.rsqrt(jnp.mean(x * x, axis=-1, keepdims=True) + NORM_EPS) * g


def _row_group(i):
    return jnp.where(i == 0, 1, 0)


def _mod_kernel(a_ref, w_ref, b_ref, o_ref):
    a = _silu(a_ref[...])
    o_ref[...] = jnp.dot(a.astype(BF16), w_ref[...].astype(BF16),
                         preferred_element_type=F32) + b_ref[...]


def _modulation(c, c_ctx, w_mod, b_mod):
    depth, d, six_d = w_mod.shape
    a = jnp.zeros((8, d), F32).at[0].set(c[0]).at[1].set(c_ctx)
    tn = d // 2
    out = pl.pallas_call(
        _mod_kernel,
        grid=(depth, six_d // tn),
        in_specs=[pl.BlockSpec((8, d), lambda l, j: (0, 0)),
                  pl.BlockSpec((None, d, tn), lambda l, j: (l, 0, j)),
                  pl.BlockSpec((None, 1, tn), lambda l, j: (l, 0, j))],
        out_specs=pl.BlockSpec((None, 8, tn), lambda l, j: (l, 0, j)),
        out_shape=jax.ShapeDtypeStruct((depth, 8, six_d), F32),
        compiler_params=_params("parallel", "parallel"),
        name="modulation",
    )(a, w_mod, b_mod.reshape(depth, 1, six_d))
    m = out[:, :2].reshape(depth, 2, 6, d)
    return jnp.pad(m, ((0, 0), (0, 0), (0, 2), (0, 0)))


def _rope(xh, cos, sin, first, shift_first, shift_second):
    partner = jnp.where(first, pltpu.roll(xh, shift_first, 1), pltpu.roll(xh, shift_second, 1))
    return xh * cos + partner * sin


def _inproj_kernel(x_ref, mod_ref, g_ref, w_ref, wa_ref, ba_ref, qg_ref, kg_ref,
                   cg_ref, sg_ref, cd_ref, sd_ref,
                   glaq_ref, glak_ref, glav_ref, gate_ref, la_ref,
                   qt_ref, k_ref, vt_ref, dqt_ref, dk_ref, dvt_ref):
    x = x_ref[...]
    h = _rms(x, g_ref[...]) * (1.0 + mod_ref[1:2, :]) + mod_ref[0:1, :]
    hb = h.astype(BF16)

    def proj(start, width):
        return jnp.dot(hb, w_ref[:, start:start + width], preferred_element_type=F32)

    glaq_ref[...] = proj(C_GLAQ, GLA_QK) * (GLA_DK ** -0.5)
    glak_ref[...] = proj(C_GLAK, GLA_QK)
    glav_ref[...] = proj(C_GLAV, GLA_V)
    gate_ref[...] = proj(C_GATE, GLA_V)
    zv = proj(C_GQAV, GQA_KV)
    for hd in range(GQA_KV_HEADS):
        sl = slice(hd * HEAD_DIM, (hd + 1) * HEAD_DIM)
        vt_ref[sl, :] = zv[:, sl].T.astype(BF16)
    zdv = proj(C_DV, DIFF_V)
    for hd in range(DIFF_HEADS):
        sl = slice(hd * DIFF_DV, (hd + 1) * DIFF_DV)
        dvt_ref[sl, :] = zdv[:, sl].T.astype(BF16)

    z_lr = proj(C_LR, LANES).astype(BF16)
    pre = jnp.dot(z_lr, wa_ref[...], preferred_element_type=F32) + ba_ref[...]
    log_sig = -(jnp.maximum(-pre, 0.0) + jnp.log1p(jnp.exp(-jnp.abs(pre))))
    la = log_sig * (1.0 / GLA_TAU)
    la_ref[0] = la[:, :GLA_QK]
    la_ref[1] = la[:, GLA_QK:]

    rows = x.shape[0]
    lane = lax.broadcasted_iota(jnp.int32, (rows, LANES), 1)
    first_g = (lane % 64) < 32
    first_d = (lane % 32) < 16
    cg, sg, cd, sd = cg_ref[...], sg_ref[...], cd_ref[...], sd_ref[...]
    scale_g = HEAD_DIM ** -0.5 * LOG2_E
    scale_d = DIFF_DQK ** -0.5 * LOG2_E

    zq = proj(C_GQAQ, GQA_Q)
    for hd in range(GQA_HEADS):
        sl = slice(hd * HEAD_DIM, (hd + 1) * HEAD_DIM)
        qh = _rope(_rms(zq[:, sl], qg_ref[...]), cg, sg, first_g, 96, 32) * scale_g
        qt_ref[sl, :] = qh.T.astype(BF16)
    zk = proj(C_GQAK, GQA_KV)
    for hd in range(GQA_KV_HEADS):
        kh = _rms(zk[:, hd * HEAD_DIM:(hd + 1) * HEAD_DIM], kg_ref[...])
        k_ref[:, hd * HEAD_DIM:(hd + 1) * HEAD_DIM] = _rope(kh, cg, sg, first_g, 96, 32).astype(BF16)
    zdq = proj(C_DQ, DIFF_QK)
    zdk = proj(C_DK, DIFF_QK)
    for hd in range(DIFF_HEADS):
        sl = slice(hd * LANES, (hd + 1) * LANES)
        dqt_ref[sl, :] = (_rope(zdq[:, sl], cd, sd, first_d, 112, 16) * scale_d).T.astype(BF16)
        dk_ref[:, sl] = _rope(zdk[:, sl], cd, sd, first_d, 112, 16).astype(BF16)


def _input_projection(s, mod, norm_g, w_in_b, wa, ba, q_norm_g, k_norm_g, tables):
    n, d = s.shape
    nt = n // ROW_TILE
    row = lambda w: pl.BlockSpec((ROW_TILE, w), lambda i: (i, 0))
    col = lambda w: pl.BlockSpec((w, ROW_TILE), lambda i: (0, i))
    tile_t = lambda w: pl.BlockSpec((None, w, ROW_TILE), lambda i: (i, 0, 0))
    f32o = lambda w: jax.ShapeDtypeStruct((n, w), F32)
    bfo = lambda w: jax.ShapeDtypeStruct((n, w), BF16)
    return pl.pallas_call(
        _inproj_kernel,
        grid=(nt,),
        in_specs=[row(d),
                  pl.BlockSpec((None, 8, d), lambda i: (_row_group(i), 0, 0)),
                  _resident((1, d)),
                  _resident((d, IN_COLS)),
                  _resident((LANES, 2 * GLA_QK)),
                  _resident((1, 2 * GLA_QK)),
                  _resident((1, HEAD_DIM)),
                  _resident((1, HEAD_DIM)),
                  row(LANES), row(LANES), row(LANES), row(LANES)],
        out_specs=[row(GLA_QK), row(GLA_QK), row(GLA_V), row(GLA_V),
                   pl.BlockSpec((2, ROW_TILE, GLA_QK), lambda i: (0, i, 0)),
                   col(GQA_Q), row(GQA_KV), tile_t(GQA_KV), col(DIFF_QK), row(DIFF_QK), tile_t(DIFF_V)],
        out_shape=[f32o(GLA_QK), f32o(GLA_QK), f32o(GLA_V), f32o(GLA_V),
                   jax.ShapeDtypeStruct((2, n, GLA_QK), F32),
                   jax.ShapeDtypeStruct((GQA_Q, n), BF16), bfo(GQA_KV),
                   jax.ShapeDtypeStruct((nt, GQA_KV, ROW_TILE), BF16),
                   jax.ShapeDtypeStruct((DIFF_QK, n), BF16), bfo(DIFF_QK),
                   jax.ShapeDtypeStruct((nt, DIFF_V, ROW_TILE), BF16)],
        compiler_params=_params("parallel"),
        name="input_projection",
    )(s, mod, norm_g.reshape(1, d), w_in_b, wa, ba, q_norm_g.reshape(1, -1), k_norm_g.reshape(1, -1),
      *tables)


def _gla_block(reverse, q_ref, k_ref, v_ref, la_ref, o_ref, st_ref):
    rows = q_ref.shape[0]
    n_chunks = rows // GLA_CHUNK
    ri = lax.broadcasted_iota(jnp.int32, (rows, rows), 0)
    ci = lax.broadcasted_iota(jnp.int32, (rows, rows), 1)
    same_chunk = (ri // GLA_CHUNK) == (ci // GLA_CHUNK)
    allowed = same_chunk & ((ci >= ri) if reverse else (ci <= ri))
    la = la_ref[...]
    cum = jnp.dot(jnp.where(allowed, 1.0, 0.0).astype(F32), la,
                  preferred_element_type=F32, precision=lax.Precision.HIGHEST)
    tot = jnp.dot(jnp.where(same_chunk, 1.0, 0.0).astype(F32), la,
                  preferred_element_type=F32, precision=lax.Precision.HIGHEST)
    k = k_ref[...]
    q_dec = q_ref[...] * jnp.exp(cum)
    k_inv = (k * jnp.exp(-cum)).astype(BF16)
    k_end = k * jnp.exp(tot - cum)
    v = v_ref[...]
    vb = v.astype(BF16)
    v_t = v.T.astype(BF16)
    lane_head = lax.broadcasted_iota(jnp.int32, (rows, GLA_QK), 1) // GLA_DK
    row_chunk = lax.broadcasted_iota(jnp.int32, (rows, GLA_QK), 0) // GLA_CHUNK
    nt_dims = (((1,), (1,)), ((), ()))

    for hd in range(GLA_HEADS):
        qh = jnp.where(lane_head == hd, q_dec, 0.0).astype(BF16)
        a = lax.dot_general(qh, k_inv, nt_dims, preferred_element_type=F32)
        a = jnp.where(allowed, a, 0.0).astype(BF16)
        o_ref[:, hd * GLA_DV:(hd + 1) * GLA_DV] = jnp.dot(
            a, vb[:, hd * GLA_DV:(hd + 1) * GLA_DV], preferred_element_type=F32)

    chunk_order = range(n_chunks - 1, -1, -1) if reverse else range(n_chunks)
    chunk_lane_head = lax.broadcasted_iota(jnp.int32, (GLA_CHUNK, GLA_QK), 1) // GLA_DK
    for c in chunk_order:
        sl = slice(c * GLA_CHUNK, (c + 1) * GLA_CHUNK)
        state = st_ref[...]
        q4 = jnp.concatenate(
            [jnp.where(chunk_lane_head == hd, q_dec[sl], 0.0) for hd in range(GLA_HEADS)],
            axis=0).astype(BF16)
        r = lax.dot_general(q4, state.astype(BF16), nt_dims, preferred_element_type=F32)
        for hd in range(GLA_HEADS):
            o_ref[sl, hd * GLA_DV:(hd + 1) * GLA_DV] += r[hd * GLA_CHUNK:(hd + 1) * GLA_CHUNK,
                                                          hd * GLA_DV:(hd + 1) * GLA_DV]
        k_end_c = jnp.where(row_chunk == c, k_end, 0.0).astype(BF16)
        u_t = jnp.dot(v_t, k_end_c, preferred_element_type=F32)
        st_ref[...] = state * jnp.exp(tot[c * GLA_CHUNK:c * GLA_CHUNK + 1]) + u_t


def _gla_kernel(q_ref, k_ref, v_ref, la_ref, o_ref, st_ref):
    @pl.when(pl.program_id(1) == 0)
    def _():
        st_ref[...] = jnp.zeros_like(st_ref)

    @pl.when(pl.program_id(0) == 0)
    def _():
        _gla_block(False, q_ref, k_ref, v_ref, la_ref, o_ref, st_ref)

    @pl.when(pl.program_id(0) == 1)
    def _():
        _gla_block(True, q_ref, k_ref, v_ref, la_ref, o_ref, st_ref)


def _gla(glaq, glak, glav, la):
    n = glaq.shape[0]
    nt = n // ROW_TILE

    def blk(dr, j):
        return jnp.where(dr == 0, j, jnp.where(j == 0, 0, nt - j))

    row = lambda w: pl.BlockSpec((ROW_TILE, w), lambda dr, j: (blk(dr, j), 0))
    return pl.pallas_call(
        _gla_kernel,
        grid=(2, nt),
        in_specs=[row(GLA_QK), row(GLA_QK), row(GLA_V),
                  pl.BlockSpec((None, ROW_TILE, GLA_QK), lambda dr, j: (dr, blk(dr, j), 0))],
        out_specs=pl.BlockSpec((None, ROW_TILE, GLA_V), lambda dr, j: (dr, blk(dr, j), 0)),
        out_shape=jax.ShapeDtypeStruct((2, n, GLA_V), F32),
        scratch_shapes=[pltpu.VMEM((GLA_V, GLA_QK), F32)],
        compiler_params=_params("arbitrary", "arbitrary"),
        name="gla_scan",
    )(glaq, glak, glav, la)


class _FlashMaps:
    def __init__(self, q_maps, k_of_map, load_k, load_vt):
        self.q_maps, self.k_of_map, self.load_k, self.load_vt = q_maps, k_of_map, load_k, load_vt


def _flash_steps(maps, first_tile, tiles_per_step, n_steps, s_ref, m_ref, l_ref, acc_ref):
    n_maps = len(maps.q_maps)
    rows = tiles_per_step * ROW_TILE
    for u in range(n_steps):
        first_row = pl.multiple_of((first_tile + u * tiles_per_step) * ROW_TILE, ROW_TILE)
        keys = {src: maps.load_k(src, first_row, rows) for src in sorted(set(maps.k_of_map))}
        for j in range(n_maps):
            s_ref[u * n_maps + j, :rows] = jnp.dot(keys[maps.k_of_map[j]], maps.q_maps[j],
                                                   preferred_element_type=F32)
    for u in range(n_steps):
        tile = first_tile + u * tiles_per_step
        for j in range(n_maps):
            s_t = s_ref[u * n_maps + j, :rows]
            m_prev = m_ref[j]
            m_new = jnp.maximum(m_prev, jnp.max(s_t, axis=0, keepdims=True))
            alpha = jnp.exp2(m_prev - m_new)
            p = jnp.exp2(s_t - m_new)
            l_ref[j] = alpha * l_ref[j] + jnp.sum(p, axis=0, keepdims=True)
            pb = p.astype(BF16)
            pv = jnp.dot(maps.load_vt(maps.k_of_map[j], tile), pb[:ROW_TILE], preferred_element_type=F32)
            for r in range(1, tiles_per_step):
                pv += jnp.dot(maps.load_vt(maps.k_of_map[j], tile + r), pb[r * ROW_TILE:(r + 1) * ROW_TILE],
                              preferred_element_type=F32)
            acc_ref[j] = alpha * acc_ref[j] + pv
            m_ref[j] = m_new


def _flash_attend(maps, first_q_tile, n_key_tiles, tiles_per_step, n_steps, s_ref, m_ref, l_ref, acc_ref):
    m_ref[...] = jnp.full(m_ref.shape, -jnp.inf, F32)
    l_ref[...] = jnp.zeros(l_ref.shape, F32)
    acc_ref[...] = jnp.zeros(acc_ref.shape, F32)
    _flash_steps(maps, 0, 1, 1, s_ref, m_ref, l_ref, acc_ref)
    per_iter = tiles_per_step * n_steps
    qi = pl.program_id(1) + first_q_tile
    n_iter = jnp.where(qi == 0, 0, (n_key_tiles - 1) // per_iter)

    def body(it, carry):
        _flash_steps(maps, 1 + it * per_iter, tiles_per_step, n_steps, s_ref, m_ref, l_ref, acc_ref)
        return carry

    lax.fori_loop(0, n_iter, body, 0)


def _flash_plan(n_key_tiles):
    latent = n_key_tiles - 1
    tiles_per_step = 2 if latent % 2 == 0 else 1
    n_steps = 2 if latent % (2 * tiles_per_step) == 0 else 1
    return tiles_per_step, n_steps


def _flash_scratch(n_maps, dv, plan):
    tiles_per_step, n_steps = plan
    return [pltpu.VMEM((n_steps * n_maps, tiles_per_step * ROW_TILE, ROW_TILE), F32),
            pltpu.VMEM((n_maps, 1, ROW_TILE), F32), pltpu.VMEM((n_maps, 1, ROW_TILE), F32),
            pltpu.VMEM((n_maps, dv, ROW_TILE), F32)]


GQA_GROUP = GQA_HEADS // GQA_KV_HEADS


def _gqa_kernel(first_q_tile, plan, qt_ref, k_ref, vt_ref, o_ref, s_ref, m_ref, l_ref, acc_ref):
    maps = _FlashMaps(
        q_maps=[qt_ref[hd * HEAD_DIM:(hd + 1) * HEAD_DIM, :] for hd in range(GQA_GROUP)],
        k_of_map=[0] * GQA_GROUP,
        load_k=lambda src, first_row, rows: k_ref[pl.ds(first_row, rows), :],
        load_vt=lambda src, tile: vt_ref[tile])
    _flash_attend(maps, first_q_tile, vt_ref.shape[0], *plan, s_ref, m_ref, l_ref, acc_ref)
    for hd in range(GQA_GROUP):
        o_t = acc_ref[hd] / l_ref[hd]
        o_ref[:, hd * HEAD_DIM:(hd + 1) * HEAD_DIM] = o_t.T.astype(o_ref.dtype)


def _gqa_attention(q_t, k, v_t, first_q_tile):
    n = k.shape[0]
    nt = n // ROW_TILE
    nq = nt - first_q_tile
    gw = GQA_GROUP * HEAD_DIM
    plan = _flash_plan(nt)
    return pl.pallas_call(
        functools.partial(_gqa_kernel, first_q_tile, plan),
        grid=(GQA_KV_HEADS, nq),
        in_specs=[pl.BlockSpec((gw, ROW_TILE), lambda g, i: (g, i + first_q_tile)),
                  pl.BlockSpec((n, HEAD_DIM), lambda g, i: (0, g)),
                  pl.BlockSpec((nt, HEAD_DIM, ROW_TILE), lambda g, i: (0, g, 0))],
        out_specs=pl.BlockSpec((ROW_TILE, gw), lambda g, i: (i + first_q_tile, g)),
        out_shape=jax.ShapeDtypeStruct((n, GQA_Q), BF16),
        scratch_shapes=_flash_scratch(GQA_GROUP, HEAD_DIM, plan),
        compiler_params=_params("parallel", "arbitrary"),
        name="gqa_attention",
    )(q_t, k, v_t)


DIFF_PAIR = 2


def _diff_kernel(first_q_tile, plan, lam_init, qt_ref, k_ref, vt_ref, lamv_ref, g_ref, o_ref,
                 s_ref, m_ref, l_ref, acc_ref):
    row = lax.broadcasted_iota(jnp.int32, (LANES, ROW_TILE), 0)
    q_maps = []
    for hd in range(DIFF_PAIR):
        q_t = qt_ref[hd * LANES:(hd + 1) * LANES, :]
        zero = jnp.zeros_like(q_t)
        q_maps += [jnp.where(row < DIFF_DQK, q_t, zero), jnp.where(row >= DIFF_DQK, q_t, zero)]
    maps = _FlashMaps(
        q_maps=q_maps,
        k_of_map=[hd for hd in range(DIFF_PAIR) for _ in range(2)],
        load_k=lambda src, first_row, rows: k_ref[pl.ds(first_row, rows), src * LANES:(src + 1) * LANES],
        load_vt=lambda src, tile: vt_ref[tile, src * DIFF_DV:(src + 1) * DIFF_DV, :])
    _flash_attend(maps, first_q_tile, vt_ref.shape[0], *plan, s_ref, m_ref, l_ref, acc_ref)
    lv = lamv_ref[...]
    lam = (jnp.exp(jnp.sum(lv[0:1] * lv[1:2], axis=-1, keepdims=True))
           - jnp.exp(jnp.sum(lv[2:3] * lv[3:4], axis=-1, keepdims=True)) + lam_init)
    for hd in range(DIFF_PAIR):
        o_t = acc_ref[2 * hd] / l_ref[2 * hd] - lam * (acc_ref[2 * hd + 1] / l_ref[2 * hd + 1])
        o = _rms(o_t.T, g_ref[...]) * (1.0 - lam_init)
        o_ref[:, hd * DIFF_DV:(hd + 1) * DIFF_DV] = o.astype(o_ref.dtype)


def _diff_attention(q_t, k, v_t, lamv, norm_g, lam_init, first_q_tile):
    n = k.shape[0]
    nt = n // ROW_TILE
    nq = nt - first_q_tile
    pw = DIFF_PAIR * LANES
    plan = _flash_plan(nt)
    return pl.pallas_call(
        functools.partial(_diff_kernel, first_q_tile, plan, lam_init),
        grid=(DIFF_HEADS // DIFF_PAIR, nq),
        in_specs=[pl.BlockSpec((pw, ROW_TILE), lambda h, i: (h, i + first_q_tile)),
                  pl.BlockSpec((n, pw), lambda h, i: (0, h)),
                  pl.BlockSpec((nt, DIFF_PAIR * DIFF_DV, ROW_TILE), lambda h, i: (0, h, 0)),
                  pl.BlockSpec((8, LANES), lambda h, i: (0, 0)),
                  pl.BlockSpec((1, DIFF_DV), lambda h, i: (0, 0))],
        out_specs=pl.BlockSpec((ROW_TILE, DIFF_PAIR * DIFF_DV), lambda h, i: (i + first_q_tile, h)),
        out_shape=jax.ShapeDtypeStruct((n, DIFF_V), BF16),
        scratch_shapes=_flash_scratch(2 * DIFF_PAIR, DIFF_DV, plan),
        compiler_params=_params("parallel", "arbitrary"),
        name="diff_attention",
    )(q_t, k, v_t, lamv, norm_g.reshape(1, -1))


def _outproj_kernel(s_ref, mod_ref, og_ref, gate_ref, oq_ref, od_ref, gg_ref, w_ref, o_ref):
    og = og_ref[0] + og_ref[1]
    gate = gate_ref[...]
    y = jnp.zeros(s_ref.shape, F32)
    for hd in range(GLA_HEADS):
        sl = slice(hd * GLA_DV, (hd + 1) * GLA_DV)
        oh = _rms(og[:, sl], gg_ref[...]) * _silu(gate[:, sl])
        y += jnp.dot(oh.astype(BF16), w_ref[hd * GLA_DV:(hd + 1) * GLA_DV, :], preferred_element_type=F32)
    y += jnp.dot(oq_ref[...], w_ref[GLA_V:GLA_V + GQA_Q, :], preferred_element_type=F32)
    y += jnp.dot(od_ref[...], w_ref[GLA_V + GQA_Q:, :], preferred_element_type=F32)
    o_ref[...] = s_ref[...] + mod_ref[2:3, :] * y


def _output_projection(s, mod, o_gla, gate, o_gqa, o_diff, gla_norm_g, w_out_b):
    n, d = s.shape
    nt = n // ROW_TILE
    row = lambda w: pl.BlockSpec((ROW_TILE, w), lambda i: (i, 0))
    return pl.pallas_call(
        _outproj_kernel,
        grid=(nt,),
        in_specs=[row(d),
                  pl.BlockSpec((None, 8, d), lambda i: (_row_group(i), 0, 0)),
                  pl.BlockSpec((2, ROW_TILE, GLA_V), lambda i: (0, i, 0)),
                  row(GLA_V), row(GQA_Q), row(DIFF_V),
                  _resident((1, GLA_DV)),
                  _resident((MIX_WIDTH, d))],
        out_specs=row(d),
        out_shape=jax.ShapeDtypeStruct((n, d), F32),
        compiler_params=_params("parallel"),
        name="output_projection",
    )(s, mod, o_gla, gate, o_gqa, o_diff, gla_norm_g.reshape(1, -1), w_out_b)


def _router_kernel(n_experts, s_ref, mod_ref, g_ref, rw_ref, rb_ref,
                   h_ref, eid_ref, rank_ref, w_ref, count_ref, carry_ref):
    @pl.when(pl.program_id(0) == 0)
    def _():
        carry_ref[...] = jnp.zeros_like(carry_ref)

    h = _rms(s_ref[...], g_ref[...]) * (1.0 + mod_ref[4:5, :]) + mod_ref[3:4, :]
    h_ref[...] = h
    logits = jnp.dot(h.astype(BF16), rw_ref[...], preferred_element_type=F32)
    scores = 1.0 / (1.0 + jnp.exp(-logits))
    rows = scores.shape[0]
    lane = lax.broadcasted_iota(jnp.int32, scores.shape, 1)
    lane_f = lane.astype(F32)
    cand = jnp.where(lane < n_experts, scores + rb_ref[...], -jnp.inf)
    hits = []
    for _ in range(TOP_K):
        best = jnp.max(cand, axis=-1, keepdims=True)
        first = jnp.min(jnp.where(cand == best, lane_f, float(LANES)), axis=-1, keepdims=True)
        hit = lane_f == first
        hits.append(hit)
        cand = jnp.where(hit, -jnp.inf, cand)
    chosen = functools.reduce(jnp.logical_or, hits)
    total = jnp.sum(jnp.where(chosen, scores, 0.0), axis=-1, keepdims=True)
    ri = lax.broadcasted_iota(jnp.int32, (rows, rows), 0)
    ci = lax.broadcasted_iota(jnp.int32, (rows, rows), 1)
    chosen_b = jnp.where(chosen, 1.0, 0.0).astype(BF16)
    before = jnp.dot(jnp.where(ci < ri, 1.0, 0.0).astype(BF16), chosen_b, preferred_element_type=F32)
    rank_all = before + carry_ref[...]
    for k, hit in enumerate(hits):
        pick = lambda a: jnp.sum(jnp.where(hit, a, 0.0), axis=-1, keepdims=True)
        eid_ref[:, k:k + 1] = pick(lane_f).astype(jnp.int32)
        rank_ref[:, k:k + 1] = pick(rank_all).astype(jnp.int32)
        w_ref[:, k:k + 1] = pick(scores) / total * ROUTE_SCALE
    carry_ref[...] += jnp.sum(chosen_b.astype(F32), axis=0, keepdims=True)
    count_ref[...] = carry_ref[...]


def _router(s, mod, norm_g, router_w_b, router_b, n_experts):
    n, d = s.shape
    nt = n // ROW_TILE
    row = lambda w: pl.BlockSpec((ROW_TILE, w), lambda i: (i, 0))
    return pl.pallas_call(
        functools.partial(_router_kernel, n_experts),
        grid=(nt,),
        in_specs=[row(d),
                  pl.BlockSpec((None, 8, d), lambda i: (_row_group(i), 0, 0)),
                  _resident((1, d)),
                  _resident((d, LANES)),
                  _resident((1, LANES))],
        out_specs=[row(d), row(TOP_K), row(TOP_K), row(TOP_K), pl.BlockSpec((1, LANES), lambda i: (0, 0))],
        out_shape=[jax.ShapeDtypeStruct((n, d), F32), jax.ShapeDtypeStruct((n, TOP_K), jnp.int32),
                   jax.ShapeDtypeStruct((n, TOP_K), jnp.int32), jax.ShapeDtypeStruct((n, TOP_K), F32),
                   jax.ShapeDtypeStruct((1, LANES), F32)],
        scratch_shapes=[pltpu.VMEM((1, LANES), F32)],
        compiler_params=_params("arbitrary"),
        name="router",
    )(s, mod, norm_g.reshape(1, d), router_w_b, router_b)


def _routing_plan(eid, rank, counts_f, n_experts):
    counts = counts_f[0, :n_experts].astype(jnp.int32)
    padded = (counts + SUBLANES - 1) // SUBLANES * SUBLANES
    starts = jnp.concatenate([jnp.zeros((1,), jnp.int32), jnp.cumsum(padded)])
    experts = jnp.arange(n_experts, dtype=jnp.int32)
    dest = jnp.sum(jnp.where(eid[..., None] == experts, starts[:-1], 0), axis=-1) + rank
    gap = jnp.arange(SUBLANES, dtype=jnp.int32)[None, :]
    in_gap = gap < (padded - counts)[:, None]
    gap_row = (starts[:-1] + counts)[:, None] + gap
    n_slack_before = jnp.cumsum(jnp.logical_not(in_gap).reshape(-1).astype(jnp.int32)) - 1
    filler = jnp.where(in_gap.reshape(-1), gap_row.reshape(-1), starts[-1] + n_slack_before)
    return dest, starts, counts, filler.astype(jnp.int32)


def _wait_rows(ref_hbm, n_rows, sem):
    rows = ref_hbm.at[pl.ds(0, n_rows)]
    pltpu.make_async_copy(rows, rows, sem).wait()


def _dispatch_kernel(n_filler, filler_ref, dest_ref, h_ref, xs_hbm, sem):
    def row_to(j, d):
        return pltpu.make_async_copy(h_ref.at[pl.ds(j, 1)], xs_hbm.at[pl.ds(d, 1)], sem)

    def body(j, carry):
        for k in range(TOP_K):
            row_to(j, dest_ref[k, j]).start()
        return carry

    lax.fori_loop(0, ROW_TILE, body, 0)
    _wait_rows(xs_hbm, ROW_TILE * TOP_K, sem)

    @pl.when(pl.program_id(0) == 0)
    def _():
        def fill(f, carry):
            row_to(0, filler_ref[f]).start()
            return carry

        lax.fori_loop(0, n_filler, fill, 0)
        tail = xs_hbm.shape[0] - EXPERT_ROW_TILE
        tail_copy = pltpu.make_async_copy(h_ref.at[pl.ds(0, EXPERT_ROW_TILE)],
                                          xs_hbm.at[pl.ds(tail, EXPERT_ROW_TILE)], sem)
        tail_copy.start()
        _wait_rows(xs_hbm, n_filler, sem)
        tail_copy.wait()


def _dispatch(h, dest_tiles, filler):
    n, d = h.shape
    nt = n // ROW_TILE
    n_filler = filler.shape[0]
    grid_spec = pltpu.PrefetchScalarGridSpec(
        num_scalar_prefetch=1,
        grid=(nt,),
        in_specs=[pl.BlockSpec((None, TOP_K, ROW_TILE), lambda i, f: (i, 0, 0), memory_space=pltpu.SMEM),
                  pl.BlockSpec((ROW_TILE, d), lambda i, f: (i, 0))],
        out_specs=pl.BlockSpec(memory_space=pl.ANY),
        scratch_shapes=[pltpu.SemaphoreType.DMA(())])
    return pl.pallas_call(
        functools.partial(_dispatch_kernel, n_filler),
        grid_spec=grid_spec,
        out_shape=jax.ShapeDtypeStruct((n * TOP_K + n_filler + EXPERT_ROW_TILE, d), F32),
        compiler_params=_params("arbitrary"),
        name="moe_dispatch",
    )(filler, dest_tiles, h)


def _grouped_kernel(n_fill, starts_ref, counts_ref, wg_ref, wu_ref, wd_ref, xs_hbm, ys_hbm,
                    wgb_ref, wub_ref, wdb_ref, xbuf_ref, ybuf_ref, xsem, ysem):
    e = pl.program_id(0)
    tm = EXPERT_ROW_TILE
    base = pl.multiple_of(starts_ref[e], SUBLANES)
    n_windows = (counts_ref[e] + tm - 1) // tm

    def window(ref_hbm, c):
        return ref_hbm.at[pl.ds(pl.multiple_of(base + c * tm, SUBLANES), tm)]

    def x_copy(c, slot):
        return pltpu.make_async_copy(window(xs_hbm, c), xbuf_ref.at[slot], xsem.at[slot])

    def y_copy(c, slot):
        return pltpu.make_async_copy(ybuf_ref.at[slot], window(ys_hbm, c), ysem.at[slot])

    @pl.when(n_windows > 0)
    def _():
        x_copy(0, 0).start()
        wgb_ref[...] = wg_ref[...].astype(BF16)
        wub_ref[...] = wu_ref[...].astype(BF16)
        wdb_ref[...] = wd_ref[...].astype(BF16)

        def body(c, carry):
            slot = c % 2
            x_copy(c, slot).wait()

            @pl.when(c + 1 < n_windows)
            def _():
                x_copy(c + 1, 1 - slot).start()

            x = xbuf_ref[slot].astype(BF16)
            a = jnp.dot(x, wgb_ref[...], preferred_element_type=F32)
            b = jnp.dot(x, wub_ref[...], preferred_element_type=F32)
            y = jnp.dot((_silu(a) * b).astype(BF16), wdb_ref[...], preferred_element_type=F32)

            @pl.when(c > 0)
            def _():
                y_copy(c - 1, 1 - slot).wait()

            ybuf_ref[slot] = y
            y_copy(c, slot).start()
            return carry

        lax.fori_loop(0, n_windows, body, 0)
        y_copy(n_windows - 1, (n_windows - 1) % 2).wait()

    @pl.when(e == pl.num_programs(0) - 1)
    def _():
        ybuf_ref[0] = jnp.zeros(ybuf_ref.shape[1:], F32)
        end = starts_ref[e + 1]
        last_start = ys_hbm.shape[0] - tm
        for i in range(n_fill):
            start = pl.multiple_of(jnp.minimum(end + i * tm, last_start), SUBLANES)
            fill = pltpu.make_async_copy(ybuf_ref.at[0], ys_hbm.at[pl.ds(start, tm)], ysem.at[0])
            fill.start()
            fill.wait()


def _grouped_experts(xs, layer, w_gate, w_up, w_down, starts, counts):
    p, d = xs.shape
    n_experts, hidden = w_gate.shape[1], w_gate.shape[-1]
    tm = EXPERT_ROW_TILE
    n_fill = -(-(SUBLANES * n_experts + tm) // tm)
    weight = lambda shape: pl.BlockSpec((None, None) + shape, lambda e, st, ct: (layer, e, 0, 0))
    grid_spec = pltpu.PrefetchScalarGridSpec(
        num_scalar_prefetch=2,
        grid=(n_experts,),
        in_specs=[weight((d, hidden)), weight((d, hidden)), weight((hidden, d)),
                  pl.BlockSpec(memory_space=pl.ANY)],
        out_specs=pl.BlockSpec(memory_space=pl.ANY),
        scratch_shapes=[pltpu.VMEM((d, hidden), BF16), pltpu.VMEM((d, hidden), BF16),
                        pltpu.VMEM((hidden, d), BF16),
                        pltpu.VMEM((2, tm, d), F32), pltpu.VMEM((2, tm, d), F32),
                        pltpu.SemaphoreType.DMA((2,)), pltpu.SemaphoreType.DMA((2,))])
    return pl.pallas_call(
        functools.partial(_grouped_kernel, n_fill),
        grid_spec=grid_spec,
        out_shape=jax.ShapeDtypeStruct((p, d), F32),
        compiler_params=_params("arbitrary"),
        name="moe_grouped_experts",
    )(starts, counts, w_gate, w_up, w_down, xs)


def _combine_kernel(n_ctx_rows, dest_ref, dest_next_ref, w_ref, h_ref, s_ref, mod_ref,
                    sg_ref, su_ref, sd_ref, ys_hbm, o_ref, ybuf_ref, sems):
    i = pl.program_id(0)
    n_steps = pl.num_programs(0)
    rows = h_ref.shape[0]
    slot = i % 2

    def gather(dref, to_slot):
        def body(j, carry):
            for k in range(TOP_K):
                pltpu.make_async_copy(ys_hbm.at[pl.ds(dref[k, j], 1)],
                                      ybuf_ref.at[to_slot, k, pl.ds(j, 1)], sems.at[to_slot]).start()
            return carry
        lax.fori_loop(0, rows, body, 0)

    @pl.when(i == 0)
    def _():
        gather(dest_ref, 0)

    @pl.when(i + 1 < n_steps)
    def _():
        gather(dest_next_ref, 1 - slot)

    hb = h_ref[...].astype(BF16)
    a = jnp.dot(hb, sg_ref[...], preferred_element_type=F32)
    b = jnp.dot(hb, su_ref[...], preferred_element_type=F32)
    y = jnp.dot((_silu(a) * b).astype(BF16), sd_ref[...], preferred_element_type=F32)

    pltpu.make_async_copy(ybuf_ref.at[slot], ybuf_ref.at[slot], sems.at[slot]).wait()
    w = w_ref[...]
    for k in range(TOP_K):
        y += w[:, k:k + 1] * ybuf_ref[slot, k]
    r = i * rows + lax.broadcasted_iota(jnp.int32, (rows, 1), 0)
    gate2 = jnp.where(r < n_ctx_rows, mod_ref[1, 5:6, :], mod_ref[0, 5:6, :])
    o_ref[...] = s_ref[...] + gate2 * y


def _combine(ys, dest_tiles, w, h, s, mod, sh_gate_b, sh_up_b, sh_down_b, n_ctx_rows):
    n, d = s.shape
    tc = COMBINE_ROW_TILE
    steps = n // tc
    hidden = sh_gate_b.shape[-1]
    row = lambda width: pl.BlockSpec((tc, width), lambda i: (i, 0))
    idx = lambda shift: pl.BlockSpec((None, TOP_K, tc), lambda i: (jnp.minimum(i + shift, steps - 1), 0, 0),
                                     memory_space=pltpu.SMEM)
    return pl.pallas_call(
        functools.partial(_combine_kernel, n_ctx_rows),
        grid=(steps,),
        in_specs=[idx(0), idx(1), row(TOP_K), row(d), row(d),
                  pl.BlockSpec((2, 8, d), lambda i: (0, 0, 0)),
                  _resident((d, hidden)), _resident((d, hidden)), _resident((hidden, d)),
                  pl.BlockSpec(memory_space=pl.ANY)],
        out_specs=row(d),
        out_shape=jax.ShapeDtypeStruct((n, d), F32),
        scratch_shapes=[pltpu.VMEM((2, TOP_K, tc, d), F32), pltpu.SemaphoreType.DMA((2,))],
        compiler_params=_params("arbitrary"),
        name="moe_combine",
    )(dest_tiles, dest_tiles, w, h, s, mod, sh_gate_b, sh_up_b, sh_down_b, ys)


def _tile_major(dest, tile):
    n, k = dest.shape
    return dest.reshape(n // tile, tile, k).transpose(0, 2, 1)


def _final_norm_kernel(s_ref, g_ref, o_ref):
    o_ref[...] = _rms(s_ref[...], g_ref[...])


def _final_norm(s, g, first_tile):
    n, d = s.shape
    nt = n // ROW_TILE - first_tile
    return pl.pallas_call(
        _final_norm_kernel,
        grid=(nt,),
        in_specs=[pl.BlockSpec((ROW_TILE, d), lambda i: (i + first_tile, 0)), _resident((1, d))],
        out_specs=pl.BlockSpec((ROW_TILE, d), lambda i: (i, 0)),
        out_shape=jax.ShapeDtypeStruct((nt * ROW_TILE, d), F32),
        compiler_params=_params("parallel"),
        name="final_norm",
    )(s, g.reshape(1, d))


def _rope_tables(n_ctx, n_tokens, dim):
    half = dim // 2
    inv_freq = ROPE_THETA ** (-jnp.arange(0, half, 2, dtype=F32) / half)
    t = jnp.arange(n_tokens, dtype=jnp.int32)
    ang_r = (t // GRID_W).astype(F32)[:, None] * inv_freq
    ang_c = (t % GRID_W).astype(F32)[:, None] * inv_freq
    cos = jnp.concatenate([jnp.cos(ang_r)] * 2 + [jnp.cos(ang_c)] * 2, axis=-1)
    sin = jnp.concatenate([-jnp.sin(ang_r), jnp.sin(ang_r), -jnp.sin(ang_c), jnp.sin(ang_c)], axis=-1)
    reps = LANES // dim
    cos = jnp.tile(cos, (1, reps))
    sin = jnp.tile(sin, (1, reps))
    cos = jnp.concatenate([jnp.ones((n_ctx, LANES), F32), cos], axis=0)
    sin = jnp.concatenate([jnp.zeros((n_ctx, LANES), F32), sin], axis=0)
    return cos, sin


def kernel(x, c, ctx, c_ctx, norm1_g, norm2_g, w_mod, b_mod, w_in, gla_wa_f, gla_ba_f, gla_wa_b,
           gla_ba_b, gla_norm_g, q_norm_g, k_norm_g, diff_lq1, diff_lk1, diff_lq2, diff_lk2,
           diff_norm_g, w_out, router_w, router_b, exp_w_gate, exp_w_up, exp_w_down,
           sh_w_gate, sh_w_up, sh_w_down, final_g):
    batch, n_tokens, d = x.shape
    n_ctx = ctx.shape[1]
    depth = w_mod.shape[0]
    n_experts = router_w.shape[-1]
    n_rows = n_ctx + n_tokens
    assert batch == 1 and n_ctx == ROW_TILE and n_tokens % ROW_TILE == 0
    assert TOP_K <= n_experts <= LANES and EXPERT_ROW_TILE <= ROW_TILE

    s = jnp.concatenate([ctx[0], x[0]], axis=0)
    mod_all = _modulation(c, c_ctx, w_mod, b_mod)
    tables = _rope_tables(n_ctx, n_tokens, HEAD_DIM) + _rope_tables(n_ctx, n_tokens, DIFF_DQK)

    for l in range(depth):
        first_q_tile = 0
        lam_init = 0.8 - 0.6 * math.exp(-0.3 * l)
        mod = mod_all[l]

        w_in_b = jnp.concatenate(
            [w_in[l][:, :ORIG_LR], w_in[l][:, ORIG_LR + 2 * GLA_GATE_RANK:],
             w_in[l][:, ORIG_LR:ORIG_LR + 2 * GLA_GATE_RANK],
             jnp.zeros((d, LANES - 2 * GLA_GATE_RANK), F32)], axis=1).astype(BF16)
        wa = jnp.zeros((LANES, 2 * GLA_QK), F32)
        wa = wa.at[:GLA_GATE_RANK, :GLA_QK].set(gla_wa_f[l])
        wa = wa.at[GLA_GATE_RANK:2 * GLA_GATE_RANK, GLA_QK:].set(gla_wa_b[l]).astype(BF16)
        ba = jnp.concatenate([gla_ba_f[l], gla_ba_b[l]]).reshape(1, -1)
        lamv = jnp.zeros((8, LANES), F32)
        for r, vec in enumerate((diff_lq1[l], diff_lk1[l], diff_lq2[l], diff_lk2[l])):
            lamv = lamv.at[r, :DIFF_DQK].set(vec)
        rw = jnp.pad(router_w[l], ((0, 0), (0, LANES - n_experts))).astype(BF16)
        rb = jnp.pad(router_b[l], (0, LANES - n_experts)).reshape(1, LANES)

        (glaq, glak, glav, gate, la, q_t, k, v_t, dq_t, dk, dv_t) = _input_projection(
            s, mod, norm1_g[l], w_in_b, wa, ba, q_norm_g[l], k_norm_g[l], tables)
        o_gla = _gla(glaq, glak, glav, la)
        o_gqa = _gqa_attention(q_t, k, v_t, first_q_tile)
        o_diff = _diff_attention(dq_t, dk, dv_t, lamv, diff_norm_g[l], lam_init, first_q_tile)
        s = _output_projection(s, mod, o_gla, gate, o_gqa, o_diff, gla_norm_g[l], w_out[l].astype(BF16))
        h2, eid, rank, w_route, counts_f = _router(s, mod, norm2_g[l], rw, rb, n_experts)
        dest, starts, counts, filler = _routing_plan(eid, rank, counts_f, n_experts)
        xs = _dispatch(h2, _tile_major(dest, ROW_TILE), filler)
        ys = _grouped_experts(xs, l, exp_w_gate, exp_w_up, exp_w_down, starts, counts)
        s = _combine(ys, _tile_major(dest, COMBINE_ROW_TILE), w_route, h2, s, mod,
                     sh_w_gate[l].astype(BF16), sh_w_up[l].astype(BF16), sh_w_down[l].astype(BF16), n_ctx)

    out = _final_norm(s, final_g, n_ctx // ROW_TILE)
    return out.reshape(batch, n_tokens, d)
```

```python
import functools
import math

import jax
import jax.numpy as jnp
from jax import lax
from jax.experimental import pallas as pl
from jax.experimental.pallas import tpu as pltpu

F32 = jnp.float32
BF16 = jnp.bfloat16

GRID_W = 64
HEAD_DIM = 128
GLA_HEADS = 4
GLA_DK = 64
GLA_DV = 128
GLA_GATE_RANK = 16
GLA_TAU = 16.0
GLA_CHUNK = 64
GQA_HEADS = 8
GQA_KV_HEADS = 2
DIFF_HEADS = 4
DIFF_DQK = 64
DIFF_DV = 128
TOP_K = 8
ROUTE_SCALE = 2.5
ROPE_THETA = 10000.0
NORM_EPS = 1e-6
LOG2_E = math.log2(math.e)

LANES = 128
SUBLANES = 8
ROW_TILE = 256
EXPERT_ROW_TILE = 256
COMBINE_ROW_TILE = 128
WINDOW_DMA_PRIORITY = 1
VMEM_LIMIT = 56 * 1024 * 1024

GLA_QK = GLA_HEADS * GLA_DK
GLA_V = GLA_HEADS * GLA_DV
GQA_Q = GQA_HEADS * HEAD_DIM
GQA_KV = GQA_KV_HEADS * HEAD_DIM
DIFF_QK = DIFF_HEADS * 2 * DIFF_DQK
DIFF_V = DIFF_HEADS * DIFF_DV
MIX_WIDTH = GLA_V + GQA_Q + DIFF_V

C_GLAQ = 0
C_GLAK = C_GLAQ + GLA_QK
C_GLAV = C_GLAK + GLA_QK
C_GATE = C_GLAV + GLA_V
C_GQAQ = C_GATE + GLA_V
C_GQAK = C_GQAQ + GQA_Q
C_GQAV = C_GQAK + GQA_KV
C_DQ = C_GQAV + GQA_KV
C_DK = C_DQ + DIFF_QK
C_DV = C_DK + DIFF_QK
C_LR = C_DV + DIFF_V
IN_COLS = C_LR + LANES
ORIG_LR = 2 * GLA_QK + 2 * GLA_V


def _params(*sem):
    return pltpu.CompilerParams(dimension_semantics=sem, vmem_limit_bytes=VMEM_LIMIT)


def _resident(shape):
    nd = len(shape)
    return pl.BlockSpec(shape, lambda *_: (0,) * nd, pipeline_mode=pl.Buffered(1))


def _silu(a):
    return a / (1.0 + jnp.exp(-a))


def _rms(x, g):
    return x * lax.rsqrt(jnp.mean(x * x, axis=-1, keepdims=True) + NORM_EPS) * g


def _row_group(i):
    return jnp.where(i == 0, 1, 0)


def _mod_kernel(a_ref, w_ref, b_ref, o_ref):
    a = _silu(a_ref[...])
    o_ref[...] = jnp.dot(a.astype(BF16), w_ref[...].astype(BF16),
                         preferred_element_type=F32) + b_ref[...]


def _modulation(c, c_ctx, w_mod, b_mod):
    depth, d, six_d = w_mod.shape
    a = jnp.zeros((8, d), F32).at[0].set(c[0]).at[1].set(c_ctx)
    tn = d // 2
    out = pl.pallas_call(
        _mod_kernel,
        grid=(depth, six_d // tn),
        in_specs=[pl.BlockSpec((8, d), lambda l, j: (0, 0)),
                  pl.BlockSpec((None, d, tn), lambda l, j: (l, 0, j)),
                  pl.BlockSpec((None, 1, tn), lambda l, j: (l, 0, j))],
        out_specs=pl.BlockSpec((None, 8, tn), lambda l, j: (l, 0, j)),
        out_shape=jax.ShapeDtypeStruct((depth, 8, six_d), F32),
        compiler_params=_params("parallel", "parallel"),
        name="modulation",
    )(a, w_mod, b_mod.reshape(depth, 1, six_d))
    m = out[:, :2].reshape(depth, 2, 6, d)
    return jnp.pad(m, ((0, 0), (0, 0), (0, 2), (0, 0)))


def _rope(xh, cos, sin, first, shift_first, shift_second):
    partner = jnp.where(first, pltpu.roll(xh, shift_first, 1), pltpu.roll(xh, shift_second, 1))
    return xh * cos + partner * sin


def _inproj_kernel(x_ref, mod_ref, g_ref, w_ref, wa_ref, ba_ref, qg_ref, kg_ref,
                   cg_ref, sg_ref, cd_ref, sd_ref,
                   glaq_ref, glak_ref, glav_ref, gate_ref, la_ref,
                   qt_ref, k_ref, vt_ref, dqt_ref, dk_ref, dvt_ref):
    x = x_ref[...]
    h = _rms(x, g_ref[...]) * (1.0 + mod_ref[1:2, :]) + mod_ref[0:1, :]
    hb = h.astype(BF16)

    def proj(start, width):
        return jnp.dot(hb, w_ref[:, start:start + width], preferred_element_type=F32)

    glaq_ref[...] = proj(C_GLAQ, GLA_QK) * (GLA_DK ** -0.5)
    glak_ref[...] = proj(C_GLAK, GLA_QK)
    glav_ref[...] = proj(C_GLAV, GLA_V)
    gate_ref[...] = proj(C_GATE, GLA_V)
    zv = proj(C_GQAV, GQA_KV)
    for hd in range(GQA_KV_HEADS):
        sl = slice(hd * HEAD_DIM, (hd + 1) * HEAD_DIM)
        vt_ref[sl, :] = zv[:, sl].T.astype(BF16)
    zdv = proj(C_DV, DIFF_V)
    for hd in range(DIFF_HEADS):
        sl = slice(hd * DIFF_DV, (hd + 1) * DIFF_DV)
        dvt_ref[sl, :] = zdv[:, sl].T.astype(BF16)

    z_lr = proj(C_LR, LANES).astype(BF16)
    pre = jnp.dot(z_lr, wa_ref[...], preferred_element_type=F32) + ba_ref[...]
    log_sig = -(jnp.maximum(-pre, 0.0) + jnp.log1p(jnp.exp(-jnp.abs(pre))))
    la = log_sig * (1.0 / GLA_TAU)
    la_ref[0] = la[:, :GLA_QK]
    la_ref[1] = la[:, GLA_QK:]

    rows = x.shape[0]
    lane = lax.broadcasted_iota(jnp.int32, (rows, LANES), 1)
    first_g = (lane % 64) < 32
    first_d = (lane % 32) < 16
    cg, sg, cd, sd = cg_ref[...], sg_ref[...], cd_ref[...], sd_ref[...]
    scale_g = HEAD_DIM ** -0.5 * LOG2_E
    scale_d = DIFF_DQK ** -0.5 * LOG2_E

    zq = proj(C_GQAQ, GQA_Q)
    for hd in range(GQA_HEADS):
        sl = slice(hd * HEAD_DIM, (hd + 1) * HEAD_DIM)
        qh = _rope(_rms(zq[:, sl], qg_ref[...]), cg, sg, first_g, 96, 32) * scale_g
        qt_ref[sl, :] = qh.T.astype(BF16)
    zk = proj(C_GQAK, GQA_KV)
    for hd in range(GQA_KV_HEADS):
        kh = _rms(zk[:, hd * HEAD_DIM:(hd + 1) * HEAD_DIM], kg_ref[...])
        k_ref[:, hd * HEAD_DIM:(hd + 1) * HEAD_DIM] = _rope(kh, cg, sg, first_g, 96, 32).astype(BF16)
    zdq = proj(C_DQ, DIFF_QK)
    zdk = proj(C_DK, DIFF_QK)
    for hd in range(DIFF_HEADS):
        sl = slice(hd * LANES, (hd + 1) * LANES)
        dqt_ref[sl, :] = (_rope(zdq[:, sl], cd, sd, first_d, 112, 16) * scale_d).T.astype(BF16)
        dk_ref[:, sl] = _rope(zdk[:, sl], cd, sd, first_d, 112, 16).astype(BF16)


def _input_projection(s, mod, norm_g, w_in_b, wa, ba, q_norm_g, k_norm_g, tables):
    n, d = s.shape
    nt = n // ROW_TILE
    row = lambda w: pl.BlockSpec((ROW_TILE, w), lambda i: (i, 0))
    col = lambda w: pl.BlockSpec((w, ROW_TILE), lambda i: (0, i))
    tile_t = lambda w: pl.BlockSpec((None, w, ROW_TILE), lambda i: (i, 0, 0))
    f32o = lambda w: jax.ShapeDtypeStruct((n, w), F32)
    bfo = lambda w: jax.ShapeDtypeStruct((n, w), BF16)
    return pl.pallas_call(
        _inproj_kernel,
        grid=(nt,),
        in_specs=[row(d),
                  pl.BlockSpec((None, 8, d), lambda i: (_row_group(i), 0, 0)),
                  _resident((1, d)),
                  _resident((d, IN_COLS)),
                  _resident((LANES, 2 * GLA_QK)),
                  _resident((1, 2 * GLA_QK)),
                  _resident((1, HEAD_DIM)),
                  _resident((1, HEAD_DIM)),
                  row(LANES), row(LANES), row(LANES), row(LANES)],
        out_specs=[row(GLA_QK), row(GLA_QK), row(GLA_V), row(GLA_V),
                   pl.BlockSpec((2, ROW_TILE, GLA_QK), lambda i: (0, i, 0)),
                   col(GQA_Q), row(GQA_KV), tile_t(GQA_KV), col(DIFF_QK), row(DIFF_QK), tile_t(DIFF_V)],
        out_shape=[f32o(GLA_QK), f32o(GLA_QK), f32o(GLA_V), f32o(GLA_V),
                   jax.ShapeDtypeStruct((2, n, GLA_QK), F32),
                   jax.ShapeDtypeStruct((GQA_Q, n), BF16), bfo(GQA_KV),
                   jax.ShapeDtypeStruct((nt, GQA_KV, ROW_TILE), BF16),
                   jax.ShapeDtypeStruct((DIFF_QK, n), BF16), bfo(DIFF_QK),
                   jax.ShapeDtypeStruct((nt, DIFF_V, ROW_TILE), BF16)],
        compiler_params=_params("parallel"),
        name="input_projection",
    )(s, mod, norm_g.reshape(1, d), w_in_b, wa, ba, q_norm_g.reshape(1, -1), k_norm_g.reshape(1, -1),
      *tables)


def _gla_block(reverse, q_ref, k_ref, v_ref, la_ref, o_ref, st_ref):
    rows = q_ref.shape[0]
    n_chunks = rows // GLA_CHUNK
    ri = lax.broadcasted_iota(jnp.int32, (rows, rows), 0)
    ci = lax.broadcasted_iota(jnp.int32, (rows, rows), 1)
    same_chunk = (ri // GLA_CHUNK) == (ci // GLA_CHUNK)
    allowed = same_chunk & ((ci >= ri) if reverse else (ci <= ri))
    la = la_ref[...]
    cum = jnp.dot(jnp.where(allowed, 1.0, 0.0).astype(F32), la,
                  preferred_element_type=F32, precision=lax.Precision.HIGHEST)
    tot = jnp.dot(jnp.where(same_chunk, 1.0, 0.0).astype(F32), la,
                  preferred_element_type=F32, precision=lax.Precision.HIGHEST)
    k = k_ref[...]
    q_dec = q_ref[...] * jnp.exp(cum)
    k_inv = (k * jnp.exp(-cum)).astype(BF16)
    k_end = k * jnp.exp(tot - cum)
    v = v_ref[...]
    vb = v.astype(BF16)
    v_t = v.T.astype(BF16)
    lane_head = lax.broadcasted_iota(jnp.int32, (rows, GLA_QK), 1) // GLA_DK
    row_chunk = lax.broadcasted_iota(jnp.int32, (rows, GLA_QK), 0) // GLA_CHUNK
    nt_dims = (((1,), (1,)), ((), ()))

    for hd in range(GLA_HEADS):
        qh = jnp.where(lane_head == hd, q_dec, 0.0).astype(BF16)
        a = lax.dot_general(qh, k_inv, nt_dims, preferred_element_type=F32)
        a = jnp.where(allowed, a, 0.0).astype(BF16)
        o_ref[:, hd * GLA_DV:(hd + 1) * GLA_DV] = jnp.dot(
            a, vb[:, hd * GLA_DV:(hd + 1) * GLA_DV], preferred_element_type=F32)

    chunk_order = range(n_chunks - 1, -1, -1) if reverse else range(n_chunks)
    chunk_lane_head = lax.broadcasted_iota(jnp.int32, (GLA_CHUNK, GLA_QK), 1) // GLA_DK
    for c in chunk_order:
        sl = slice(c * GLA_CHUNK, (c + 1) * GLA_CHUNK)
        state = st_ref[...]
        q4 = jnp.concatenate(
            [jnp.where(chunk_lane_head == hd, q_dec[sl], 0.0) for hd in range(GLA_HEADS)],
            axis=0).astype(BF16)
        r = lax.dot_general(q4, state.astype(BF16), nt_dims, preferred_element_type=F32)
        for hd in range(GLA_HEADS):
            o_ref[sl, hd * GLA_DV:(hd + 1) * GLA_DV] += r[hd * GLA_CHUNK:(hd + 1) * GLA_CHUNK,
                                                          hd * GLA_DV:(hd + 1) * GLA_DV]
        k_end_c = jnp.where(row_chunk == c, k_end, 0.0).astype(BF16)
        u_t = jnp.dot(v_t, k_end_c, preferred_element_type=F32)
        st_ref[...] = state * jnp.exp(tot[c * GLA_CHUNK:c * GLA_CHUNK + 1]) + u_t


def _gla_kernel(q_ref, k_ref, v_ref, la_ref, o_ref, st_ref):
    @pl.when(pl.program_id(1) == 0)
    def _():
        st_ref[...] = jnp.zeros_like(st_ref)

    @pl.when(pl.program_id(0) == 0)
    def _():
        _gla_block(False, q_ref, k_ref, v_ref, la_ref, o_ref, st_ref)

    @pl.when(pl.program_id(0) == 1)
    def _():
        _gla_block(True, q_ref, k_ref, v_ref, la_ref, o_ref, st_ref)


def _gla(glaq, glak, glav, la):
    n = glaq.shape[0]
    nt = n // ROW_TILE

    def blk(dr, j):
        return jnp.where(dr == 0, j, jnp.where(j == 0, 0, nt - j))

    row = lambda w: pl.BlockSpec((ROW_TILE, w), lambda dr, j: (blk(dr, j), 0))
    return pl.pallas_call(
        _gla_kernel,
        grid=(2, nt),
        in_specs=[row(GLA_QK), row(GLA_QK), row(GLA_V),
                  pl.BlockSpec((None, ROW_TILE, GLA_QK), lambda dr, j: (dr, blk(dr, j), 0))],
        out_specs=pl.BlockSpec((None, ROW_TILE, GLA_V), lambda dr, j: (dr, blk(dr, j), 0)),
        out_shape=jax.ShapeDtypeStruct((2, n, GLA_V), F32),
        scratch_shapes=[pltpu.VMEM((GLA_V, GLA_QK), F32)],
        compiler_params=_params("arbitrary", "arbitrary"),
        name="gla_scan",
    )(glaq, glak, glav, la)


class _FlashMaps:
    def __init__(self, q_maps, k_of_map, load_k, load_vt):
        self.q_maps, self.k_of_map, self.load_k, self.load_vt = q_maps, k_of_map, load_k, load_vt


def _flash_steps(maps, first_tile, tiles_per_step, n_steps, s_ref, m_ref, l_ref, acc_ref):
    n_maps = len(maps.q_maps)
    rows = tiles_per_step * ROW_TILE
    for u in range(n_steps):
        first_row = pl.multiple_of((first_tile + u * tiles_per_step) * ROW_TILE, ROW_TILE)
        keys = {src: maps.load_k(src, first_row, rows) for src in sorted(set(maps.k_of_map))}
        for j in range(n_maps):
            s_ref[u * n_maps + j, :rows] = jnp.dot(keys[maps.k_of_map[j]], maps.q_maps[j],
                                                   preferred_element_type=F32)
    for u in range(n_steps):
        tile = first_tile + u * tiles_per_step
        for j in range(n_maps):
            s_t = s_ref[u * n_maps + j, :rows]
            m_prev = m_ref[j]
            m_new = jnp.maximum(m_prev, jnp.max(s_t, axis=0, keepdims=True))
            alpha = jnp.exp2(m_prev - m_new)
            p = jnp.exp2(s_t - m_new)
            l_ref[j] = alpha * l_ref[j] + jnp.sum(p, axis=0, keepdims=True)
            pb = p.astype(BF16)
            pv = jnp.dot(maps.load_vt(maps.k_of_map[j], tile), pb[:ROW_TILE], preferred_element_type=F32)
            for r in range(1, tiles_per_step):
                pv += jnp.dot(maps.load_vt(maps.k_of_map[j], tile + r), pb[r * ROW_TILE:(r + 1) * ROW_TILE],
                              preferred_element_type=F32)
            acc_ref[j] = alpha * acc_ref[j] + pv
            m_ref[j] = m_new


def _flash_attend(maps, first_q_tile, n_key_tiles, tiles_per_step, n_steps, s_ref, m_ref, l_ref, acc_ref):
    m_ref[...] = jnp.full(m_ref.shape, -jnp.inf, F32)
    l_ref[...] = jnp.zeros(l_ref.shape, F32)
    acc_ref[...] = jnp.zeros(acc_ref.shape, F32)
    _flash_steps(maps, 0, 1, 1, s_ref, m_ref, l_ref, acc_ref)
    per_iter = tiles_per_step * n_steps
    qi = pl.program_id(1) + first_q_tile
    n_iter = jnp.where(qi == 0, 0, (n_key_tiles - 1) // per_iter)

    def body(it, carry):
        _flash_steps(maps, 1 + it * per_iter, tiles_per_step, n_steps, s_ref, m_ref, l_ref, acc_ref)
        return carry

    lax.fori_loop(0, n_iter, body, 0)


def _flash_plan(n_key_tiles):
    latent = n_key_tiles - 1
    tiles_per_step = 2 if latent % 2 == 0 else 1
    n_steps = 2 if latent % (2 * tiles_per_step) == 0 else 1
    return tiles_per_step, n_steps


def _flash_scratch(n_maps, dv, plan):
    tiles_per_step, n_steps = plan
    return [pltpu.VMEM((n_steps * n_maps, tiles_per_step * ROW_TILE, ROW_TILE), F32),
            pltpu.VMEM((n_maps, 1, ROW_TILE), F32), pltpu.VMEM((n_maps, 1, ROW_TILE), F32),
            pltpu.VMEM((n_maps, dv, ROW_TILE), F32)]


GQA_GROUP = GQA_HEADS // GQA_KV_HEADS


def _gqa_kernel(first_q_tile, plan, qt_ref, k_ref, vt_ref, o_ref, s_ref, m_ref, l_ref, acc_ref):
    maps = _FlashMaps(
        q_maps=[qt_ref[hd * HEAD_DIM:(hd + 1) * HEAD_DIM, :] for hd in range(GQA_GROUP)],
        k_of_map=[0] * GQA_GROUP,
        load_k=lambda src, first_row, rows: k_ref[pl.ds(first_row, rows), :],
        load_vt=lambda src, tile: vt_ref[tile])
    _flash_attend(maps, first_q_tile, vt_ref.shape[0], *plan, s_ref, m_ref, l_ref, acc_ref)
    for hd in range(GQA_GROUP):
        o_t = acc_ref[hd] / l_ref[hd]
        o_ref[:, hd * HEAD_DIM:(hd + 1) * HEAD_DIM] = o_t.T.astype(o_ref.dtype)


def _gqa_attention(q_t, k, v_t, first_q_tile):
    n = k.shape[0]
    nt = n // ROW_TILE
    nq = nt - first_q_tile
    gw = GQA_GROUP * HEAD_DIM
    plan = _flash_plan(nt)
    return pl.pallas_call(
        functools.partial(_gqa_kernel, first_q_tile, plan),
        grid=(GQA_KV_HEADS, nq),
        in_specs=[pl.BlockSpec((gw, ROW_TILE), lambda g, i: (g, i + first_q_tile)),
                  pl.BlockSpec((n, HEAD_DIM), lambda g, i: (0, g)),
                  pl.BlockSpec((nt, HEAD_DIM, ROW_TILE), lambda g, i: (0, g, 0))],
        out_specs=pl.BlockSpec((ROW_TILE, gw), lambda g, i: (i + first_q_tile, g)),
        out_shape=jax.ShapeDtypeStruct((n, GQA_Q), BF16),
        scratch_shapes=_flash_scratch(GQA_GROUP, HEAD_DIM, plan),
        compiler_params=_params("parallel", "arbitrary"),
        name="gqa_attention",
    )(q_t, k, v_t)


DIFF_PAIR = 2


def _diff_kernel(first_q_tile, plan, lam_init, qt_ref, k_ref, vt_ref, lamv_ref, g_ref, o_ref,
                 s_ref, m_ref, l_ref, acc_ref):
    row = lax.broadcasted_iota(jnp.int32, (LANES, ROW_TILE), 0)
    q_maps = []
    for hd in range(DIFF_PAIR):
        q_t = qt_ref[hd * LANES:(hd + 1) * LANES, :]
        zero = jnp.zeros_like(q_t)
        q_maps += [jnp.where(row < DIFF_DQK, q_t, zero), jnp.where(row >= DIFF_DQK, q_t, zero)]
    maps = _FlashMaps(
        q_maps=q_maps,
        k_of_map=[hd for hd in range(DIFF_PAIR) for _ in range(2)],
        load_k=lambda src, first_row, rows: k_ref[pl.ds(first_row, rows), src * LANES:(src + 1) * LANES],
        load_vt=lambda src, tile: vt_ref[tile, src * DIFF_DV:(src + 1) * DIFF_DV, :])
    _flash_attend(maps, first_q_tile, vt_ref.shape[0], *plan, s_ref, m_ref, l_ref, acc_ref)
    lv = lamv_ref[...]
    lam = (jnp.exp(jnp.sum(lv[0:1] * lv[1:2], axis=-1, keepdims=True))
           - jnp.exp(jnp.sum(lv[2:3] * lv[3:4], axis=-1, keepdims=True)) + lam_init)
    for hd in range(DIFF_PAIR):
        o_t = acc_ref[2 * hd] / l_ref[2 * hd] - lam * (acc_ref[2 * hd + 1] / l_ref[2 * hd + 1])
        o = _rms(o_t.T, g_ref[...]) * (1.0 - lam_init)
        o_ref[:, hd * DIFF_DV:(hd + 1) * DIFF_DV] = o.astype(o_ref.dtype)


def _diff_attention(q_t, k, v_t, lamv, norm_g, lam_init, first_q_tile):
    n = k.shape[0]
    nt = n // ROW_TILE
    nq = nt - first_q_tile
    pw = DIFF_PAIR * LANES
    plan = _flash_plan(nt)
    return pl.pallas_call(
        functools.partial(_diff_kernel, first_q_tile, plan, lam_init),
        grid=(DIFF_HEADS // DIFF_PAIR, nq),
        in_specs=[pl.BlockSpec((pw, ROW_TILE), lambda h, i: (h, i + first_q_tile)),
                  pl.BlockSpec((n, pw), lambda h, i: (0, h)),
                  pl.BlockSpec((nt, DIFF_PAIR * DIFF_DV, ROW_TILE), lambda h, i: (0, h, 0)),
                  pl.BlockSpec((8, LANES), lambda h, i: (0, 0)),
                  pl.BlockSpec((1, DIFF_DV), lambda h, i: (0, 0))],
        out_specs=pl.BlockSpec((ROW_TILE, DIFF_PAIR * DIFF_DV), lambda h, i: (i + first_q_tile, h)),
        out_shape=jax.ShapeDtypeStruct((n, DIFF_V), BF16),
        scratch_shapes=_flash_scratch(2 * DIFF_PAIR, DIFF_DV, plan),
        compiler_params=_params("parallel", "arbitrary"),
        name="diff_attention",
    )(q_t, k, v_t, lamv, norm_g.reshape(1, -1))


def _outproj_kernel(s_ref, mod_ref, og_ref, gate_ref, oq_ref, od_ref, gg_ref, w_ref, o_ref):
    og = og_ref[0] + og_ref[1]
    gate = gate_ref[...]
    y = jnp.zeros(s_ref.shape, F32)
    for hd in range(GLA_HEADS):
        sl = slice(hd * GLA_DV, (hd + 1) * GLA_DV)
        oh = _rms(og[:, sl], gg_ref[...]) * _silu(gate[:, sl])
        y += jnp.dot(oh.astype(BF16), w_ref[hd * GLA_DV:(hd + 1) * GLA_DV, :], preferred_element_type=F32)
    y += jnp.dot(oq_ref[...], w_ref[GLA_V:GLA_V + GQA_Q, :], preferred_element_type=F32)
    y += jnp.dot(od_ref[...], w_ref[GLA_V + GQA_Q:, :], preferred_element_type=F32)
    o_ref[...] = s_ref[...] + mod_ref[2:3, :] * y


def _output_projection(s, mod, o_gla, gate, o_gqa, o_diff, gla_norm_g, w_out_b):
    n, d = s.shape
    nt = n // ROW_TILE
    row = lambda w: pl.BlockSpec((ROW_TILE, w), lambda i: (i, 0))
    return pl.pallas_call(
        _outproj_kernel,
        grid=(nt,),
        in_specs=[row(d),
                  pl.BlockSpec((None, 8, d), lambda i: (_row_group(i), 0, 0)),
                  pl.BlockSpec((2, ROW_TILE, GLA_V), lambda i: (0, i, 0)),
                  row(GLA_V), row(GQA_Q), row(DIFF_V),
                  _resident((1, GLA_DV)),
                  _resident((MIX_WIDTH, d))],
        out_specs=row(d),
        out_shape=jax.ShapeDtypeStruct((n, d), F32),
        compiler_params=_params("parallel"),
        name="output_projection",
    )(s, mod, o_gla, gate, o_gqa, o_diff, gla_norm_g.reshape(1, -1), w_out_b)


def _router_kernel(n_experts, s_ref, mod_ref, g_ref, rw_ref, rb_ref,
                   h_ref, eid_ref, rank_ref, w_ref, count_ref, carry_ref):
    @pl.when(pl.program_id(0) == 0)
    def _():
        carry_ref[...] = jnp.zeros_like(carry_ref)

    h = _rms(s_ref[...], g_ref[...]) * (1.0 + mod_ref[4:5, :]) + mod_ref[3:4, :]
    h_ref[...] = h
    logits = jnp.dot(h.astype(BF16), rw_ref[...], preferred_element_type=F32)
    scores = 1.0 / (1.0 + jnp.exp(-logits))
    rows = scores.shape[0]
    lane = lax.broadcasted_iota(jnp.int32, scores.shape, 1)
    lane_f = lane.astype(F32)
    cand = jnp.where(lane < n_experts, scores + rb_ref[...], -jnp.inf)
    hits = []
    for _ in range(TOP_K):
        best = jnp.max(cand, axis=-1, keepdims=True)
        first = jnp.min(jnp.where(cand == best, lane_f, float(LANES)), axis=-1, keepdims=True)
        hit = lane_f == first
        hits.append(hit)
        cand = jnp.where(hit, -jnp.inf, cand)
    chosen = functools.reduce(jnp.logical_or, hits)
    total = jnp.sum(jnp.where(chosen, scores, 0.0), axis=-1, keepdims=True)
    ri = lax.broadcasted_iota(jnp.int32, (rows, rows), 0)
    ci = lax.broadcasted_iota(jnp.int32, (rows, rows), 1)
    chosen_b = jnp.where(chosen, 1.0, 0.0).astype(BF16)
    before = jnp.dot(jnp.where(ci < ri, 1.0, 0.0).astype(BF16), chosen_b, preferred_element_type=F32)
    rank_all = before + carry_ref[...]
    for k, hit in enumerate(hits):
        pick = lambda a: jnp.sum(jnp.where(hit, a, 0.0), axis=-1, keepdims=True)
        eid_ref[:, k:k + 1] = pick(lane_f).astype(jnp.int32)
        rank_ref[:, k:k + 1] = pick(rank_all).astype(jnp.int32)
        w_ref[:, k:k + 1] = pick(scores) / total * ROUTE_SCALE
    carry_ref[...] += jnp.sum(chosen_b.astype(F32), axis=0, keepdims=True)
    count_ref[...] = carry_ref[...]


def _router(s, mod, norm_g, router_w_b, router_b, n_experts):
    n, d = s.shape
    nt = n // ROW_TILE
    row = lambda w: pl.BlockSpec((ROW_TILE, w), lambda i: (i, 0))
    return pl.pallas_call(
        functools.partial(_router_kernel, n_experts),
        grid=(nt,),
        in_specs=[row(d),
                  pl.BlockSpec((None, 8, d), lambda i: (_row_group(i), 0, 0)),
                  _resident((1, d)),
                  _resident((d, LANES)),
                  _resident((1, LANES))],
        out_specs=[row(d), row(TOP_K), row(TOP_K), row(TOP_K), pl.BlockSpec((1, LANES), lambda i: (0, 0))],
        out_shape=[jax.ShapeDtypeStruct((n, d), F32), jax.ShapeDtypeStruct((n, TOP_K), jnp.int32),
                   jax.ShapeDtypeStruct((n, TOP_K), jnp.int32), jax.ShapeDtypeStruct((n, TOP_K), F32),
                   jax.ShapeDtypeStruct((1, LANES), F32)],
        scratch_shapes=[pltpu.VMEM((1, LANES), F32)],
        compiler_params=_params("arbitrary"),
        name="router",
    )(s, mod, norm_g.reshape(1, d), router_w_b, router_b)


def _routing_plan(eid, rank, counts_f, n_experts):
    counts = counts_f[0, :n_experts].astype(jnp.int32)
    padded = (counts + SUBLANES - 1) // SUBLANES * SUBLANES
    starts = jnp.concatenate([jnp.zeros((1,), jnp.int32), jnp.cumsum(padded)])
    experts = jnp.arange(n_experts, dtype=jnp.int32)
    dest = jnp.sum(jnp.where(eid[..., None] == experts, starts[:-1], 0), axis=-1) + rank
    gap = jnp.arange(SUBLANES, dtype=jnp.int32)[None, :]
    in_gap = gap < (padded - counts)[:, None]
    gap_row = (starts[:-1] + counts)[:, None] + gap
    n_slack_before = jnp.cumsum(jnp.logical_not(in_gap).reshape(-1).astype(jnp.int32)) - 1
    filler = jnp.where(in_gap.reshape(-1), gap_row.reshape(-1), starts[-1] + n_slack_before)
    return dest, starts, counts, filler.astype(jnp.int32)


def _wait_rows(ref_hbm, n_rows, sem):
    rows = ref_hbm.at[pl.ds(0, n_rows)]
    pltpu.make_async_copy(rows, rows, sem).wait()


def _dispatch_kernel(n_filler, filler_ref, dest_ref, h_ref, xs_hbm, sem):
    def row_to(j, d):
        return pltpu.make_async_copy(h_ref.at[pl.ds(j, 1)], xs_hbm.at[pl.ds(d, 1)], sem)

    def body(j, carry):
        for k in range(TOP_K):
            row_to(j, dest_ref[k, j]).start()
        return carry

    lax.fori_loop(0, ROW_TILE, body, 0)
    _wait_rows(xs_hbm, ROW_TILE * TOP_K, sem)

    @pl.when(pl.program_id(0) == 0)
    def _():
        def fill(f, carry):
            row_to(0, filler_ref[f]).start()
            return carry

        lax.fori_loop(0, n_filler, fill, 0)
        tail = xs_hbm.shape[0] - EXPERT_ROW_TILE
        tail_copy = pltpu.make_async_copy(h_ref.at[pl.ds(0, EXPERT_ROW_TILE)],
                                          xs_hbm.at[pl.ds(tail, EXPERT_ROW_TILE)], sem)
        tail_copy.start()
        _wait_rows(xs_hbm, n_filler, sem)
        tail_copy.wait()


def _dispatch(h, dest_tiles, filler):
    n, d = h.shape
    nt = n // ROW_TILE
    n_filler = filler.shape[0]
    grid_spec = pltpu.PrefetchScalarGridSpec(
        num_scalar_prefetch=1,
        grid=(nt,),
        in_specs=[pl.BlockSpec((None, TOP_K, ROW_TILE), lambda i, f: (i, 0, 0), memory_space=pltpu.SMEM),
                  pl.BlockSpec((ROW_TILE, d), lambda i, f: (i, 0))],
        out_specs=pl.BlockSpec(memory_space=pl.ANY),
        scratch_shapes=[pltpu.SemaphoreType.DMA(())])
    return pl.pallas_call(
        functools.partial(_dispatch_kernel, n_filler),
        grid_spec=grid_spec,
        out_shape=jax.ShapeDtypeStruct((n * TOP_K + n_filler + EXPERT_ROW_TILE, d), F32),
        compiler_params=_params("arbitrary"),
        name="moe_dispatch",
    )(filler, dest_tiles, h)


def _grouped_kernel(n_fill, starts_ref, counts_ref, wg_ref, wu_ref, wd_ref, xs_hbm, ys_hbm,
                    wgb_ref, wub_ref, wdb_ref, xbuf_ref, ybuf_ref, xsem, ysem):
    e = pl.program_id(0)
    tm = EXPERT_ROW_TILE
    base = pl.multiple_of(starts_ref[e], SUBLANES)
    n_windows = (counts_ref[e] + tm - 1) // tm

    def window(ref_hbm, c):
        return ref_hbm.at[pl.ds(pl.multiple_of(base + c * tm, SUBLANES), tm)]

    def x_copy(c, slot):
        return pltpu.make_async_copy(window(xs_hbm, c), xbuf_ref.at[slot], xsem.at[slot])

    def y_copy(c, slot):
        return pltpu.make_async_copy(ybuf_ref.at[slot], window(ys_hbm, c), ysem.at[slot])

    @pl.when(n_windows > 0)
    def _():
        x_copy(0, 0).start(priority=WINDOW_DMA_PRIORITY)
        wgb_ref[...] = wg_ref[...].astype(BF16)
        wub_ref[...] = wu_ref[...].astype(BF16)
        wdb_ref[...] = wd_ref[...].astype(BF16)

        def body(c, carry):
            slot = c % 2
            x_copy(c, slot).wait()

            @pl.when(c + 1 < n_windows)
            def _():
                x_copy(c + 1, 1 - slot).start(priority=WINDOW_DMA_PRIORITY)

            x = xbuf_ref[slot].astype(BF16)
            a = jnp.dot(x, wgb_ref[...], preferred_element_type=F32)
            b = jnp.dot(x, wub_ref[...], preferred_element_type=F32)
            y = jnp.dot((_silu(a) * b).astype(BF16), wdb_ref[...], preferred_element_type=F32)

            @pl.when(c > 0)
            def _():
                y_copy(c - 1, 1 - slot).wait()

            ybuf_ref[slot] = y
            y_copy(c, slot).start(priority=WINDOW_DMA_PRIORITY)
            return carry

        lax.fori_loop(0, n_windows, body, 0)
        y_copy(n_windows - 1, (n_windows - 1) % 2).wait()

    @pl.when(e == pl.num_programs(0) - 1)
    def _():
        ybuf_ref[0] = jnp.zeros(ybuf_ref.shape[1:], F32)
        end = starts_ref[e + 1]
        last_start = ys_hbm.shape[0] - tm
        for i in range(n_fill):
            start = pl.multiple_of(jnp.minimum(end + i * tm, last_start), SUBLANES)
            fill = pltpu.make_async_copy(ybuf_ref.at[0], ys_hbm.at[pl.ds(start, tm)], ysem.at[0])
            fill.start()
            fill.wait()


def _grouped_experts(xs, layer, w_gate, w_up, w_down, starts, counts):
    p, d = xs.shape
    n_experts, hidden = w_gate.shape[1], w_gate.shape[-1]
    tm = EXPERT_ROW_TILE
    n_fill = -(-(SUBLANES * n_experts + tm) // tm)
    weight = lambda shape: pl.BlockSpec((None, None) + shape, lambda e, st, ct: (layer, e, 0, 0))
    grid_spec = pltpu.PrefetchScalarGridSpec(
        num_scalar_prefetch=2,
        grid=(n_experts,),
        in_specs=[weight((d, hidden)), weight((d, hidden)), weight((hidden, d)),
                  pl.BlockSpec(memory_space=pl.ANY)],
        out_specs=pl.BlockSpec(memory_space=pl.ANY),
        scratch_shapes=[pltpu.VMEM((d, hidden), BF16), pltpu.VMEM((d, hidden), BF16),
                        pltpu.VMEM((hidden, d), BF16),
                        pltpu.VMEM((2, tm, d), F32), pltpu.VMEM((2, tm, d), F32),
                        pltpu.SemaphoreType.DMA((2,)), pltpu.SemaphoreType.DMA((2,))])
    return pl.pallas_call(
        functools.partial(_grouped_kernel, n_fill),
        grid_spec=grid_spec,
        out_shape=jax.ShapeDtypeStruct((p, d), F32),
        compiler_params=_params("arbitrary"),
        name="moe_grouped_experts",
    )(starts, counts, w_gate, w_up, w_down, xs)


def _combine_kernel(n_ctx_rows, first_tile, final_norm, dest_ref, dest_next_ref, w_ref, h_ref, s_ref,
                    mod_ref, sg_ref, su_ref, sd_ref, fg_ref, ys_hbm, o_ref, ybuf_ref, sems):
    i = pl.program_id(0)
    n_steps = pl.num_programs(0)
    rows = h_ref.shape[0]
    slot = i % 2

    def gather(dref, to_slot):
        def body(j, carry):
            for k in range(TOP_K):
                pltpu.make_async_copy(ys_hbm.at[pl.ds(dref[k, j], 1)],
                                      ybuf_ref.at[to_slot, k, pl.ds(j, 1)], sems.at[to_slot]).start()
            return carry
        lax.fori_loop(0, rows, body, 0)

    @pl.when(i == 0)
    def _():
        gather(dest_ref, 0)

    @pl.when(i + 1 < n_steps)
    def _():
        gather(dest_next_ref, 1 - slot)

    hb = h_ref[...].astype(BF16)
    a = jnp.dot(hb, sg_ref[...], preferred_element_type=F32)
    b = jnp.dot(hb, su_ref[...], preferred_element_type=F32)
    y = jnp.dot((_silu(a) * b).astype(BF16), sd_ref[...], preferred_element_type=F32)

    pltpu.make_async_copy(ybuf_ref.at[slot], ybuf_ref.at[slot], sems.at[slot]).wait()
    w = w_ref[...]
    for k in range(TOP_K):
        y += w[:, k:k + 1] * ybuf_ref[slot, k]
    r = (first_tile + i) * rows + lax.broadcasted_iota(jnp.int32, (rows, 1), 0)
    gate2 = jnp.where(r < n_ctx_rows, mod_ref[1, 5:6, :], mod_ref[0, 5:6, :])
    out = s_ref[...] + gate2 * y
    o_ref[...] = _rms(out, fg_ref[...]) if final_norm else out


def _combine(ys, dest_tiles, w, h, s, mod, sh_gate_b, sh_up_b, sh_down_b, final_g, n_ctx_rows, last_layer):
    n, d = s.shape
    tc = COMBINE_ROW_TILE
    first_tile = n_ctx_rows // tc if last_layer else 0
    steps = n // tc - first_tile
    hidden = sh_gate_b.shape[-1]
    row = lambda width: pl.BlockSpec((tc, width), lambda i: (i + first_tile, 0))
    idx = lambda shift: pl.BlockSpec(
        (None, TOP_K, tc), lambda i: (jnp.minimum(i + first_tile + shift, n // tc - 1), 0, 0),
        memory_space=pltpu.SMEM)
    return pl.pallas_call(
        functools.partial(_combine_kernel, n_ctx_rows, first_tile, last_layer),
        grid=(steps,),
        in_specs=[idx(0), idx(1), row(TOP_K), row(d), row(d),
                  pl.BlockSpec((2, 8, d), lambda i: (0, 0, 0)),
                  _resident((d, hidden)), _resident((d, hidden)), _resident((hidden, d)),
                  _resident((1, d)),
                  pl.BlockSpec(memory_space=pl.ANY)],
        out_specs=pl.BlockSpec((tc, d), lambda i: (i, 0)),
        out_shape=jax.ShapeDtypeStruct((steps * tc, d), F32),
        scratch_shapes=[pltpu.VMEM((2, TOP_K, tc, d), F32), pltpu.SemaphoreType.DMA((2,))],
        compiler_params=_params("arbitrary"),
        name="moe_combine",
    )(dest_tiles, dest_tiles, w, h, s, mod, sh_gate_b, sh_up_b, sh_down_b, final_g.reshape(1, d), ys)


def _tile_major(dest, tile):
    n, k = dest.shape
    return dest.reshape(n // tile, tile, k).transpose(0, 2, 1)


def _rope_tables(n_ctx, n_tokens, dim):
    half = dim // 2
    inv_freq = ROPE_THETA ** (-jnp.arange(0, half, 2, dtype=F32) / half)
    t = jnp.arange(n_tokens, dtype=jnp.int32)
    ang_r = (t // GRID_W).astype(F32)[:, None] * inv_freq
    ang_c = (t % GRID_W).astype(F32)[:, None] * inv_freq
    cos = jnp.concatenate([jnp.cos(ang_r)] * 2 + [jnp.cos(ang_c)] * 2, axis=-1)
    sin = jnp.concatenate([-jnp.sin(ang_r), jnp.sin(ang_r), -jnp.sin(ang_c), jnp.sin(ang_c)], axis=-1)
    reps = LANES // dim
    cos = jnp.tile(cos, (1, reps))
    sin = jnp.tile(sin, (1, reps))
    cos = jnp.concatenate([jnp.ones((n_ctx, LANES), F32), cos], axis=0)
    sin = jnp.concatenate([jnp.zeros((n_ctx, LANES), F32), sin], axis=0)
    return cos, sin


def kernel(x, c, ctx, c_ctx, norm1_g, norm2_g, w_mod, b_mod, w_in, gla_wa_f, gla_ba_f, gla_wa_b,
           gla_ba_b, gla_norm_g, q_norm_g, k_norm_g, diff_lq1, diff_lk1, diff_lq2, diff_lk2,
           diff_norm_g, w_out, router_w, router_b, exp_w_gate, exp_w_up, exp_w_down,
           sh_w_gate, sh_w_up, sh_w_down, final_g):
    batch, n_tokens, d = x.shape
    n_ctx = ctx.shape[1]
    depth = w_mod.shape[0]
    n_experts = router_w.shape[-1]
    n_rows = n_ctx + n_tokens
    assert batch == 1 and n_ctx == ROW_TILE and n_tokens % ROW_TILE == 0
    assert TOP_K <= n_experts <= LANES and EXPERT_ROW_TILE <= ROW_TILE

    s = jnp.concatenate([ctx[0], x[0]], axis=0)
    mod_all = _modulation(c, c_ctx, w_mod, b_mod)
    tables = _rope_tables(n_ctx, n_tokens, HEAD_DIM) + _rope_tables(n_ctx, n_tokens, DIFF_DQK)

    for l in range(depth):
        first_q_tile = 0
        lam_init = 0.8 - 0.6 * math.exp(-0.3 * l)
        mod = mod_all[l]

        w_in_b = jnp.concatenate(
            [w_in[l][:, :ORIG_LR], w_in[l][:, ORIG_LR + 2 * GLA_GATE_RANK:],
             w_in[l][:, ORIG_LR:ORIG_LR + 2 * GLA_GATE_RANK],
             jnp.zeros((d, LANES - 2 * GLA_GATE_RANK), F32)], axis=1).astype(BF16)
        wa = jnp.zeros((LANES, 2 * GLA_QK), F32)
        wa = wa.at[:GLA_GATE_RANK, :GLA_QK].set(gla_wa_f[l])
        wa = wa.at[GLA_GATE_RANK:2 * GLA_GATE_RANK, GLA_QK:].set(gla_wa_b[l]).astype(BF16)
        ba = jnp.concatenate([gla_ba_f[l], gla_ba_b[l]]).reshape(1, -1)
        lamv = jnp.zeros((8, LANES), F32)
        for r, vec in enumerate((diff_lq1[l], diff_lk1[l], diff_lq2[l], diff_lk2[l])):
            lamv = lamv.at[r, :DIFF_DQK].set(vec)
        rw = jnp.pad(router_w[l], ((0, 0), (0, LANES - n_experts))).astype(BF16)
        rb = jnp.pad(router_b[l], (0, LANES - n_experts)).reshape(1, LANES)

        (glaq, glak, glav, gate, la, q_t, k, v_t, dq_t, dk, dv_t) = _input_projection(
            s, mod, norm1_g[l], w_in_b, wa, ba, q_norm_g[l], k_norm_g[l], tables)
        o_gla = _gla(glaq, glak, glav, la)
        o_gqa = _gqa_attention(q_t, k, v_t, first_q_tile)
        o_diff = _diff_attention(dq_t, dk, dv_t, lamv, diff_norm_g[l], lam_init, first_q_tile)
        s = _output_projection(s, mod, o_gla, gate, o_gqa, o_diff, gla_norm_g[l], w_out[l].astype(BF16))
        h2, eid, rank, w_route, counts_f = _router(s, mod, norm2_g[l], rw, rb, n_experts)
        dest, starts, counts, filler = _routing_plan(eid, rank, counts_f, n_experts)
        xs = _dispatch(h2, _tile_major(dest, ROW_TILE), filler)
        ys = _grouped_experts(xs, l, exp_w_gate, exp_w_up, exp_w_down, starts, counts)
        s = _combine(ys, _tile_major(dest, COMBINE_ROW_TILE), w_route, h2, s, mod,
                     sh_w_gate[l].astype(BF16), sh_w_up[l].astype(BF16), sh_w_down[l].astype(BF16),
                     final_g, n_ctx, last_layer=l == depth - 1)

    return s.reshape(batch, n_tokens, d)
```

```python
import functools
import math

import jax
import jax.numpy as jnp
from jax import lax
from jax.experimental import pallas as pl
from jax.experimental.pallas import tpu as pltpu

F32 = jnp.float32
BF16 = jnp.bfloat16

GRID_W = 64
HEAD_DIM = 128
GLA_HEADS = 4
GLA_DK = 64
GLA_DV = 128
GLA_GATE_RANK = 16
GLA_TAU = 16.0
GLA_CHUNK = 64
GQA_HEADS = 8
GQA_KV_HEADS = 2
DIFF_HEADS = 4
DIFF_DQK = 64
DIFF_DV = 128
TOP_K = 8
ROUTE_SCALE = 2.5
ROPE_THETA = 10000.0
NORM_EPS = 1e-6
LOG2_E = math.log2(math.e)

LANES = 128
ROW_TILE = 256
EXPERT_ROW_TILE = 256
COMBINE_ROW_TILE = 128
WEIGHT_DMA_PRIORITY = 1
VMEM_LIMIT = 56 * 1024 * 1024

GLA_QK = GLA_HEADS * GLA_DK
GLA_V = GLA_HEADS * GLA_DV
GQA_Q = GQA_HEADS * HEAD_DIM
GQA_KV = GQA_KV_HEADS * HEAD_DIM
DIFF_QK = DIFF_HEADS * 2 * DIFF_DQK
DIFF_V = DIFF_HEADS * DIFF_DV
MIX_WIDTH = GLA_V + GQA_Q + DIFF_V

C_GLAQ = 0
C_GLAK = C_GLAQ + GLA_QK
C_GLAV = C_GLAK + GLA_QK
C_GATE = C_GLAV + GLA_V
C_GQAQ = C_GATE + GLA_V
C_GQAK = C_GQAQ + GQA_Q
C_GQAV = C_GQAK + GQA_KV
C_DQ = C_GQAV + GQA_KV
C_DK = C_DQ + DIFF_QK
C_DV = C_DK + DIFF_QK
C_LR = C_DV + DIFF_V
IN_COLS = C_LR + LANES
ORIG_LR = 2 * GLA_QK + 2 * GLA_V


def _params(*sem):
    return pltpu.CompilerParams(dimension_semantics=sem, vmem_limit_bytes=VMEM_LIMIT)


def _resident(shape):
    nd = len(shape)
    return pl.BlockSpec(shape, lambda *_: (0,) * nd, pipeline_mode=pl.Buffered(1))


def _silu(a):
    return a / (1.0 + jnp.exp(-a))


def _rms(x, g):
    return x * lax.rsqrt(jnp.mean(x * x, axis=-1, keepdims=True) + NORM_EPS) * g


def _row_group(i):
    return jnp.where(i == 0, 1, 0)


def _mod_kernel(a_ref, w_ref, b_ref, o_ref):
    a = _silu(a_ref[...])
    o_ref[...] = jnp.dot(a.astype(BF16), w_ref[...].astype(BF16),
                         preferred_element_type=F32) + b_ref[...]


def _modulation(c, c_ctx, w_mod, b_mod):
    depth, d, six_d = w_mod.shape
    a = jnp.zeros((8, d), F32).at[0].set(c[0]).at[1].set(c_ctx)
    tn = d // 2
    out = pl.pallas_call(
        _mod_kernel,
        grid=(depth, six_d // tn),
        in_specs=[pl.BlockSpec((8, d), lambda l, j: (0, 0)),
                  pl.BlockSpec((None, d, tn), lambda l, j: (l, 0, j)),
                  pl.BlockSpec((None, 1, tn), lambda l, j: (l, 0, j))],
        out_specs=pl.BlockSpec((None, 8, tn), lambda l, j: (l, 0, j)),
        out_shape=jax.ShapeDtypeStruct((depth, 8, six_d), F32),
        compiler_params=_params("parallel", "parallel"),
        name="modulation",
    )(a, w_mod, b_mod.reshape(depth, 1, six_d))
    m = out[:, :2].reshape(depth, 2, 6, d)
    return jnp.pad(m, ((0, 0), (0, 0), (0, 2), (0, 0)))


def _rope(xh, cos, sin, first, shift_first, shift_second):
    partner = jnp.where(first, pltpu.roll(xh, shift_first, 1), pltpu.roll(xh, shift_second, 1))
    return xh * cos + partner * sin


def _inproj_kernel(x_ref, mod_ref, g_ref, w_ref, wa_ref, ba_ref, qg_ref, kg_ref,
                   cg_ref, sg_ref, cd_ref, sd_ref,
                   glaq_ref, glak_ref, glav_ref, gate_ref, la_ref,
                   qt_ref, k_ref, vt_ref, dqt_ref, dk_ref, dvt_ref):
    x = x_ref[...]
    h = _rms(x, g_ref[...]) * (1.0 + mod_ref[1:2, :]) + mod_ref[0:1, :]
    hb = h.astype(BF16)

    def proj(start, width):
        return jnp.dot(hb, w_ref[:, start:start + width], preferred_element_type=F32)

    glaq_ref[...] = proj(C_GLAQ, GLA_QK) * (GLA_DK ** -0.5)
    glak_ref[...] = proj(C_GLAK, GLA_QK)
    glav_ref[...] = proj(C_GLAV, GLA_V)
    gate_ref[...] = proj(C_GATE, GLA_V)
    zv = proj(C_GQAV, GQA_KV)
    for hd in range(GQA_KV_HEADS):
        sl = slice(hd * HEAD_DIM, (hd + 1) * HEAD_DIM)
        vt_ref[sl, :] = zv[:, sl].T.astype(BF16)
    zdv = proj(C_DV, DIFF_V)
    for hd in range(DIFF_HEADS):
        sl = slice(hd * DIFF_DV, (hd + 1) * DIFF_DV)
        dvt_ref[sl, :] = zdv[:, sl].T.astype(BF16)

    z_lr = proj(C_LR, LANES).astype(BF16)
    pre = jnp.dot(z_lr, wa_ref[...], preferred_element_type=F32) + ba_ref[...]
    log_sig = -(jnp.maximum(-pre, 0.0) + jnp.log1p(jnp.exp(-jnp.abs(pre))))
    la = log_sig * (1.0 / GLA_TAU)
    la_ref[0] = la[:, :GLA_QK]
    la_ref[1] = la[:, GLA_QK:]

    rows = x.shape[0]
    lane = lax.broadcasted_iota(jnp.int32, (rows, LANES), 1)
    first_g = (lane % 64) < 32
    first_d = (lane % 32) < 16
    cg, sg, cd, sd = cg_ref[...], sg_ref[...], cd_ref[...], sd_ref[...]
    scale_g = HEAD_DIM ** -0.5 * LOG2_E
    scale_d = DIFF_DQK ** -0.5 * LOG2_E

    zq = proj(C_GQAQ, GQA_Q)
    for hd in range(GQA_HEADS):
        sl = slice(hd * HEAD_DIM, (hd + 1) * HEAD_DIM)
        qh = _rope(_rms(zq[:, sl], qg_ref[...]), cg, sg, first_g, 96, 32) * scale_g
        qt_ref[sl, :] = qh.T.astype(BF16)
    zk = proj(C_GQAK, GQA_KV)
    for hd in range(GQA_KV_HEADS):
        kh = _rms(zk[:, hd * HEAD_DIM:(hd + 1) * HEAD_DIM], kg_ref[...])
        k_ref[:, hd * HEAD_DIM:(hd + 1) * HEAD_DIM] = _rope(kh, cg, sg, first_g, 96, 32).astype(BF16)
    zdq = proj(C_DQ, DIFF_QK)
    zdk = proj(C_DK, DIFF_QK)
    for hd in range(DIFF_HEADS):
        sl = slice(hd * LANES, (hd + 1) * LANES)
        dqt_ref[sl, :] = (_rope(zdq[:, sl], cd, sd, first_d, 112, 16) * scale_d).T.astype(BF16)
        dk_ref[:, sl] = _rope(zdk[:, sl], cd, sd, first_d, 112, 16).astype(BF16)


def _input_projection(s, mod, norm_g, w_in_b, wa, ba, q_norm_g, k_norm_g, tables):
    n, d = s.shape
    nt = n // ROW_TILE
    row = lambda w: pl.BlockSpec((ROW_TILE, w), lambda i: (i, 0))
    col = lambda w: pl.BlockSpec((w, ROW_TILE), lambda i: (0, i))
    tile_t = lambda w: pl.BlockSpec((None, w, ROW_TILE), lambda i: (i, 0, 0))
    f32o = lambda w: jax.ShapeDtypeStruct((n, w), F32)
    bfo = lambda w: jax.ShapeDtypeStruct((n, w), BF16)
    return pl.pallas_call(
        _inproj_kernel,
        grid=(nt,),
        in_specs=[row(d),
                  pl.BlockSpec((None, 8, d), lambda i: (_row_group(i), 0, 0)),
                  _resident((1, d)),
                  _resident((d, IN_COLS)),
                  _resident((LANES, 2 * GLA_QK)),
                  _resident((1, 2 * GLA_QK)),
                  _resident((1, HEAD_DIM)),
                  _resident((1, HEAD_DIM)),
                  row(LANES), row(LANES), row(LANES), row(LANES)],
        out_specs=[row(GLA_QK), row(GLA_QK), row(GLA_V), row(GLA_V),
                   pl.BlockSpec((2, ROW_TILE, GLA_QK), lambda i: (0, i, 0)),
                   col(GQA_Q), row(GQA_KV), tile_t(GQA_KV), col(DIFF_QK), row(DIFF_QK), tile_t(DIFF_V)],
        out_shape=[f32o(GLA_QK), f32o(GLA_QK), f32o(GLA_V), f32o(GLA_V),
                   jax.ShapeDtypeStruct((2, n, GLA_QK), F32),
                   jax.ShapeDtypeStruct((GQA_Q, n), BF16), bfo(GQA_KV),
                   jax.ShapeDtypeStruct((nt, GQA_KV, ROW_TILE), BF16),
                   jax.ShapeDtypeStruct((DIFF_QK, n), BF16), bfo(DIFF_QK),
                   jax.ShapeDtypeStruct((nt, DIFF_V, ROW_TILE), BF16)],
        compiler_params=_params("parallel"),
        name="input_projection",
    )(s, mod, norm_g.reshape(1, d), w_in_b, wa, ba, q_norm_g.reshape(1, -1), k_norm_g.reshape(1, -1),
      *tables)


def _gla_block(reverse, q_ref, k_ref, v_ref, la_ref, o_ref, st_ref):
    rows = q_ref.shape[0]
    n_chunks = rows // GLA_CHUNK
    ri = lax.broadcasted_iota(jnp.int32, (rows, rows), 0)
    ci = lax.broadcasted_iota(jnp.int32, (rows, rows), 1)
    same_chunk = (ri // GLA_CHUNK) == (ci // GLA_CHUNK)
    allowed = same_chunk & ((ci >= ri) if reverse else (ci <= ri))
    la = la_ref[...]
    cum = jnp.dot(jnp.where(allowed, 1.0, 0.0).astype(F32), la,
                  preferred_element_type=F32, precision=lax.Precision.HIGHEST)
    tot = jnp.dot(jnp.where(same_chunk, 1.0, 0.0).astype(F32), la,
                  preferred_element_type=F32, precision=lax.Precision.HIGHEST)
    k = k_ref[...]
    q_dec = q_ref[...] * jnp.exp(cum)
    k_inv = (k * jnp.exp(-cum)).astype(BF16)
    k_end = k * jnp.exp(tot - cum)
    v = v_ref[...]
    vb = v.astype(BF16)
    v_t = v.T.astype(BF16)
    lane_head = lax.broadcasted_iota(jnp.int32, (rows, GLA_QK), 1) // GLA_DK
    row_chunk = lax.broadcasted_iota(jnp.int32, (rows, GLA_QK), 0) // GLA_CHUNK
    nt_dims = (((1,), (1,)), ((), ()))

    for hd in range(GLA_HEADS):
        qh = jnp.where(lane_head == hd, q_dec, 0.0).astype(BF16)
        a = lax.dot_general(qh, k_inv, nt_dims, preferred_element_type=F32)
        a = jnp.where(allowed, a, 0.0).astype(BF16)
        o_ref[:, hd * GLA_DV:(hd + 1) * GLA_DV] = jnp.dot(
            a, vb[:, hd * GLA_DV:(hd + 1) * GLA_DV], preferred_element_type=F32)

    chunk_order = range(n_chunks - 1, -1, -1) if reverse else range(n_chunks)
    chunk_lane_head = lax.broadcasted_iota(jnp.int32, (GLA_CHUNK, GLA_QK), 1) // GLA_DK
    for c in chunk_order:
        sl = slice(c * GLA_CHUNK, (c + 1) * GLA_CHUNK)
        state = st_ref[...]
        q4 = jnp.concatenate(
            [jnp.where(chunk_lane_head == hd, q_dec[sl], 0.0) for hd in range(GLA_HEADS)],
            axis=0).astype(BF16)
        r = lax.dot_general(q4, state.astype(BF16), nt_dims, preferred_element_type=F32)
        for hd in range(GLA_HEADS):
            o_ref[sl, hd * GLA_DV:(hd + 1) * GLA_DV] += r[hd * GLA_CHUNK:(hd + 1) * GLA_CHUNK,
                                                          hd * GLA_DV:(hd + 1) * GLA_DV]
        k_end_c = jnp.where(row_chunk == c, k_end, 0.0).astype(BF16)
        u_t = jnp.dot(v_t, k_end_c, preferred_element_type=F32)
        st_ref[...] = state * jnp.exp(tot[c * GLA_CHUNK:c * GLA_CHUNK + 1]) + u_t


def _gla_kernel(q_ref, k_ref, v_ref, la_ref, o_ref, st_ref):
    @pl.when(pl.program_id(1) == 0)
    def _():
        st_ref[...] = jnp.zeros_like(st_ref)

    @pl.when(pl.program_id(0) == 0)
    def _():
        _gla_block(False, q_ref, k_ref, v_ref, la_ref, o_ref, st_ref)

    @pl.when(pl.program_id(0) == 1)
    def _():
        _gla_block(True, q_ref, k_ref, v_ref, la_ref, o_ref, st_ref)


def _gla(glaq, glak, glav, la):
    n = glaq.shape[0]
    nt = n // ROW_TILE

    def blk(dr, j):
        return jnp.where(dr == 0, j, jnp.where(j == 0, 0, nt - j))

    row = lambda w: pl.BlockSpec((ROW_TILE, w), lambda dr, j: (blk(dr, j), 0))
    return pl.pallas_call(
        _gla_kernel,
        grid=(2, nt),
        in_specs=[row(GLA_QK), row(GLA_QK), row(GLA_V),
                  pl.BlockSpec((None, ROW_TILE, GLA_QK), lambda dr, j: (dr, blk(dr, j), 0))],
        out_specs=pl.BlockSpec((None, ROW_TILE, GLA_V), lambda dr, j: (dr, blk(dr, j), 0)),
        out_shape=jax.ShapeDtypeStruct((2, n, GLA_V), F32),
        scratch_shapes=[pltpu.VMEM((GLA_V, GLA_QK), F32)],
        compiler_params=_params("arbitrary", "arbitrary"),
        name="gla_scan",
    )(glaq, glak, glav, la)


class _FlashMaps:
    def __init__(self, q_maps, k_of_map, load_k, load_vt):
        self.q_maps, self.k_of_map, self.load_k, self.load_vt = q_maps, k_of_map, load_k, load_vt


def _flash_steps(maps, first_tile, tiles_per_step, n_steps, s_ref, m_ref, l_ref, acc_ref):
    n_maps = len(maps.q_maps)
    rows = tiles_per_step * ROW_TILE
    for u in range(n_steps):
        first_row = pl.multiple_of((first_tile + u * tiles_per_step) * ROW_TILE, ROW_TILE)
        keys = {src: maps.load_k(src, first_row, rows) for src in sorted(set(maps.k_of_map))}
        for j in range(n_maps):
            s_ref[u * n_maps + j, :rows] = jnp.dot(keys[maps.k_of_map[j]], maps.q_maps[j],
                                                   preferred_element_type=F32)
    for u in range(n_steps):
        tile = first_tile + u * tiles_per_step
        for j in range(n_maps):
            s_t = s_ref[u * n_maps + j, :rows]
            m_prev = m_ref[j]
            m_new = jnp.maximum(m_prev, jnp.max(s_t, axis=0, keepdims=True))
            alpha = jnp.exp2(m_prev - m_new)
            p = jnp.exp2(s_t - m_new)
            l_ref[j] = alpha * l_ref[j] + jnp.sum(p, axis=0, keepdims=True)
            pb = p.astype(BF16)
            pv = jnp.dot(maps.load_vt(maps.k_of_map[j], tile), pb[:ROW_TILE], preferred_element_type=F32)
            for r in range(1, tiles_per_step):
                pv += jnp.dot(maps.load_vt(maps.k_of_map[j], tile + r), pb[r * ROW_TILE:(r + 1) * ROW_TILE],
                              preferred_element_type=F32)
            acc_ref[j] = alpha * acc_ref[j] + pv
            m_ref[j] = m_new


def _flash_attend(maps, first_q_tile, n_key_tiles, tiles_per_step, n_steps, s_ref, m_ref, l_ref, acc_ref):
    m_ref[...] = jnp.full(m_ref.shape, -jnp.inf, F32)
    l_ref[...] = jnp.zeros(l_ref.shape, F32)
    acc_ref[...] = jnp.zeros(acc_ref.shape, F32)
    _flash_steps(maps, 0, 1, 1, s_ref, m_ref, l_ref, acc_ref)
    per_iter = tiles_per_step * n_steps
    qi = pl.program_id(1) + first_q_tile
    n_iter = jnp.where(qi == 0, 0, (n_key_tiles - 1) // per_iter)

    def body(it, carry):
        _flash_steps(maps, 1 + it * per_iter, tiles_per_step, n_steps, s_ref, m_ref, l_ref, acc_ref)
        return carry

    lax.fori_loop(0, n_iter, body, 0)


def _flash_plan(n_key_tiles):
    latent = n_key_tiles - 1
    tiles_per_step = 2 if latent % 2 == 0 else 1
    n_steps = 2 if latent % (2 * tiles_per_step) == 0 else 1
    return tiles_per_step, n_steps


def _flash_scratch(n_maps, dv, plan):
    tiles_per_step, n_steps = plan
    return [pltpu.VMEM((n_steps * n_maps, tiles_per_step * ROW_TILE, ROW_TILE), F32),
            pltpu.VMEM((n_maps, 1, ROW_TILE), F32), pltpu.VMEM((n_maps, 1, ROW_TILE), F32),
            pltpu.VMEM((n_maps, dv, ROW_TILE), F32)]


GQA_GROUP = GQA_HEADS // GQA_KV_HEADS


def _gqa_kernel(first_q_tile, plan, qt_ref, k_ref, vt_ref, o_ref, s_ref, m_ref, l_ref, acc_ref):
    maps = _FlashMaps(
        q_maps=[qt_ref[hd * HEAD_DIM:(hd + 1) * HEAD_DIM, :] for hd in range(GQA_GROUP)],
        k_of_map=[0] * GQA_GROUP,
        load_k=lambda src, first_row, rows: k_ref[pl.ds(first_row, rows), :],
        load_vt=lambda src, tile: vt_ref[tile])
    _flash_attend(maps, first_q_tile, vt_ref.shape[0], *plan, s_ref, m_ref, l_ref, acc_ref)
    for hd in range(GQA_GROUP):
        o_t = acc_ref[hd] / l_ref[hd]
        o_ref[:, hd * HEAD_DIM:(hd + 1) * HEAD_DIM] = o_t.T.astype(o_ref.dtype)


def _gqa_attention(q_t, k, v_t, first_q_tile):
    n = k.shape[0]
    nt = n // ROW_TILE
    nq = nt - first_q_tile
    gw = GQA_GROUP * HEAD_DIM
    plan = _flash_plan(nt)
    return pl.pallas_call(
        functools.partial(_gqa_kernel, first_q_tile, plan),
        grid=(GQA_KV_HEADS, nq),
        in_specs=[pl.BlockSpec((gw, ROW_TILE), lambda g, i: (g, i + first_q_tile)),
                  pl.BlockSpec((n, HEAD_DIM), lambda g, i: (0, g)),
                  pl.BlockSpec((nt, HEAD_DIM, ROW_TILE), lambda g, i: (0, g, 0))],
        out_specs=pl.BlockSpec((ROW_TILE, gw), lambda g, i: (i + first_q_tile, g)),
        out_shape=jax.ShapeDtypeStruct((n, GQA_Q), BF16),
        scratch_shapes=_flash_scratch(GQA_GROUP, HEAD_DIM, plan),
        compiler_params=_params("parallel", "arbitrary"),
        name="gqa_attention",
    )(q_t, k, v_t)


DIFF_PAIR = 2


def _diff_kernel(first_q_tile, plan, lam_init, qt_ref, k_ref, vt_ref, lamv_ref, g_ref, o_ref,
                 s_ref, m_ref, l_ref, acc_ref):
    row = lax.broadcasted_iota(jnp.int32, (LANES, ROW_TILE), 0)
    q_maps = []
    for hd in range(DIFF_PAIR):
        q_t = qt_ref[hd * LANES:(hd + 1) * LANES, :]
        zero = jnp.zeros_like(q_t)
        q_maps += [jnp.where(row < DIFF_DQK, q_t, zero), jnp.where(row >= DIFF_DQK, q_t, zero)]
    maps = _FlashMaps(
        q_maps=q_maps,
        k_of_map=[hd for hd in range(DIFF_PAIR) for _ in range(2)],
        load_k=lambda src, first_row, rows: k_ref[pl.ds(first_row, rows), src * LANES:(src + 1) * LANES],
        load_vt=lambda src, tile: vt_ref[tile, src * DIFF_DV:(src + 1) * DIFF_DV, :])
    _flash_attend(maps, first_q_tile, vt_ref.shape[0], *plan, s_ref, m_ref, l_ref, acc_ref)
    lv = lamv_ref[...]
    lam = (jnp.exp(jnp.sum(lv[0:1] * lv[1:2], axis=-1, keepdims=True))
           - jnp.exp(jnp.sum(lv[2:3] * lv[3:4], axis=-1, keepdims=True)) + lam_init)
    for hd in range(DIFF_PAIR):
        o_t = acc_ref[2 * hd] / l_ref[2 * hd] - lam * (acc_ref[2 * hd + 1] / l_ref[2 * hd + 1])
        o = _rms(o_t.T, g_ref[...]) * (1.0 - lam_init)
        o_ref[:, hd * DIFF_DV:(hd + 1) * DIFF_DV] = o.astype(o_ref.dtype)


def _diff_attention(q_t, k, v_t, lamv, norm_g, lam_init, first_q_tile):
    n = k.shape[0]
    nt = n // ROW_TILE
    nq = nt - first_q_tile
    pw = DIFF_PAIR * LANES
    plan = _flash_plan(nt)
    return pl.pallas_call(
        functools.partial(_diff_kernel, first_q_tile, plan, lam_init),
        grid=(DIFF_HEADS // DIFF_PAIR, nq),
        in_specs=[pl.BlockSpec((pw, ROW_TILE), lambda h, i: (h, i + first_q_tile)),
                  pl.BlockSpec((n, pw), lambda h, i: (0, h)),
                  pl.BlockSpec((nt, DIFF_PAIR * DIFF_DV, ROW_TILE), lambda h, i: (0, h, 0)),
                  pl.BlockSpec((8, LANES), lambda h, i: (0, 0)),
                  pl.BlockSpec((1, DIFF_DV), lambda h, i: (0, 0))],
        out_specs=pl.BlockSpec((ROW_TILE, DIFF_PAIR * DIFF_DV), lambda h, i: (i + first_q_tile, h)),
        out_shape=jax.ShapeDtypeStruct((n, DIFF_V), BF16),
        scratch_shapes=_flash_scratch(2 * DIFF_PAIR, DIFF_DV, plan),
        compiler_params=_params("parallel", "arbitrary"),
        name="diff_attention",
    )(q_t, k, v_t, lamv, norm_g.reshape(1, -1))


def _outproj_kernel(s_ref, mod_ref, og_ref, gate_ref, oq_ref, od_ref, gg_ref, w_ref, o_ref):
    og = og_ref[0] + og_ref[1]
    gate = gate_ref[...]
    y = jnp.zeros(s_ref.shape, F32)
    for hd in range(GLA_HEADS):
        sl = slice(hd * GLA_DV, (hd + 1) * GLA_DV)
        oh = _rms(og[:, sl], gg_ref[...]) * _silu(gate[:, sl])
        y += jnp.dot(oh.astype(BF16), w_ref[hd * GLA_DV:(hd + 1) * GLA_DV, :], preferred_element_type=F32)
    y += jnp.dot(oq_ref[...], w_ref[GLA_V:GLA_V + GQA_Q, :], preferred_element_type=F32)
    y += jnp.dot(od_ref[...], w_ref[GLA_V + GQA_Q:, :], preferred_element_type=F32)
    o_ref[...] = s_ref[...] + mod_ref[2:3, :] * y


def _output_projection(s, mod, o_gla, gate, o_gqa, o_diff, gla_norm_g, w_out_b):
    n, d = s.shape
    nt = n // ROW_TILE
    row = lambda w: pl.BlockSpec((ROW_TILE, w), lambda i: (i, 0))
    return pl.pallas_call(
        _outproj_kernel,
        grid=(nt,),
        in_specs=[row(d),
                  pl.BlockSpec((None, 8, d), lambda i: (_row_group(i), 0, 0)),
                  pl.BlockSpec((2, ROW_TILE, GLA_V), lambda i: (0, i, 0)),
                  row(GLA_V), row(GQA_Q), row(DIFF_V),
                  _resident((1, GLA_DV)),
                  _resident((MIX_WIDTH, d))],
        out_specs=row(d),
        out_shape=jax.ShapeDtypeStruct((n, d), F32),
        compiler_params=_params("parallel"),
        name="output_projection",
    )(s, mod, o_gla, gate, o_gqa, o_diff, gla_norm_g.reshape(1, -1), w_out_b)


def _router_kernel(n_experts, s_ref, mod_ref, g_ref, rw_ref, rb_ref,
                   h_ref, eid_ref, rank_ref, w_ref, count_ref, carry_ref):
    @pl.when(pl.program_id(0) == 0)
    def _():
        carry_ref[...] = jnp.zeros_like(carry_ref)

    h = _rms(s_ref[...], g_ref[...]) * (1.0 + mod_ref[4:5, :]) + mod_ref[3:4, :]
    h_ref[...] = h
    logits = jnp.dot(h.astype(BF16), rw_ref[...], preferred_element_type=F32)
    scores = 1.0 / (1.0 + jnp.exp(-logits))
    rows = scores.shape[0]
    lane = lax.broadcasted_iota(jnp.int32, scores.shape, 1)
    lane_f = lane.astype(F32)
    cand = jnp.where(lane < n_experts, scores + rb_ref[...], -jnp.inf)
    hits = []
    for _ in range(TOP_K):
        best = jnp.max(cand, axis=-1, keepdims=True)
        first = jnp.min(jnp.where(cand == best, lane_f, float(LANES)), axis=-1, keepdims=True)
        hit = lane_f == first
        hits.append(hit)
        cand = jnp.where(hit, -jnp.inf, cand)
    chosen = functools.reduce(jnp.logical_or, hits)
    total = jnp.sum(jnp.where(chosen, scores, 0.0), axis=-1, keepdims=True)
    ri = lax.broadcasted_iota(jnp.int32, (rows, rows), 0)
    ci = lax.broadcasted_iota(jnp.int32, (rows, rows), 1)
    chosen_b = jnp.where(chosen, 1.0, 0.0).astype(BF16)
    before = jnp.dot(jnp.where(ci < ri, 1.0, 0.0).astype(BF16), chosen_b, preferred_element_type=F32)
    rank_all = before + carry_ref[...]
    for k, hit in enumerate(hits):
        pick = lambda a: jnp.sum(jnp.where(hit, a, 0.0), axis=-1, keepdims=True)
        eid_ref[:, k:k + 1] = pick(lane_f).astype(jnp.int32)
        rank_ref[:, k:k + 1] = pick(rank_all).astype(jnp.int32)
        w_ref[:, k:k + 1] = pick(scores) / total * ROUTE_SCALE
    carry_ref[...] += jnp.sum(chosen_b.astype(F32), axis=0, keepdims=True)
    count_ref[...] = carry_ref[...]


def _router(s, mod, norm_g, router_w_b, router_b, n_experts):
    n, d = s.shape
    nt = n // ROW_TILE
    row = lambda w: pl.BlockSpec((ROW_TILE, w), lambda i: (i, 0))
    return pl.pallas_call(
        functools.partial(_router_kernel, n_experts),
        grid=(nt,),
        in_specs=[row(d),
                  pl.BlockSpec((None, 8, d), lambda i: (_row_group(i), 0, 0)),
                  _resident((1, d)),
                  _resident((d, LANES)),
                  _resident((1, LANES))],
        out_specs=[row(d), row(TOP_K), row(TOP_K), row(TOP_K), pl.BlockSpec((1, LANES), lambda i: (0, 0))],
        out_shape=[jax.ShapeDtypeStruct((n, d), F32), jax.ShapeDtypeStruct((n, TOP_K), jnp.int32),
                   jax.ShapeDtypeStruct((n, TOP_K), jnp.int32), jax.ShapeDtypeStruct((n, TOP_K), F32),
                   jax.ShapeDtypeStruct((1, LANES), F32)],
        scratch_shapes=[pltpu.VMEM((1, LANES), F32)],
        compiler_params=_params("arbitrary"),
        name="router",
    )(s, mod, norm_g.reshape(1, d), router_w_b, router_b)


def _routing_plan(eid, rank, counts_f, n_experts, n_tiles):
    counts = counts_f[0, :n_experts].astype(jnp.int32)
    starts = jnp.concatenate([jnp.zeros((1,), jnp.int32), jnp.cumsum(counts)])
    experts = jnp.arange(n_experts, dtype=jnp.int32)
    dest = jnp.sum(jnp.where(eid[..., None] == experts, starts[:-1], 0), axis=-1) + rank
    first_tile = starts[:-1] // EXPERT_ROW_TILE
    last_tile = (starts[1:] - 1) // EXPERT_ROW_TILE
    visits_per_expert = jnp.where(counts > 0, last_tile - first_tile + 1, 0)
    visit_end = jnp.cumsum(visits_per_expert)
    visit_start = visit_end - visits_per_expert
    n_visits = visit_end[-1]
    v = jnp.minimum(jnp.arange(n_tiles + n_experts - 1, dtype=jnp.int32), n_visits - 1)
    visit_expert = jnp.sum((v[:, None] >= visit_end[None, :]).astype(jnp.int32), axis=-1)
    own = visit_expert[:, None] == experts[None, :]
    visit_tile = jnp.sum(jnp.where(own, (first_tile - visit_start)[None, :], 0), axis=-1) + v
    visited = visits_per_expert > 0
    later = jnp.logical_and(experts[None, :] > experts[:, None], visited[None, :])
    next_visited = jnp.min(jnp.where(later, experts[None, :], n_experts), axis=-1)
    next_visited = jnp.where(next_visited == n_experts, -1, next_visited)
    visit_next = jnp.sum(jnp.where(own, next_visited[None, :], 0), axis=-1)
    visit_first = jnp.sum(jnp.where(own, visit_start[None, :], 0), axis=-1) == v
    order = jnp.cumsum(visited.astype(jnp.int32)) - 1
    visit_slot = jnp.sum(jnp.where(own, order[None, :], 0), axis=-1) % 2
    plan = (visit_tile, visit_expert, starts, n_visits.reshape(1),
            visit_first.astype(jnp.int32), visit_slot, visit_next)
    return dest, plan


def _dispatch_kernel(dest_ref, h_ref, xs_hbm, sem):
    def body(j, carry):
        for k in range(TOP_K):
            pltpu.make_async_copy(h_ref.at[pl.ds(j, 1)], xs_hbm.at[pl.ds(dest_ref[k, j], 1)], sem).start()
        return carry

    lax.fori_loop(0, ROW_TILE, body, 0)
    all_rows = xs_hbm.at[pl.ds(0, ROW_TILE * TOP_K)]
    pltpu.make_async_copy(all_rows, all_rows, sem).wait()


def _dispatch(h, dest_tiles):
    n, d = h.shape
    nt = n // ROW_TILE
    return pl.pallas_call(
        _dispatch_kernel,
        grid=(nt,),
        in_specs=[pl.BlockSpec((None, TOP_K, ROW_TILE), lambda i: (i, 0, 0), memory_space=pltpu.SMEM),
                  pl.BlockSpec((ROW_TILE, d), lambda i: (i, 0))],
        out_specs=pl.BlockSpec(memory_space=pl.ANY),
        out_shape=jax.ShapeDtypeStruct((n * TOP_K, d), F32),
        scratch_shapes=[pltpu.SemaphoreType.DMA(())],
        compiler_params=_params("arbitrary"),
        name="moe_dispatch",
    )(dest_tiles, h)


def _grouped_kernel(layer, vt_ref, ve_ref, starts_ref, nv_ref, first_ref, slot_ref, next_ref,
                    x_ref, wg_hbm, wu_hbm, wd_hbm, y_ref,
                    wgb_ref, wub_ref, wdb_ref, sg_ref, su_ref, sd_ref, wsem):
    v = pl.program_id(0)
    e = ve_ref[v]
    t = vt_ref[v]
    prev = jnp.maximum(v - 1, 0)

    def weight_copies(expert, slot):
        return [pltpu.make_async_copy(src.at[layer, expert], dst.at[slot], wsem.at[slot])
                for src, dst in ((wg_hbm, sg_ref), (wu_hbm, su_ref), (wd_hbm, sd_ref))]

    @pl.when(v == 0)
    def _():
        for cp in weight_copies(e, 0):
            cp.start(priority=WEIGHT_DMA_PRIORITY)

    @pl.when(jnp.logical_and(v < nv_ref[0], first_ref[v] == 1))
    def _():
        slot = slot_ref[v]
        for cp in weight_copies(e, slot):
            cp.wait()
        wgb_ref[...] = sg_ref[slot].astype(BF16)
        wub_ref[...] = su_ref[slot].astype(BF16)
        wdb_ref[...] = sd_ref[slot].astype(BF16)
        nxt = next_ref[v]

        @pl.when(nxt >= 0)
        def _():
            for cp in weight_copies(nxt, 1 - slot):
                cp.start(priority=WEIGHT_DMA_PRIORITY)

    @pl.when(v < nv_ref[0])
    def _():
        x = x_ref[...].astype(BF16)
        a = jnp.dot(x, wgb_ref[...], preferred_element_type=F32)
        b = jnp.dot(x, wub_ref[...], preferred_element_type=F32)
        y = jnp.dot((_silu(a) * b).astype(BF16), wdb_ref[...], preferred_element_type=F32)
        rows = x.shape[0]
        r = t * rows + lax.broadcasted_iota(jnp.int32, (rows, 1), 0)
        y = jnp.where(jnp.logical_and(r >= starts_ref[e], r < starts_ref[e + 1]), y, 0.0)
        first_visit_of_tile = jnp.logical_or(v == 0, vt_ref[prev] != t)

        @pl.when(first_visit_of_tile)
        def _():
            y_ref[...] = y

        @pl.when(jnp.logical_not(first_visit_of_tile))
        def _():
            y_ref[...] += y


def _grouped_experts(xs, layer, w_gate, w_up, w_down, plan):
    p, d = xs.shape
    hidden = w_gate.shape[-1]
    tm = EXPERT_ROW_TILE
    tile = lambda v, vt, *_: (vt[v], 0)
    grid_spec = pltpu.PrefetchScalarGridSpec(
        num_scalar_prefetch=len(plan),
        grid=(plan[0].shape[0],),
        in_specs=[pl.BlockSpec((tm, d), tile),
                  pl.BlockSpec(memory_space=pl.ANY), pl.BlockSpec(memory_space=pl.ANY),
                  pl.BlockSpec(memory_space=pl.ANY)],
        out_specs=pl.BlockSpec((tm, d), tile),
        scratch_shapes=[pltpu.VMEM((d, hidden), BF16), pltpu.VMEM((d, hidden), BF16),
                        pltpu.VMEM((hidden, d), BF16),
                        pltpu.VMEM((2, d, hidden), F32), pltpu.VMEM((2, d, hidden), F32),
                        pltpu.VMEM((2, hidden, d), F32), pltpu.SemaphoreType.DMA((2,))])
    return pl.pallas_call(
        functools.partial(_grouped_kernel, layer),
        grid_spec=grid_spec,
        out_shape=jax.ShapeDtypeStruct((p, d), F32),
        compiler_params=_params("arbitrary"),
        name="moe_grouped_experts",
    )(*plan, xs, w_gate, w_up, w_down)


def _combine_kernel(n_ctx_rows, first_tile, final_norm, dest_ref, dest_next_ref, w_ref, h_ref, s_ref,
                    mod_ref, sg_ref, su_ref, sd_ref, fg_ref, ys_hbm, o_ref, ybuf_ref, sems):
    i = pl.program_id(0)
    n_steps = pl.num_programs(0)
    rows = h_ref.shape[0]
    slot = i % 2

    def gather(dref, to_slot):
        def body(j, carry):
            for k in range(TOP_K):
                pltpu.make_async_copy(ys_hbm.at[pl.ds(dref[k, j], 1)],
                                      ybuf_ref.at[to_slot, k, pl.ds(j, 1)], sems.at[to_slot]).start()
            return carry
        lax.fori_loop(0, rows, body, 0)

    @pl.when(i == 0)
    def _():
        gather(dest_ref, 0)

    @pl.when(i + 1 < n_steps)
    def _():
        gather(dest_next_ref, 1 - slot)

    hb = h_ref[...].astype(BF16)
    a = jnp.dot(hb, sg_ref[...], preferred_element_type=F32)
    b = jnp.dot(hb, su_ref[...], preferred_element_type=F32)
    y = jnp.dot((_silu(a) * b).astype(BF16), sd_ref[...], preferred_element_type=F32)

    pltpu.make_async_copy(ybuf_ref.at[slot], ybuf_ref.at[slot], sems.at[slot]).wait()
    w = w_ref[...]
    for k in range(TOP_K):
        y += w[:, k:k + 1] * ybuf_ref[slot, k]
    r = (first_tile + i) * rows + lax.broadcasted_iota(jnp.int32, (rows, 1), 0)
    gate2 = jnp.where(r < n_ctx_rows, mod_ref[1, 5:6, :], mod_ref[0, 5:6, :])
    out = s_ref[...] + gate2 * y
    o_ref[...] = _rms(out, fg_ref[...]) if final_norm else out


def _combine(ys, dest_tiles, w, h, s, mod, sh_gate_b, sh_up_b, sh_down_b, final_g, n_ctx_rows, last_layer):
    n, d = s.shape
    tc = COMBINE_ROW_TILE
    first_tile = n_ctx_rows // tc if last_layer else 0
    steps = n // tc - first_tile
    hidden = sh_gate_b.shape[-1]
    row = lambda width: pl.BlockSpec((tc, width), lambda i: (i + first_tile, 0))
    idx = lambda shift: pl.BlockSpec(
        (None, TOP_K, tc), lambda i: (jnp.minimum(i + first_tile + shift, n // tc - 1), 0, 0),
        memory_space=pltpu.SMEM)
    return pl.pallas_call(
        functools.partial(_combine_kernel, n_ctx_rows, first_tile, last_layer),
        grid=(steps,),
        in_specs=[idx(0), idx(1), row(TOP_K), row(d), row(d),
                  pl.BlockSpec((2, 8, d), lambda i: (0, 0, 0)),
                  _resident((d, hidden)), _resident((d, hidden)), _resident((hidden, d)),
                  _resident((1, d)),
                  pl.BlockSpec(memory_space=pl.ANY)],
        out_specs=pl.BlockSpec((tc, d), lambda i: (i, 0)),
        out_shape=jax.ShapeDtypeStruct((steps * tc, d), F32),
        scratch_shapes=[pltpu.VMEM((2, TOP_K, tc, d), F32), pltpu.SemaphoreType.DMA((2,))],
        compiler_params=_params("arbitrary"),
        name="moe_combine",
    )(dest_tiles, dest_tiles, w, h, s, mod, sh_gate_b, sh_up_b, sh_down_b, final_g.reshape(1, d), ys)


def _tile_major(dest, tile):
    n, k = dest.shape
    return dest.reshape(n // tile, tile, k).transpose(0, 2, 1)


def _rope_tables(n_ctx, n_tokens, dim):
    half = dim // 2
    inv_freq = ROPE_THETA ** (-jnp.arange(0, half, 2, dtype=F32) / half)
    t = jnp.arange(n_tokens, dtype=jnp.int32)
    ang_r = (t // GRID_W).astype(F32)[:, None] * inv_freq
    ang_c = (t % GRID_W).astype(F32)[:, None] * inv_freq
    cos = jnp.concatenate([jnp.cos(ang_r)] * 2 + [jnp.cos(ang_c)] * 2, axis=-1)
    sin = jnp.concatenate([-jnp.sin(ang_r), jnp.sin(ang_r), -jnp.sin(ang_c), jnp.sin(ang_c)], axis=-1)
    reps = LANES // dim
    cos = jnp.tile(cos, (1, reps))
    sin = jnp.tile(sin, (1, reps))
    cos = jnp.concatenate([jnp.ones((n_ctx, LANES), F32), cos], axis=0)
    sin = jnp.concatenate([jnp.zeros((n_ctx, LANES), F32), sin], axis=0)
    return cos, sin


def kernel(x, c, ctx, c_ctx, norm1_g, norm2_g, w_mod, b_mod, w_in, gla_wa_f, gla_ba_f, gla_wa_b,
           gla_ba_b, gla_norm_g, q_norm_g, k_norm_g, diff_lq1, diff_lk1, diff_lq2, diff_lk2,
           diff_norm_g, w_out, router_w, router_b, exp_w_gate, exp_w_up, exp_w_down,
           sh_w_gate, sh_w_up, sh_w_down, final_g):
    batch, n_tokens, d = x.shape
    n_ctx = ctx.shape[1]
    depth = w_mod.shape[0]
    n_experts = router_w.shape[-1]
    n_rows = n_ctx + n_tokens
    assert batch == 1 and n_ctx == ROW_TILE and n_tokens % ROW_TILE == 0
    assert TOP_K <= n_experts <= LANES and EXPERT_ROW_TILE <= ROW_TILE

    s = jnp.concatenate([ctx[0], x[0]], axis=0)
    mod_all = _modulation(c, c_ctx, w_mod, b_mod)
    tables = _rope_tables(n_ctx, n_tokens, HEAD_DIM) + _rope_tables(n_ctx, n_tokens, DIFF_DQK)

    for l in range(depth):
        first_q_tile = 0
        lam_init = 0.8 - 0.6 * math.exp(-0.3 * l)
        mod = mod_all[l]

        w_in_b = jnp.concatenate(
            [w_in[l][:, :ORIG_LR], w_in[l][:, ORIG_LR + 2 * GLA_GATE_RANK:],
             w_in[l][:, ORIG_LR:ORIG_LR + 2 * GLA_GATE_RANK],
             jnp.zeros((d, LANES - 2 * GLA_GATE_RANK), F32)], axis=1).astype(BF16)
        wa = jnp.zeros((LANES, 2 * GLA_QK), F32)
        wa = wa.at[:GLA_GATE_RANK, :GLA_QK].set(gla_wa_f[l])
        wa = wa.at[GLA_GATE_RANK:2 * GLA_GATE_RANK, GLA_QK:].set(gla_wa_b[l]).astype(BF16)
        ba = jnp.concatenate([gla_ba_f[l], gla_ba_b[l]]).reshape(1, -1)
        lamv = jnp.zeros((8, LANES), F32)
        for r, vec in enumerate((diff_lq1[l], diff_lk1[l], diff_lq2[l], diff_lk2[l])):
            lamv = lamv.at[r, :DIFF_DQK].set(vec)
        rw = jnp.pad(router_w[l], ((0, 0), (0, LANES - n_experts))).astype(BF16)
        rb = jnp.pad(router_b[l], (0, LANES - n_experts)).reshape(1, LANES)

        (glaq, glak, glav, gate, la, q_t, k, v_t, dq_t, dk, dv_t) = _input_projection(
            s, mod, norm1_g[l], w_in_b, wa, ba, q_norm_g[l], k_norm_g[l], tables)
        o_gla = _gla(glaq, glak, glav, la)
        o_gqa = _gqa_attention(q_t, k, v_t, first_q_tile)
        o_diff = _diff_attention(dq_t, dk, dv_t, lamv, diff_norm_g[l], lam_init, first_q_tile)
        s = _output_projection(s, mod, o_gla, gate, o_gqa, o_diff, gla_norm_g[l], w_out[l].astype(BF16))
        h2, eid, rank, w_route, counts_f = _router(s, mod, norm2_g[l], rw, rb, n_experts)
        dest, plan = _routing_plan(eid, rank, counts_f, n_experts, n_rows * TOP_K // EXPERT_ROW_TILE)
        xs = _dispatch(h2, _tile_major(dest, ROW_TILE))
        ys = _grouped_experts(xs, l, exp_w_gate, exp_w_up, exp_w_down, plan)
        s = _combine(ys, _tile_major(dest, COMBINE_ROW_TILE), w_route, h2, s, mod,
                     sh_w_gate[l].astype(BF16), sh_w_up[l].astype(BF16), sh_w_down[l].astype(BF16),
                     final_g, n_ctx, last_layer=l == depth - 1)

    return s.reshape(batch, n_tokens, d)
```

```python
import functools
import math

import jax
import jax.numpy as jnp
from jax import lax
from jax.experimental import pallas as pl
from jax.experimental.pallas import tpu as pltpu

F32 = jnp.float32
BF16 = jnp.bfloat16

GRID_W = 64
HEAD_DIM = 128
GLA_HEADS = 4
GLA_DK = 64
GLA_DV = 128
GLA_GATE_RANK = 16
GLA_TAU = 16.0
GLA_CHUNK = 64
GQA_HEADS = 8
GQA_KV_HEADS = 2
DIFF_HEADS = 4
DIFF_DQK = 64
DIFF_DV = 128
TOP_K = 8
ROUTE_SCALE = 2.5
ROPE_THETA = 10000.0
NORM_EPS = 1e-6
LOG2_E = math.log2(math.e)

LANES = 128
ROW_TILE = 256
EXPERT_ROW_TILE = 256
COMBINE_ROW_TILE = 128
WEIGHT_DMA_PRIORITY = 1
VMEM_LIMIT = 56 * 1024 * 1024

GLA_QK = GLA_HEADS * GLA_DK
GLA_V = GLA_HEADS * GLA_DV
GQA_Q = GQA_HEADS * HEAD_DIM
GQA_KV = GQA_KV_HEADS * HEAD_DIM
DIFF_QK = DIFF_HEADS * 2 * DIFF_DQK
DIFF_V = DIFF_HEADS * DIFF_DV
MIX_WIDTH = GLA_V + GQA_Q + DIFF_V

C_GLAQ = 0
C_GLAK = C_GLAQ + GLA_QK
C_GLAV = C_GLAK + GLA_QK
C_GATE = C_GLAV + GLA_V
C_GQAQ = C_GATE + GLA_V
C_GQAK = C_GQAQ + GQA_Q
C_GQAV = C_GQAK + GQA_KV
C_DQ = C_GQAV + GQA_KV
C_DK = C_DQ + DIFF_QK
C_DV = C_DK + DIFF_QK
C_LR = C_DV + DIFF_V
IN_COLS = C_LR + LANES
ORIG_LR = 2 * GLA_QK + 2 * GLA_V


def _params(*sem):
    return pltpu.CompilerParams(dimension_semantics=sem, vmem_limit_bytes=VMEM_LIMIT)


def _resident(shape):
    nd = len(shape)
    return pl.BlockSpec(shape, lambda *_: (0,) * nd, pipeline_mode=pl.Buffered(1))


def _silu(a):
    return a / (1.0 + jnp.exp(-a))


def _rms(x, g):
    return x * lax.rsqrt(jnp.mean(x * x, axis=-1, keepdims=True) + NORM_EPS) * g


def _row_group(i):
    return jnp.where(i == 0, 1, 0)


def _mod_kernel(a_ref, w_ref, b_ref, o_ref):
    a = _silu(a_ref[...])
    o_ref[...] = jnp.dot(a.astype(BF16), w_ref[...].astype(BF16),
                         preferred_element_type=F32) + b_ref[...]


def _modulation(c, c_ctx, w_mod, b_mod):
    depth, d, six_d = w_mod.shape
    a = jnp.zeros((8, d), F32).at[0].set(c[0]).at[1].set(c_ctx)
    tn = d // 2
    out = pl.pallas_call(
        _mod_kernel,
        grid=(depth, six_d // tn),
        in_specs=[pl.BlockSpec((8, d), lambda l, j: (0, 0)),
                  pl.BlockSpec((None, d, tn), lambda l, j: (l, 0, j)),
                  pl.BlockSpec((None, 1, tn), lambda l, j: (l, 0, j))],
        out_specs=pl.BlockSpec((None, 8, tn), lambda l, j: (l, 0, j)),
        out_shape=jax.ShapeDtypeStruct((depth, 8, six_d), F32),
        compiler_params=_params("parallel", "parallel"),
        name="modulation",
    )(a, w_mod, b_mod.reshape(depth, 1, six_d))
    m = out[:, :2].reshape(depth, 2, 6, d)
    return jnp.pad(m, ((0, 0), (0, 0), (0, 2), (0, 0)))


def _rope(xh, cos, sin, first, shift_first, shift_second):
    partner = jnp.where(first, pltpu.roll(xh, shift_first, 1), pltpu.roll(xh, shift_second, 1))
    return xh * cos + partner * sin


def _inproj_kernel(x_ref, mod_ref, g_ref, w_ref, wa_ref, ba_ref, qg_ref, kg_ref,
                   cg_ref, sg_ref, cd_ref, sd_ref,
                   glaq_ref, glak_ref, glav_ref, gate_ref, la_ref,
                   qt_ref, k_ref, vt_ref, dqt_ref, dk_ref, dvt_ref):
    x = x_ref[...]
    h = _rms(x, g_ref[...]) * (1.0 + mod_ref[1:2, :]) + mod_ref[0:1, :]
    hb = h.astype(BF16)

    def proj(start, width):
        return jnp.dot(hb, w_ref[:, start:start + width], preferred_element_type=F32)

    glaq_ref[...] = proj(C_GLAQ, GLA_QK) * (GLA_DK ** -0.5)
    glak_ref[...] = proj(C_GLAK, GLA_QK)
    glav_ref[...] = proj(C_GLAV, GLA_V)
    gate_ref[...] = proj(C_GATE, GLA_V)
    zv = proj(C_GQAV, GQA_KV)
    for hd in range(GQA_KV_HEADS):
        sl = slice(hd * HEAD_DIM, (hd + 1) * HEAD_DIM)
        vt_ref[sl, :] = zv[:, sl].T.astype(BF16)
    zdv = proj(C_DV, DIFF_V)
    for hd in range(DIFF_HEADS):
        sl = slice(hd * DIFF_DV, (hd + 1) * DIFF_DV)
        dvt_ref[sl, :] = zdv[:, sl].T.astype(BF16)

    z_lr = proj(C_LR, LANES).astype(BF16)
    pre = jnp.dot(z_lr, wa_ref[...], preferred_element_type=F32) + ba_ref[...]
    log_sig = -(jnp.maximum(-pre, 0.0) + jnp.log1p(jnp.exp(-jnp.abs(pre))))
    la = log_sig * (1.0 / GLA_TAU)
    la_ref[0] = la[:, :GLA_QK]
    la_ref[1] = la[:, GLA_QK:]

    rows = x.shape[0]
    lane = lax.broadcasted_iota(jnp.int32, (rows, LANES), 1)
    first_g = (lane % 64) < 32
    first_d = (lane % 32) < 16
    cg, sg, cd, sd = cg_ref[...], sg_ref[...], cd_ref[...], sd_ref[...]
    scale_g = HEAD_DIM ** -0.5 * LOG2_E
    scale_d = DIFF_DQK ** -0.5 * LOG2_E

    zq = proj(C_GQAQ, GQA_Q)
    for hd in range(GQA_HEADS):
        sl = slice(hd * HEAD_DIM, (hd + 1) * HEAD_DIM)
        qh = _rope(_rms(zq[:, sl], qg_ref[...]), cg, sg, first_g, 96, 32) * scale_g
        qt_ref[sl, :] = qh.T.astype(BF16)
    zk = proj(C_GQAK, GQA_KV)
    for hd in range(GQA_KV_HEADS):
        kh = _rms(zk[:, hd * HEAD_DIM:(hd + 1) * HEAD_DIM], kg_ref[...])
        k_ref[:, hd * HEAD_DIM:(hd + 1) * HEAD_DIM] = _rope(kh, cg, sg, first_g, 96, 32).astype(BF16)
    zdq = proj(C_DQ, DIFF_QK)
    zdk = proj(C_DK, DIFF_QK)
    for hd in range(DIFF_HEADS):
        sl = slice(hd * LANES, (hd + 1) * LANES)
        dqt_ref[sl, :] = (_rope(zdq[:, sl], cd, sd, first_d, 112, 16) * scale_d).T.astype(BF16)
        dk_ref[:, sl] = _rope(zdk[:, sl], cd, sd, first_d, 112, 16).astype(BF16)


def _input_projection(s, mod, norm_g, w_in_b, wa, ba, q_norm_g, k_norm_g, tables):
    n, d = s.shape
    nt = n // ROW_TILE
    row = lambda w: pl.BlockSpec((ROW_TILE, w), lambda i: (i, 0))
    col = lambda w: pl.BlockSpec((w, ROW_TILE), lambda i: (0, i))
    tile_t = lambda w: pl.BlockSpec((None, w, ROW_TILE), lambda i: (i, 0, 0))
    f32o = lambda w: jax.ShapeDtypeStruct((n, w), F32)
    bfo = lambda w: jax.ShapeDtypeStruct((n, w), BF16)
    return pl.pallas_call(
        _inproj_kernel,
        grid=(nt,),
        in_specs=[row(d),
                  pl.BlockSpec((None, 8, d), lambda i: (_row_group(i), 0, 0)),
                  _resident((1, d)),
                  _resident((d, IN_COLS)),
                  _resident((LANES, 2 * GLA_QK)),
                  _resident((1, 2 * GLA_QK)),
                  _resident((1, HEAD_DIM)),
                  _resident((1, HEAD_DIM)),
                  row(LANES), row(LANES), row(LANES), row(LANES)],
        out_specs=[row(GLA_QK), row(GLA_QK), row(GLA_V), row(GLA_V),
                   pl.BlockSpec((2, ROW_TILE, GLA_QK), lambda i: (0, i, 0)),
                   col(GQA_Q), row(GQA_KV), tile_t(GQA_KV), col(DIFF_QK), row(DIFF_QK), tile_t(DIFF_V)],
        out_shape=[f32o(GLA_QK), f32o(GLA_QK), f32o(GLA_V), f32o(GLA_V),
                   jax.ShapeDtypeStruct((2, n, GLA_QK), F32),
                   jax.ShapeDtypeStruct((GQA_Q, n), BF16), bfo(GQA_KV),
                   jax.ShapeDtypeStruct((nt, GQA_KV, ROW_TILE), BF16),
                   jax.ShapeDtypeStruct((DIFF_QK, n), BF16), bfo(DIFF_QK),
                   jax.ShapeDtypeStruct((nt, DIFF_V, ROW_TILE), BF16)],
        compiler_params=_params("parallel"),
        name="input_projection",
    )(s, mod, norm_g.reshape(1, d), w_in_b, wa, ba, q_norm_g.reshape(1, -1), k_norm_g.reshape(1, -1),
      *tables)


def _gla_block(reverse, q_ref, k_ref, v_ref, la_ref, o_ref, st_ref):
    rows = q_ref.shape[0]
    n_chunks = rows // GLA_CHUNK
    ri = lax.broadcasted_iota(jnp.int32, (rows, rows), 0)
    ci = lax.broadcasted_iota(jnp.int32, (rows, rows), 1)
    same_chunk = (ri // GLA_CHUNK) == (ci // GLA_CHUNK)
    allowed = same_chunk & ((ci >= ri) if reverse else (ci <= ri))
    la = la_ref[...]
    cum = jnp.dot(jnp.where(allowed, 1.0, 0.0).astype(F32), la,
                  preferred_element_type=F32, precision=lax.Precision.HIGHEST)
    tot = jnp.dot(jnp.where(same_chunk, 1.0, 0.0).astype(F32), la,
                  preferred_element_type=F32, precision=lax.Precision.HIGHEST)
    k = k_ref[...]
    q_dec = q_ref[...] * jnp.exp(cum)
    k_inv = (k * jnp.exp(-cum)).astype(BF16)
    k_end = k * jnp.exp(tot - cum)
    v = v_ref[...]
    vb = v.astype(BF16)
    v_t = v.T.astype(BF16)
    lane_head = lax.broadcasted_iota(jnp.int32, (rows, GLA_QK), 1) // GLA_DK
    row_chunk = lax.broadcasted_iota(jnp.int32, (rows, GLA_QK), 0) // GLA_CHUNK
    nt_dims = (((1,), (1,)), ((), ()))

    for hd in range(GLA_HEADS):
        qh = jnp.where(lane_head == hd, q_dec, 0.0).astype(BF16)
        a = lax.dot_general(qh, k_inv, nt_dims, preferred_element_type=F32)
        a = jnp.where(allowed, a, 0.0).astype(BF16)
        o_ref[:, hd * GLA_DV:(hd + 1) * GLA_DV] = jnp.dot(
            a, vb[:, hd * GLA_DV:(hd + 1) * GLA_DV], preferred_element_type=F32)

    chunk_order = range(n_chunks - 1, -1, -1) if reverse else range(n_chunks)
    chunk_lane_head = lax.broadcasted_iota(jnp.int32, (GLA_CHUNK, GLA_QK), 1) // GLA_DK
    for c in chunk_order:
        sl = slice(c * GLA_CHUNK, (c + 1) * GLA_CHUNK)
        state = st_ref[...]
        q4 = jnp.concatenate(
            [jnp.where(chunk_lane_head == hd, q_dec[sl], 0.0) for hd in range(GLA_HEADS)],
            axis=0).astype(BF16)
        r = lax.dot_general(q4, state.astype(BF16), nt_dims, preferred_element_type=F32)
        for hd in range(GLA_HEADS):
            o_ref[sl, hd * GLA_DV:(hd + 1) * GLA_DV] += r[hd * GLA_CHUNK:(hd + 1) * GLA_CHUNK,
                                                          hd * GLA_DV:(hd + 1) * GLA_DV]
        k_end_c = jnp.where(row_chunk == c, k_end, 0.0).astype(BF16)
        u_t = jnp.dot(v_t, k_end_c, preferred_element_type=F32)
        st_ref[...] = state * jnp.exp(tot[c * GLA_CHUNK:c * GLA_CHUNK + 1]) + u_t


def _gla_kernel(q_ref, k_ref, v_ref, la_ref, o_ref, st_ref):
    @pl.when(pl.program_id(1) == 0)
    def _():
        st_ref[...] = jnp.zeros_like(st_ref)

    @pl.when(pl.program_id(0) == 0)
    def _():
        _gla_block(False, q_ref, k_ref, v_ref, la_ref, o_ref, st_ref)

    @pl.when(pl.program_id(0) == 1)
    def _():
        _gla_block(True, q_ref, k_ref, v_ref, la_ref, o_ref, st_ref)


def _gla(glaq, glak, glav, la):
    n = glaq.shape[0]
    nt = n // ROW_TILE

    def blk(dr, j):
        return jnp.where(dr == 0, j, jnp.where(j == 0, 0, nt - j))

    row = lambda w: pl.BlockSpec((ROW_TILE, w), lambda dr, j: (blk(dr, j), 0))
    return pl.pallas_call(
        _gla_kernel,
        grid=(2, nt),
        in_specs=[row(GLA_QK), row(GLA_QK), row(GLA_V),
                  pl.BlockSpec((None, ROW_TILE, GLA_QK), lambda dr, j: (dr, blk(dr, j), 0))],
        out_specs=pl.BlockSpec((None, ROW_TILE, GLA_V), lambda dr, j: (dr, blk(dr, j), 0)),
        out_shape=jax.ShapeDtypeStruct((2, n, GLA_V), F32),
        scratch_shapes=[pltpu.VMEM((GLA_V, GLA_QK), F32)],
        compiler_params=_params("arbitrary", "arbitrary"),
        name="gla_scan",
    )(glaq, glak, glav, la)


class _FlashMaps:
    def __init__(self, q_maps, k_of_map, load_k, load_vt):
        self.q_maps, self.k_of_map, self.load_k, self.load_vt = q_maps, k_of_map, load_k, load_vt


def _flash_steps(maps, first_tile, tiles_per_step, n_steps, s_ref, m_ref, l_ref, acc_ref):
    n_maps = len(maps.q_maps)
    rows = tiles_per_step * ROW_TILE
    for u in range(n_steps):
        first_row = pl.multiple_of((first_tile + u * tiles_per_step) * ROW_TILE, ROW_TILE)
        keys = {src: maps.load_k(src, first_row, rows) for src in sorted(set(maps.k_of_map))}
        for j in range(n_maps):
            s_ref[u * n_maps + j, :rows] = jnp.dot(keys[maps.k_of_map[j]], maps.q_maps[j],
                                                   preferred_element_type=F32)
    for u in range(n_steps):
        tile = first_tile + u * tiles_per_step
        for j in range(n_maps):
            s_t = s_ref[u * n_maps + j, :rows]
            m_prev = m_ref[j]
            m_new = jnp.maximum(m_prev, jnp.max(s_t, axis=0, keepdims=True))
            alpha = jnp.exp2(m_prev - m_new)
            p = jnp.exp2(s_t - m_new)
            l_ref[j] = alpha * l_ref[j] + jnp.sum(p, axis=0, keepdims=True)
            pb = p.astype(BF16)
            pv = jnp.dot(maps.load_vt(maps.k_of_map[j], tile), pb[:ROW_TILE], preferred_element_type=F32)
            for r in range(1, tiles_per_step):
                pv += jnp.dot(maps.load_vt(maps.k_of_map[j], tile + r), pb[r * ROW_TILE:(r + 1) * ROW_TILE],
                              preferred_element_type=F32)
            acc_ref[j] = alpha * acc_ref[j] + pv
            m_ref[j] = m_new


def _flash_attend(maps, latent, n_key_tiles, tiles_per_step, n_steps, s_ref, m_ref, l_ref, acc_ref):
    m_ref[...] = jnp.full(m_ref.shape, -jnp.inf, F32)
    l_ref[...] = jnp.zeros(l_ref.shape, F32)
    acc_ref[...] = jnp.zeros(acc_ref.shape, F32)
    _flash_steps(maps, 0, 1, 1, s_ref, m_ref, l_ref, acc_ref)
    if not latent:
        return
    per_iter = tiles_per_step * n_steps

    def body(it, carry):
        _flash_steps(maps, 1 + it * per_iter, tiles_per_step, n_steps, s_ref, m_ref, l_ref, acc_ref)
        return carry

    lax.fori_loop(0, (n_key_tiles - 1) // per_iter, body, 0)


Q_SUBTILES = 2


def _query_specs(width, latent):
    if not latent:
        return [pl.BlockSpec((width, ROW_TILE), lambda g, i: (g, 0))]
    return [pl.BlockSpec((width, ROW_TILE), lambda g, i, sub=sub: (g, 1 + Q_SUBTILES * i + sub))
            for sub in range(Q_SUBTILES)]


def _query_steps(n_key_tiles, latent):
    assert (n_key_tiles - 1) % Q_SUBTILES == 0
    return (n_key_tiles - 1) // Q_SUBTILES if latent else 1


def _flash_plan(n_key_tiles):
    latent = n_key_tiles - 1
    tiles_per_step = 2 if latent % 2 == 0 else 1
    n_steps = 2 if latent % (2 * tiles_per_step) == 0 else 1
    return tiles_per_step, n_steps


def _flash_scratch(n_maps, dv, plan):
    tiles_per_step, n_steps = plan
    return [pltpu.VMEM((n_steps * n_maps, tiles_per_step * ROW_TILE, ROW_TILE), F32),
            pltpu.VMEM((n_maps, 1, ROW_TILE), F32), pltpu.VMEM((n_maps, 1, ROW_TILE), F32),
            pltpu.VMEM((n_maps, dv, ROW_TILE), F32)]


GQA_GROUP = GQA_HEADS // GQA_KV_HEADS


def _gqa_kernel(latent, plan, *refs):
    n_sub = Q_SUBTILES if latent else 1
    q_refs = refs[:n_sub]
    k_ref, vt_ref, o_ref, s_ref, m_ref, l_ref, acc_ref = refs[n_sub:]
    maps = _FlashMaps(
        q_maps=[q[hd * HEAD_DIM:(hd + 1) * HEAD_DIM, :] for q in q_refs for hd in range(GQA_GROUP)],
        k_of_map=[0] * (n_sub * GQA_GROUP),
        load_k=lambda src, first_row, rows: k_ref[pl.ds(first_row, rows), :],
        load_vt=lambda src, tile: vt_ref[tile])
    _flash_attend(maps, latent, vt_ref.shape[0], *plan, s_ref, m_ref, l_ref, acc_ref)
    for sub in range(n_sub):
        for hd in range(GQA_GROUP):
            j = sub * GQA_GROUP + hd
            o_t = acc_ref[j] / l_ref[j]
            o_ref[sub * ROW_TILE:(sub + 1) * ROW_TILE, hd * HEAD_DIM:(hd + 1) * HEAD_DIM] = (
                o_t.T.astype(o_ref.dtype))


def _gqa_attention(q_t, k, v_t, latent):
    n = k.shape[0]
    nt = n // ROW_TILE
    gw = GQA_GROUP * HEAD_DIM
    plan = _flash_plan(nt)
    q_specs = _query_specs(gw, latent)
    steps = _query_steps(nt, latent)
    rows = len(q_specs) * ROW_TILE
    return pl.pallas_call(
        functools.partial(_gqa_kernel, latent, plan),
        grid=(GQA_KV_HEADS, steps),
        in_specs=q_specs + [pl.BlockSpec((n, HEAD_DIM), lambda g, i: (0, g)),
                            pl.BlockSpec((nt, HEAD_DIM, ROW_TILE), lambda g, i: (0, g, 0))],
        out_specs=pl.BlockSpec((rows, gw), lambda g, i: (i, g)),
        out_shape=jax.ShapeDtypeStruct((steps * rows, GQA_Q), BF16),
        scratch_shapes=_flash_scratch(len(q_specs) * GQA_GROUP, HEAD_DIM, plan),
        compiler_params=_params("parallel", "arbitrary"),
        name="gqa_attention",
    )(*([q_t] * len(q_specs)), k, v_t)


DIFF_PAIR = 2


def _diff_kernel(latent, plan, lam_init, *refs):
    n_sub = Q_SUBTILES if latent else 1
    q_refs = refs[:n_sub]
    k_ref, vt_ref, lamv_ref, g_ref, o_ref, s_ref, m_ref, l_ref, acc_ref = refs[n_sub:]
    row = lax.broadcasted_iota(jnp.int32, (LANES, ROW_TILE), 0)
    q_maps = []
    for q_ref in q_refs:
        for hd in range(DIFF_PAIR):
            q_t = q_ref[hd * LANES:(hd + 1) * LANES, :]
            zero = jnp.zeros_like(q_t)
            q_maps += [jnp.where(row < DIFF_DQK, q_t, zero), jnp.where(row >= DIFF_DQK, q_t, zero)]
    maps = _FlashMaps(
        q_maps=q_maps,
        k_of_map=[hd for _ in range(n_sub) for hd in range(DIFF_PAIR) for _ in range(2)],
        load_k=lambda src, first_row, rows: k_ref[pl.ds(first_row, rows), src * LANES:(src + 1) * LANES],
        load_vt=lambda src, tile: vt_ref[tile, src * DIFF_DV:(src + 1) * DIFF_DV, :])
    _flash_attend(maps, latent, vt_ref.shape[0], *plan, s_ref, m_ref, l_ref, acc_ref)
    lv = lamv_ref[...]
    lam = (jnp.exp(jnp.sum(lv[0:1] * lv[1:2], axis=-1, keepdims=True))
           - jnp.exp(jnp.sum(lv[2:3] * lv[3:4], axis=-1, keepdims=True)) + lam_init)
    for sub in range(n_sub):
        for hd in range(DIFF_PAIR):
            j = 2 * (sub * DIFF_PAIR + hd)
            o_t = acc_ref[j] / l_ref[j] - lam * (acc_ref[j + 1] / l_ref[j + 1])
            o = _rms(o_t.T, g_ref[...]) * (1.0 - lam_init)
            o_ref[sub * ROW_TILE:(sub + 1) * ROW_TILE, hd * DIFF_DV:(hd + 1) * DIFF_DV] = o.astype(o_ref.dtype)


def _diff_attention(q_t, k, v_t, lamv, norm_g, lam_init, latent):
    n = k.shape[0]
    nt = n // ROW_TILE
    pw = DIFF_PAIR * LANES
    plan = _flash_plan(nt)
    q_specs = _query_specs(pw, latent)
    steps = _query_steps(nt, latent)
    rows = len(q_specs) * ROW_TILE
    return pl.pallas_call(
        functools.partial(_diff_kernel, latent, plan, lam_init),
        grid=(DIFF_HEADS // DIFF_PAIR, steps),
        in_specs=q_specs + [pl.BlockSpec((n, pw), lambda h, i: (0, h)),
                            pl.BlockSpec((nt, DIFF_PAIR * DIFF_DV, ROW_TILE), lambda h, i: (0, h, 0)),
                            pl.BlockSpec((8, LANES), lambda h, i: (0, 0)),
                            pl.BlockSpec((1, DIFF_DV), lambda h, i: (0, 0))],
        out_specs=pl.BlockSpec((rows, DIFF_PAIR * DIFF_DV), lambda h, i: (i, h)),
        out_shape=jax.ShapeDtypeStruct((steps * rows, DIFF_V), BF16),
        scratch_shapes=_flash_scratch(2 * DIFF_PAIR * len(q_specs), DIFF_DV, plan),
        compiler_params=_params("parallel", "arbitrary"),
        name="diff_attention",
    )(*([q_t] * len(q_specs)), k, v_t, lamv, norm_g.reshape(1, -1))


def _outproj_kernel(s_ref, mod_ref, og_ref, gate_ref, oq_ref, od_ref, gg_ref, w_ref, o_ref):
    og = og_ref[0] + og_ref[1]
    gate = gate_ref[...]
    y = jnp.zeros(s_ref.shape, F32)
    for hd in range(GLA_HEADS):
        sl = slice(hd * GLA_DV, (hd + 1) * GLA_DV)
        oh = _rms(og[:, sl], gg_ref[...]) * _silu(gate[:, sl])
        y += jnp.dot(oh.astype(BF16), w_ref[hd * GLA_DV:(hd + 1) * GLA_DV, :], preferred_element_type=F32)
    y += jnp.dot(oq_ref[...], w_ref[GLA_V:GLA_V + GQA_Q, :], preferred_element_type=F32)
    y += jnp.dot(od_ref[...], w_ref[GLA_V + GQA_Q:, :], preferred_element_type=F32)
    o_ref[...] = s_ref[...] + mod_ref[2:3, :] * y


def _output_projection(s, mod, o_gla, gate, o_gqa, o_diff, gla_norm_g, w_out_b):
    n, d = s.shape
    nt = n // ROW_TILE
    row = lambda w: pl.BlockSpec((ROW_TILE, w), lambda i: (i, 0))
    return pl.pallas_call(
        _outproj_kernel,
        grid=(nt,),
        in_specs=[row(d),
                  pl.BlockSpec((None, 8, d), lambda i: (_row_group(i), 0, 0)),
                  pl.BlockSpec((2, ROW_TILE, GLA_V), lambda i: (0, i, 0)),
                  row(GLA_V), row(GQA_Q), row(DIFF_V),
                  _resident((1, GLA_DV)),
                  _resident((MIX_WIDTH, d))],
        out_specs=row(d),
        out_shape=jax.ShapeDtypeStruct((n, d), F32),
        compiler_params=_params("parallel"),
        name="output_projection",
    )(s, mod, o_gla, gate, o_gqa, o_diff, gla_norm_g.reshape(1, -1), w_out_b)


def _router_kernel(n_experts, s_ref, mod_ref, g_ref, rw_ref, rb_ref,
                   h_ref, eid_ref, rank_ref, w_ref, count_ref, carry_ref):
    @pl.when(pl.program_id(0) == 0)
    def _():
        carry_ref[...] = jnp.zeros_like(carry_ref)

    h = _rms(s_ref[...], g_ref[...]) * (1.0 + mod_ref[4:5, :]) + mod_ref[3:4, :]
    h_ref[...] = h
    logits = jnp.dot(h.astype(BF16), rw_ref[...], preferred_element_type=F32)
    scores = 1.0 / (1.0 + jnp.exp(-logits))
    rows = scores.shape[0]
    lane = lax.broadcasted_iota(jnp.int32, scores.shape, 1)
    lane_f = lane.astype(F32)
    cand = jnp.where(lane < n_experts, scores + rb_ref[...], -jnp.inf)
    hits = []
    for _ in range(TOP_K):
        best = jnp.max(cand, axis=-1, keepdims=True)
        first = jnp.min(jnp.where(cand == best, lane_f, float(LANES)), axis=-1, keepdims=True)
        hit = lane_f == first
        hits.append(hit)
        cand = jnp.where(hit, -jnp.inf, cand)
    chosen = functools.reduce(jnp.logical_or, hits)
    total = jnp.sum(jnp.where(chosen, scores, 0.0), axis=-1, keepdims=True)
    ri = lax.broadcasted_iota(jnp.int32, (rows, rows), 0)
    ci = lax.broadcasted_iota(jnp.int32, (rows, rows), 1)
    chosen_b = jnp.where(chosen, 1.0, 0.0).astype(BF16)
    before = jnp.dot(jnp.where(ci < ri, 1.0, 0.0).astype(BF16), chosen_b, preferred_element_type=F32)
    rank_all = before + carry_ref[...]
    for k, hit in enumerate(hits):
        pick = lambda a: jnp.sum(jnp.where(hit, a, 0.0), axis=-1, keepdims=True)
        eid_ref[:, k:k + 1] = pick(lane_f).astype(jnp.int32)
        rank_ref[:, k:k + 1] = pick(rank_all).astype(jnp.int32)
        w_ref[:, k:k + 1] = pick(scores) / total * ROUTE_SCALE
    carry_ref[...] += jnp.sum(chosen_b.astype(F32), axis=0, keepdims=True)
    count_ref[...] = carry_ref[...]


def _router(s, mod, norm_g, router_w_b, router_b, n_experts):
    n, d = s.shape
    nt = n // ROW_TILE
    row = lambda w: pl.BlockSpec((ROW_TILE, w), lambda i: (i, 0))
    return pl.pallas_call(
        functools.partial(_router_kernel, n_experts),
        grid=(nt,),
        in_specs=[row(d),
                  pl.BlockSpec((None, 8, d), lambda i: (_row_group(i), 0, 0)),
                  _resident((1, d)),
                  _resident((d, LANES)),
                  _resident((1, LANES))],
        out_specs=[row(d), row(TOP_K), row(TOP_K), row(TOP_K), pl.BlockSpec((1, LANES), lambda i: (0, 0))],
        out_shape=[jax.ShapeDtypeStruct((n, d), F32), jax.ShapeDtypeStruct((n, TOP_K), jnp.int32),
                   jax.ShapeDtypeStruct((n, TOP_K), jnp.int32), jax.ShapeDtypeStruct((n, TOP_K), F32),
                   jax.ShapeDtypeStruct((1, LANES), F32)],
        scratch_shapes=[pltpu.VMEM((1, LANES), F32)],
        compiler_params=_params("arbitrary"),
        name="router",
    )(s, mod, norm_g.reshape(1, d), router_w_b, router_b)


def _routing_plan(eid, rank, counts_f, n_experts, n_tiles):
    counts = counts_f[0, :n_experts].astype(jnp.int32)
    starts = jnp.concatenate([jnp.zeros((1,), jnp.int32), jnp.cumsum(counts)])
    experts = jnp.arange(n_experts, dtype=jnp.int32)
    dest = jnp.sum(jnp.where(eid[..., None] == experts, starts[:-1], 0), axis=-1) + rank
    first_tile = starts[:-1] // EXPERT_ROW_TILE
    last_tile = (starts[1:] - 1) // EXPERT_ROW_TILE
    visits_per_expert = jnp.where(counts > 0, last_tile - first_tile + 1, 0)
    visit_end = jnp.cumsum(visits_per_expert)
    visit_start = visit_end - visits_per_expert
    n_visits = visit_end[-1]
    v = jnp.minimum(jnp.arange(n_tiles + n_experts - 1, dtype=jnp.int32), n_visits - 1)
    visit_expert = jnp.sum((v[:, None] >= visit_end[None, :]).astype(jnp.int32), axis=-1)
    own = visit_expert[:, None] == experts[None, :]
    visit_tile = jnp.sum(jnp.where(own, (first_tile - visit_start)[None, :], 0), axis=-1) + v
    visited = visits_per_expert > 0
    later = jnp.logical_and(experts[None, :] > experts[:, None], visited[None, :])
    next_visited = jnp.min(jnp.where(later, experts[None, :], n_experts), axis=-1)
    next_visited = jnp.where(next_visited == n_experts, -1, next_visited)
    visit_next = jnp.sum(jnp.where(own, next_visited[None, :], 0), axis=-1)
    visit_first = jnp.sum(jnp.where(own, visit_start[None, :], 0), axis=-1) == v
    order = jnp.cumsum(visited.astype(jnp.int32)) - 1
    visit_slot = jnp.sum(jnp.where(own, order[None, :], 0), axis=-1) % 2
    plan = (visit_tile, visit_expert, starts, n_visits.reshape(1),
            visit_first.astype(jnp.int32), visit_slot, visit_next)
    return dest, plan


def _dispatch_kernel(dest_ref, h_ref, xs_hbm, sem):
    def body(j, carry):
        for k in range(TOP_K):
            pltpu.make_async_copy(h_ref.at[pl.ds(j, 1)], xs_hbm.at[pl.ds(dest_ref[k, j], 1)], sem).start()
        return carry

    lax.fori_loop(0, ROW_TILE, body, 0)
    all_rows = xs_hbm.at[pl.ds(0, ROW_TILE * TOP_K)]
    pltpu.make_async_copy(all_rows, all_rows, sem).wait()


def _dispatch(h, dest_tiles):
    n, d = h.shape
    nt = n // ROW_TILE
    return pl.pallas_call(
        _dispatch_kernel,
        grid=(nt,),
        in_specs=[pl.BlockSpec((None, TOP_K, ROW_TILE), lambda i: (i, 0, 0), memory_space=pltpu.SMEM),
                  pl.BlockSpec((ROW_TILE, d), lambda i: (i, 0))],
        out_specs=pl.BlockSpec(memory_space=pl.ANY),
        out_shape=jax.ShapeDtypeStruct((n * TOP_K, d), F32),
        scratch_shapes=[pltpu.SemaphoreType.DMA(())],
        compiler_params=_params("arbitrary"),
        name="moe_dispatch",
    )(dest_tiles, h)


def _grouped_kernel(layer, vt_ref, ve_ref, starts_ref, nv_ref, first_ref, slot_ref, next_ref,
                    x_ref, wg_hbm, wu_hbm, wd_hbm, y_ref,
                    wgb_ref, wub_ref, wdb_ref, sg_ref, su_ref, sd_ref, wsem):
    v = pl.program_id(0)
    e = ve_ref[v]
    t = vt_ref[v]
    prev = jnp.maximum(v - 1, 0)

    def weight_copies(expert, slot):
        return [pltpu.make_async_copy(src.at[layer, expert], dst.at[slot], wsem.at[slot])
                for src, dst in ((wg_hbm, sg_ref), (wu_hbm, su_ref), (wd_hbm, sd_ref))]

    @pl.when(v == 0)
    def _():
        for cp in weight_copies(e, 0):
            cp.start(priority=WEIGHT_DMA_PRIORITY)

    @pl.when(jnp.logical_and(v < nv_ref[0], first_ref[v] == 1))
    def _():
        slot = slot_ref[v]
        for cp in weight_copies(e, slot):
            cp.wait()
        wgb_ref[...] = sg_ref[slot].astype(BF16)
        wub_ref[...] = su_ref[slot].astype(BF16)
        wdb_ref[...] = sd_ref[slot].astype(BF16)
        nxt = next_ref[v]

        @pl.when(nxt >= 0)
        def _():
            for cp in weight_copies(nxt, 1 - slot):
                cp.start(priority=WEIGHT_DMA_PRIORITY)

    @pl.when(v < nv_ref[0])
    def _():
        x = x_ref[...].astype(BF16)
        a = jnp.dot(x, wgb_ref[...], preferred_element_type=F32)
        b = jnp.dot(x, wub_ref[...], preferred_element_type=F32)
        y = jnp.dot((_silu(a) * b).astype(BF16), wdb_ref[...], preferred_element_type=F32)
        rows = x.shape[0]
        r = t * rows + lax.broadcasted_iota(jnp.int32, (rows, 1), 0)
        y = jnp.where(jnp.logical_and(r >= starts_ref[e], r < starts_ref[e + 1]), y, 0.0)
        first_visit_of_tile = jnp.logical_or(v == 0, vt_ref[prev] != t)

        @pl.when(first_visit_of_tile)
        def _():
            y_ref[...] = y

        @pl.when(jnp.logical_not(first_visit_of_tile))
        def _():
            y_ref[...] += y


def _grouped_experts(xs, layer, w_gate, w_up, w_down, plan):
    p, d = xs.shape
    hidden = w_gate.shape[-1]
    tm = EXPERT_ROW_TILE
    tile = lambda v, vt, *_: (vt[v], 0)
    grid_spec = pltpu.PrefetchScalarGridSpec(
        num_scalar_prefetch=len(plan),
        grid=(plan[0].shape[0],),
        in_specs=[pl.BlockSpec((tm, d), tile),
                  pl.BlockSpec(memory_space=pl.ANY), pl.BlockSpec(memory_space=pl.ANY),
                  pl.BlockSpec(memory_space=pl.ANY)],
        out_specs=pl.BlockSpec((tm, d), tile),
        scratch_shapes=[pltpu.VMEM((d, hidden), BF16), pltpu.VMEM((d, hidden), BF16),
                        pltpu.VMEM((hidden, d), BF16),
                        pltpu.VMEM((2, d, hidden), F32), pltpu.VMEM((2, d, hidden), F32),
                        pltpu.VMEM((2, hidden, d), F32), pltpu.SemaphoreType.DMA((2,))])
    return pl.pallas_call(
        functools.partial(_grouped_kernel, layer),
        grid_spec=grid_spec,
        out_shape=jax.ShapeDtypeStruct((p, d), F32),
        compiler_params=_params("arbitrary"),
        name="moe_grouped_experts",
    )(*plan, xs, w_gate, w_up, w_down)


def _combine_kernel(n_ctx_rows, first_tile, final_norm, dest_ref, dest_next_ref, w_ref, h_ref, s_ref,
                    mod_ref, sg_ref, su_ref, sd_ref, fg_ref, ys_hbm, o_ref, ybuf_ref, sems):
    i = pl.program_id(0)
    n_steps = pl.num_programs(0)
    rows = h_ref.shape[0]
    slot = i % 2

    def gather(dref, to_slot):
        def body(j, carry):
            for k in range(TOP_K):
                pltpu.make_async_copy(ys_hbm.at[pl.ds(dref[k, j], 1)],
                                      ybuf_ref.at[to_slot, k, pl.ds(j, 1)], sems.at[to_slot]).start()
            return carry
        lax.fori_loop(0, rows, body, 0)

    @pl.when(i == 0)
    def _():
        gather(dest_ref, 0)

    @pl.when(i + 1 < n_steps)
    def _():
        gather(dest_next_ref, 1 - slot)

    hb = h_ref[...].astype(BF16)
    a = jnp.dot(hb, sg_ref[...], preferred_element_type=F32)
    b = jnp.dot(hb, su_ref[...], preferred_element_type=F32)
    y = jnp.dot((_silu(a) * b).astype(BF16), sd_ref[...], preferred_element_type=F32)

    pltpu.make_async_copy(ybuf_ref.at[slot], ybuf_ref.at[slot], sems.at[slot]).wait()
    w = w_ref[...]
    for k in range(TOP_K):
        y += w[:, k:k + 1] * ybuf_ref[slot, k]
    r = (first_tile + i) * rows + lax.broadcasted_iota(jnp.int32, (rows, 1), 0)
    gate2 = jnp.where(r < n_ctx_rows, mod_ref[1, 5:6, :], mod_ref[0, 5:6, :])
    out = s_ref[...] + gate2 * y
    o_ref[...] = _rms(out, fg_ref[...]) if final_norm else out


def _combine(ys, dest_tiles, w, h, s, mod, sh_gate_b, sh_up_b, sh_down_b, final_g, n_ctx_rows, last_layer):
    n, d = s.shape
    tc = COMBINE_ROW_TILE
    first_tile = n_ctx_rows // tc if last_layer else 0
    steps = n // tc - first_tile
    hidden = sh_gate_b.shape[-1]
    row = lambda width: pl.BlockSpec((tc, width), lambda i: (i + first_tile, 0))
    idx = lambda shift: pl.BlockSpec(
        (None, TOP_K, tc), lambda i: (jnp.minimum(i + first_tile + shift, n // tc - 1), 0, 0),
        memory_space=pltpu.SMEM)
    return pl.pallas_call(
        functools.partial(_combine_kernel, n_ctx_rows, first_tile, last_layer),
        grid=(steps,),
        in_specs=[idx(0), idx(1), row(TOP_K), row(d), row(d),
                  pl.BlockSpec((2, 8, d), lambda i: (0, 0, 0)),
                  _resident((d, hidden)), _resident((d, hidden)), _resident((hidden, d)),
                  _resident((1, d)),
                  pl.BlockSpec(memory_space=pl.ANY)],
        out_specs=pl.BlockSpec((tc, d), lambda i: (i, 0)),
        out_shape=jax.ShapeDtypeStruct((steps * tc, d), F32),
        scratch_shapes=[pltpu.VMEM((2, TOP_K, tc, d), F32), pltpu.SemaphoreType.DMA((2,))],
        compiler_params=_params("arbitrary"),
        name="moe_combine",
    )(dest_tiles, dest_tiles, w, h, s, mod, sh_gate_b, sh_up_b, sh_down_b, final_g.reshape(1, d), ys)


def _tile_major(dest, tile):
    n, k = dest.shape
    return dest.reshape(n // tile, tile, k).transpose(0, 2, 1)


def _rope_tables(n_ctx, n_tokens, dim):
    half = dim // 2
    inv_freq = ROPE_THETA ** (-jnp.arange(0, half, 2, dtype=F32) / half)
    t = jnp.arange(n_tokens, dtype=jnp.int32)
    ang_r = (t // GRID_W).astype(F32)[:, None] * inv_freq
    ang_c = (t % GRID_W).astype(F32)[:, None] * inv_freq
    cos = jnp.concatenate([jnp.cos(ang_r)] * 2 + [jnp.cos(ang_c)] * 2, axis=-1)
    sin = jnp.concatenate([-jnp.sin(ang_r), jnp.sin(ang_r), -jnp.sin(ang_c), jnp.sin(ang_c)], axis=-1)
    reps = LANES // dim
    cos = jnp.tile(cos, (1, reps))
    sin = jnp.tile(sin, (1, reps))
    cos = jnp.concatenate([jnp.ones((n_ctx, LANES), F32), cos], axis=0)
    sin = jnp.concatenate([jnp.zeros((n_ctx, LANES), F32), sin], axis=0)
    return cos, sin


def kernel(x, c, ctx, c_ctx, norm1_g, norm2_g, w_mod, b_mod, w_in, gla_wa_f, gla_ba_f, gla_wa_b,
           gla_ba_b, gla_norm_g, q_norm_g, k_norm_g, diff_lq1, diff_lk1, diff_lq2, diff_lk2,
           diff_norm_g, w_out, router_w, router_b, exp_w_gate, exp_w_up, exp_w_down,
           sh_w_gate, sh_w_up, sh_w_down, final_g):
    batch, n_tokens, d = x.shape
    n_ctx = ctx.shape[1]
    depth = w_mod.shape[0]
    n_experts = router_w.shape[-1]
    n_rows = n_ctx + n_tokens
    assert batch == 1 and n_ctx == ROW_TILE and n_tokens % ROW_TILE == 0
    assert TOP_K <= n_experts <= LANES and EXPERT_ROW_TILE <= ROW_TILE

    s = jnp.concatenate([ctx[0], x[0]], axis=0)
    mod_all = _modulation(c, c_ctx, w_mod, b_mod)
    tables = _rope_tables(n_ctx, n_tokens, HEAD_DIM) + _rope_tables(n_ctx, n_tokens, DIFF_DQK)

    for l in range(depth):
        need_ctx = l < depth - 1
        lam_init = 0.8 - 0.6 * math.exp(-0.3 * l)
        mod = mod_all[l]

        w_in_b = jnp.concatenate(
            [w_in[l][:, :ORIG_LR], w_in[l][:, ORIG_LR + 2 * GLA_GATE_RANK:],
             w_in[l][:, ORIG_LR:ORIG_LR + 2 * GLA_GATE_RANK],
             jnp.zeros((d, LANES - 2 * GLA_GATE_RANK), F32)], axis=1).astype(BF16)
        wa = jnp.zeros((LANES, 2 * GLA_QK), F32)
        wa = wa.at[:GLA_GATE_RANK, :GLA_QK].set(gla_wa_f[l])
        wa = wa.at[GLA_GATE_RANK:2 * GLA_GATE_RANK, GLA_QK:].set(gla_wa_b[l]).astype(BF16)
        ba = jnp.concatenate([gla_ba_f[l], gla_ba_b[l]]).reshape(1, -1)
        lamv = jnp.zeros((8, LANES), F32)
        for r, vec in enumerate((diff_lq1[l], diff_lk1[l], diff_lq2[l], diff_lk2[l])):
            lamv = lamv.at[r, :DIFF_DQK].set(vec)
        rw = jnp.pad(router_w[l], ((0, 0), (0, LANES - n_experts))).astype(BF16)
        rb = jnp.pad(router_b[l], (0, LANES - n_experts)).reshape(1, LANES)

        (glaq, glak, glav, gate, la, q_t, k, v_t, dq_t, dk, dv_t) = _input_projection(
            s, mod, norm1_g[l], w_in_b, wa, ba, q_norm_g[l], k_norm_g[l], tables)
        o_gla = _gla(glaq, glak, glav, la)
        gqa_parts = [_gqa_attention(q_t, k, v_t, latent=True)]
        diff_parts = [_diff_attention(dq_t, dk, dv_t, lamv, diff_norm_g[l], lam_init, latent=True)]
        if need_ctx:
            gqa_parts.insert(0, _gqa_attention(q_t, k, v_t, latent=False))
            diff_parts.insert(0, _diff_attention(dq_t, dk, dv_t, lamv, diff_norm_g[l], lam_init, latent=False))
        else:
            gqa_parts.insert(0, jnp.zeros((n_ctx, GQA_Q), BF16))
            diff_parts.insert(0, jnp.zeros((n_ctx, DIFF_V), BF16))
        o_gqa = jnp.concatenate(gqa_parts, axis=0)
        o_diff = jnp.concatenate(diff_parts, axis=0)
        s = _output_projection(s, mod, o_gla, gate, o_gqa, o_diff, gla_norm_g[l], w_out[l].astype(BF16))
        h2, eid, rank, w_route, counts_f = _router(s, mod, norm2_g[l], rw, rb, n_experts)
        dest, plan = _routing_plan(eid, rank, counts_f, n_experts, n_rows * TOP_K // EXPERT_ROW_TILE)
        xs = _dispatch(h2, _tile_major(dest, ROW_TILE))
        ys = _grouped_experts(xs, l, exp_w_gate, exp_w_up, exp_w_down, plan)
        s = _combine(ys, _tile_major(dest, COMBINE_ROW_TILE), w_route, h2, s, mod,
                     sh_w_gate[l].astype(BF16), sh_w_up[l].astype(BF16), sh_w_down[l].astype(BF16),
                     final_g, n_ctx, last_layer=l == depth - 1)

    return s.reshape(batch, n_tokens, d)
```

```python
import functools
import math

import jax
import jax.numpy as jnp
from jax import lax
from jax.experimental import pallas as pl
from jax.experimental.pallas import tpu as pltpu

F32 = jnp.float32
BF16 = jnp.bfloat16

GRID_W = 64
HEAD_DIM = 128
GLA_HEADS = 4
GLA_DK = 64
GLA_DV = 128
GLA_GATE_RANK = 16
GLA_TAU = 16.0
GLA_CHUNK = 64
GQA_HEADS = 8
GQA_KV_HEADS = 2
DIFF_HEADS = 4
DIFF_DQK = 64
DIFF_DV = 128
TOP_K = 8
ROUTE_SCALE = 2.5
ROPE_THETA = 10000.0
NORM_EPS = 1e-6
LOG2_E = math.log2(math.e)

LANES = 128
ROW_TILE = 256
EXPERT_ROW_TILE = 256
COMBINE_ROW_TILE = 256
WEIGHT_DMA_PRIORITY = 1
VMEM_LIMIT = 56 * 1024 * 1024

GLA_QK = GLA_HEADS * GLA_DK
GLA_V = GLA_HEADS * GLA_DV
GQA_Q = GQA_HEADS * HEAD_DIM
GQA_KV = GQA_KV_HEADS * HEAD_DIM
DIFF_QK = DIFF_HEADS * 2 * DIFF_DQK
DIFF_V = DIFF_HEADS * DIFF_DV
MIX_WIDTH = GLA_V + GQA_Q + DIFF_V

C_GLAQ = 0
C_GLAK = C_GLAQ + GLA_QK
C_GLAV = C_GLAK + GLA_QK
C_GATE = C_GLAV + GLA_V
C_GQAQ = C_GATE + GLA_V
C_GQAK = C_GQAQ + GQA_Q
C_GQAV = C_GQAK + GQA_KV
C_DQ = C_GQAV + GQA_KV
C_DK = C_DQ + DIFF_QK
C_DV = C_DK + DIFF_QK
C_LR = C_DV + DIFF_V
IN_COLS = C_LR + LANES
ORIG_LR = 2 * GLA_QK + 2 * GLA_V


def _params(*sem):
    return pltpu.CompilerParams(dimension_semantics=sem, vmem_limit_bytes=VMEM_LIMIT)


def _resident(shape):
    nd = len(shape)
    return pl.BlockSpec(shape, lambda *_: (0,) * nd, pipeline_mode=pl.Buffered(1))


def _silu(a):
    return a / (1.0 + jnp.exp(-a))


def _rms(x, g):
    return x * lax.rsqrt(jnp.mean(x * x, axis=-1, keepdims=True) + NORM_EPS) * g


def _row_group(i):
    return jnp.where(i == 0, 1, 0)


def _mod_kernel(a_ref, w_ref, b_ref, o_ref):
    a = _silu(a_ref[...])
    o_ref[...] = jnp.dot(a.astype(BF16), w_ref[...].astype(BF16),
                         preferred_element_type=F32) + b_ref[...]


def _modulation(c, c_ctx, w_mod, b_mod):
    depth, d, six_d = w_mod.shape
    a = jnp.zeros((8, d), F32).at[0].set(c[0]).at[1].set(c_ctx)
    tn = d // 2
    out = pl.pallas_call(
        _mod_kernel,
        grid=(depth, six_d // tn),
        in_specs=[pl.BlockSpec((8, d), lambda l, j: (0, 0)),
                  pl.BlockSpec((None, d, tn), lambda l, j: (l, 0, j)),
                  pl.BlockSpec((None, 1, tn), lambda l, j: (l, 0, j))],
        out_specs=pl.BlockSpec((None, 8, tn), lambda l, j: (l, 0, j)),
        out_shape=jax.ShapeDtypeStruct((depth, 8, six_d), F32),
        compiler_params=_params("parallel", "parallel"),
        name="modulation",
    )(a, w_mod, b_mod.reshape(depth, 1, six_d))
    m = out[:, :2].reshape(depth, 2, 6, d)
    return jnp.pad(m, ((0, 0), (0, 0), (0, 2), (0, 0)))


def _rope(xh, cos, sin, first, shift_first, shift_second):
    partner = jnp.where(first, pltpu.roll(xh, shift_first, 1), pltpu.roll(xh, shift_second, 1))
    return xh * cos + partner * sin


def _inproj_kernel(x_ref, mod_ref, g_ref, w_ref, wa_ref, ba_ref, qg_ref, kg_ref,
                   cg_ref, sg_ref, cd_ref, sd_ref,
                   glaq_ref, glak_ref, glav_ref, gate_ref, la_ref,
                   qt_ref, k_ref, vt_ref, dqt_ref, dk_ref, dvt_ref):
    x = x_ref[...]
    h = _rms(x, g_ref[...]) * (1.0 + mod_ref[1:2, :]) + mod_ref[0:1, :]
    hb = h.astype(BF16)

    def proj(start, width):
        return jnp.dot(hb, w_ref[:, start:start + width], preferred_element_type=F32)

    glaq_ref[...] = proj(C_GLAQ, GLA_QK) * (GLA_DK ** -0.5)
    glak_ref[...] = proj(C_GLAK, GLA_QK)
    glav_ref[...] = proj(C_GLAV, GLA_V)
    gate_ref[...] = proj(C_GATE, GLA_V)
    zv = proj(C_GQAV, GQA_KV)
    for hd in range(GQA_KV_HEADS):
        sl = slice(hd * HEAD_DIM, (hd + 1) * HEAD_DIM)
        vt_ref[sl, :] = zv[:, sl].T.astype(BF16)
    zdv = proj(C_DV, DIFF_V)
    for hd in range(DIFF_HEADS):
        sl = slice(hd * DIFF_DV, (hd + 1) * DIFF_DV)
        dvt_ref[sl, :] = zdv[:, sl].T.astype(BF16)

    z_lr = proj(C_LR, LANES).astype(BF16)
    pre = jnp.dot(z_lr, wa_ref[...], preferred_element_type=F32) + ba_ref[...]
    log_sig = -(jnp.maximum(-pre, 0.0) + jnp.log1p(jnp.exp(-jnp.abs(pre))))
    la = log_sig * (1.0 / GLA_TAU)
    la_ref[0] = la[:, :GLA_QK]
    la_ref[1] = la[:, GLA_QK:]

    rows = x.shape[0]
    lane = lax.broadcasted_iota(jnp.int32, (rows, LANES), 1)
    first_g = (lane % 64) < 32
    first_d = (lane % 32) < 16
    cg, sg, cd, sd = cg_ref[...], sg_ref[...], cd_ref[...], sd_ref[...]
    scale_g = HEAD_DIM ** -0.5 * LOG2_E
    scale_d = DIFF_DQK ** -0.5 * LOG2_E

    zq = proj(C_GQAQ, GQA_Q)
    for hd in range(GQA_HEADS):
        sl = slice(hd * HEAD_DIM, (hd + 1) * HEAD_DIM)
        qh = _rope(_rms(zq[:, sl], qg_ref[...]), cg, sg, first_g, 96, 32) * scale_g
        qt_ref[sl, :] = qh.T.astype(BF16)
    zk = proj(C_GQAK, GQA_KV)
    for hd in range(GQA_KV_HEADS):
        kh = _rms(zk[:, hd * HEAD_DIM:(hd + 1) * HEAD_DIM], kg_ref[...])
        k_ref[:, hd * HEAD_DIM:(hd + 1) * HEAD_DIM] = _rope(kh, cg, sg, first_g, 96, 32).astype(BF16)
    zdq = proj(C_DQ, DIFF_QK)
    zdk = proj(C_DK, DIFF_QK)
    for hd in range(DIFF_HEADS):
        sl = slice(hd * LANES, (hd + 1) * LANES)
        dqt_ref[sl, :] = (_rope(zdq[:, sl], cd, sd, first_d, 112, 16) * scale_d).T.astype(BF16)
        dk_ref[:, sl] = _rope(zdk[:, sl], cd, sd, first_d, 112, 16).astype(BF16)


def _input_projection(s, mod, norm_g, w_in_b, wa, ba, q_norm_g, k_norm_g, tables):
    n, d = s.shape
    nt = n // ROW_TILE
    row = lambda w: pl.BlockSpec((ROW_TILE, w), lambda i: (i, 0))
    col = lambda w: pl.BlockSpec((w, ROW_TILE), lambda i: (0, i))
    tile_t = lambda w: pl.BlockSpec((None, w, ROW_TILE), lambda i: (i, 0, 0))
    f32o = lambda w: jax.ShapeDtypeStruct((n, w), F32)
    bfo = lambda w: jax.ShapeDtypeStruct((n, w), BF16)
    return pl.pallas_call(
        _inproj_kernel,
        grid=(nt,),
        in_specs=[row(d),
                  pl.BlockSpec((None, 8, d), lambda i: (_row_group(i), 0, 0)),
                  _resident((1, d)),
                  _resident((d, IN_COLS)),
                  _resident((LANES, 2 * GLA_QK)),
                  _resident((1, 2 * GLA_QK)),
                  _resident((1, HEAD_DIM)),
                  _resident((1, HEAD_DIM)),
                  row(LANES), row(LANES), row(LANES), row(LANES)],
        out_specs=[row(GLA_QK), row(GLA_QK), row(GLA_V), row(GLA_V),
                   pl.BlockSpec((2, ROW_TILE, GLA_QK), lambda i: (0, i, 0)),
                   col(GQA_Q), row(GQA_KV), tile_t(GQA_KV), col(DIFF_QK), row(DIFF_QK), tile_t(DIFF_V)],
        out_shape=[f32o(GLA_QK), f32o(GLA_QK), f32o(GLA_V), f32o(GLA_V),
                   jax.ShapeDtypeStruct((2, n, GLA_QK), F32),
                   jax.ShapeDtypeStruct((GQA_Q, n), BF16), bfo(GQA_KV),
                   jax.ShapeDtypeStruct((nt, GQA_KV, ROW_TILE), BF16),
                   jax.ShapeDtypeStruct((DIFF_QK, n), BF16), bfo(DIFF_QK),
                   jax.ShapeDtypeStruct((nt, DIFF_V, ROW_TILE), BF16)],
        compiler_params=_params("parallel"),
        name="input_projection",
    )(s, mod, norm_g.reshape(1, d), w_in_b, wa, ba, q_norm_g.reshape(1, -1), k_norm_g.reshape(1, -1),
      *tables)


def _gla_block(reverse, q_ref, k_ref, v_ref, la_ref, o_ref, st_ref):
    rows = q_ref.shape[0]
    n_chunks = rows // GLA_CHUNK
    ri = lax.broadcasted_iota(jnp.int32, (rows, rows), 0)
    ci = lax.broadcasted_iota(jnp.int32, (rows, rows), 1)
    same_chunk = (ri // GLA_CHUNK) == (ci // GLA_CHUNK)
    allowed = same_chunk & ((ci >= ri) if reverse else (ci <= ri))
    la = la_ref[...]
    cum = jnp.dot(jnp.where(allowed, 1.0, 0.0).astype(F32), la,
                  preferred_element_type=F32, precision=lax.Precision.HIGHEST)
    tot = jnp.dot(jnp.where(same_chunk, 1.0, 0.0).astype(F32), la,
                  preferred_element_type=F32, precision=lax.Precision.HIGHEST)
    k = k_ref[...]
    q_dec = q_ref[...] * jnp.exp(cum)
    k_inv = (k * jnp.exp(-cum)).astype(BF16)
    k_end = k * jnp.exp(tot - cum)
    v = v_ref[...]
    vb = v.astype(BF16)
    v_t = v.T.astype(BF16)
    lane_head = lax.broadcasted_iota(jnp.int32, (rows, GLA_QK), 1) // GLA_DK
    row_chunk = lax.broadcasted_iota(jnp.int32, (rows, GLA_QK), 0) // GLA_CHUNK
    nt_dims = (((1,), (1,)), ((), ()))

    for hd in range(GLA_HEADS):
        qh = jnp.where(lane_head == hd, q_dec, 0.0).astype(BF16)
        a = lax.dot_general(qh, k_inv, nt_dims, preferred_element_type=F32)
        a = jnp.where(allowed, a, 0.0).astype(BF16)
        o_ref[:, hd * GLA_DV:(hd + 1) * GLA_DV] = jnp.dot(
            a, vb[:, hd * GLA_DV:(hd + 1) * GLA_DV], preferred_element_type=F32)

    chunk_order = range(n_chunks - 1, -1, -1) if reverse else range(n_chunks)
    chunk_lane_head = lax.broadcasted_iota(jnp.int32, (GLA_CHUNK, GLA_QK), 1) // GLA_DK
    for c in chunk_order:
        sl = slice(c * GLA_CHUNK, (c + 1) * GLA_CHUNK)
        state = st_ref[...]
        q4 = jnp.concatenate(
            [jnp.where(chunk_lane_head == hd, q_dec[sl], 0.0) for hd in range(GLA_HEADS)],
            axis=0).astype(BF16)
        r = lax.dot_general(q4, state.astype(BF16), nt_dims, preferred_element_type=F32)
        for hd in range(GLA_HEADS):
            o_ref[sl, hd * GLA_DV:(hd + 1) * GLA_DV] += r[hd * GLA_CHUNK:(hd + 1) * GLA_CHUNK,
                                                          hd * GLA_DV:(hd + 1) * GLA_DV]
        k_end_c = jnp.where(row_chunk == c, k_end, 0.0).astype(BF16)
        u_t = jnp.dot(v_t, k_end_c, preferred_element_type=F32)
        st_ref[...] = state * jnp.exp(tot[c * GLA_CHUNK:c * GLA_CHUNK + 1]) + u_t


def _gla_kernel(q_ref, k_ref, v_ref, la_ref, o_ref, st_ref):
    @pl.when(pl.program_id(1) == 0)
    def _():
        st_ref[...] = jnp.zeros_like(st_ref)

    @pl.when(pl.program_id(0) == 0)
    def _():
        _gla_block(False, q_ref, k_ref, v_ref, la_ref, o_ref, st_ref)

    @pl.when(pl.program_id(0) == 1)
    def _():
        _gla_block(True, q_ref, k_ref, v_ref, la_ref, o_ref, st_ref)


def _gla(glaq, glak, glav, la):
    n = glaq.shape[0]
    nt = n // ROW_TILE

    def blk(dr, j):
        return jnp.where(dr == 0, j, jnp.where(j == 0, 0, nt - j))

    row = lambda w: pl.BlockSpec((ROW_TILE, w), lambda dr, j: (blk(dr, j), 0))
    return pl.pallas_call(
        _gla_kernel,
        grid=(2, nt),
        in_specs=[row(GLA_QK), row(GLA_QK), row(GLA_V),
                  pl.BlockSpec((None, ROW_TILE, GLA_QK), lambda dr, j: (dr, blk(dr, j), 0))],
        out_specs=pl.BlockSpec((None, ROW_TILE, GLA_V), lambda dr, j: (dr, blk(dr, j), 0)),
        out_shape=jax.ShapeDtypeStruct((2, n, GLA_V), F32),
        scratch_shapes=[pltpu.VMEM((GLA_V, GLA_QK), F32)],
        compiler_params=_params("arbitrary", "arbitrary"),
        name="gla_scan",
    )(glaq, glak, glav, la)


class _FlashMaps:
    def __init__(self, q_maps, k_of_map, load_k, load_vt):
        self.q_maps, self.k_of_map, self.load_k, self.load_vt = q_maps, k_of_map, load_k, load_vt


def _flash_steps(maps, first_tile, tiles_per_step, n_steps, s_ref, m_ref, l_ref, acc_ref):
    n_maps = len(maps.q_maps)
    rows = tiles_per_step * ROW_TILE
    for u in range(n_steps):
        first_row = pl.multiple_of((first_tile + u * tiles_per_step) * ROW_TILE, ROW_TILE)
        keys = {src: maps.load_k(src, first_row, rows) for src in sorted(set(maps.k_of_map))}
        for j in range(n_maps):
            s_ref[u * n_maps + j, :rows] = jnp.dot(keys[maps.k_of_map[j]], maps.q_maps[j],
                                                   preferred_element_type=F32)
    for u in range(n_steps):
        tile = first_tile + u * tiles_per_step
        for j in range(n_maps):
            s_t = s_ref[u * n_maps + j, :rows]
            m_prev = m_ref[j]
            m_new = jnp.maximum(m_prev, jnp.max(s_t, axis=0, keepdims=True))
            alpha = jnp.exp2(m_prev - m_new)
            p = jnp.exp2(s_t - m_new)
            l_ref[j] = alpha * l_ref[j] + jnp.sum(p, axis=0, keepdims=True)
            pb = p.astype(BF16)
            pv = jnp.dot(maps.load_vt(maps.k_of_map[j], tile), pb[:ROW_TILE], preferred_element_type=F32)
            for r in range(1, tiles_per_step):
                pv += jnp.dot(maps.load_vt(maps.k_of_map[j], tile + r), pb[r * ROW_TILE:(r + 1) * ROW_TILE],
                              preferred_element_type=F32)
            acc_ref[j] = alpha * acc_ref[j] + pv
            m_ref[j] = m_new


def _flash_attend(maps, latent, n_key_tiles, tiles_per_step, n_steps, s_ref, m_ref, l_ref, acc_ref):
    m_ref[...] = jnp.full(m_ref.shape, -jnp.inf, F32)
    l_ref[...] = jnp.zeros(l_ref.shape, F32)
    acc_ref[...] = jnp.zeros(acc_ref.shape, F32)
    _flash_steps(maps, 0, 1, 1, s_ref, m_ref, l_ref, acc_ref)
    if not latent:
        return
    per_iter = tiles_per_step * n_steps

    def body(it, carry):
        _flash_steps(maps, 1 + it * per_iter, tiles_per_step, n_steps, s_ref, m_ref, l_ref, acc_ref)
        return carry

    lax.fori_loop(0, (n_key_tiles - 1) // per_iter, body, 0)


MAX_Q_SUBTILES = 4


def _query_subtiles(n_key_tiles, latent):
    n_sub = 1
    while latent and 2 * n_sub <= MAX_Q_SUBTILES and (n_key_tiles - 1) % (2 * n_sub) == 0:
        n_sub *= 2
    return n_sub


def _query_specs(width, n_sub, latent):
    if not latent:
        return [pl.BlockSpec((width, ROW_TILE), lambda g, i: (g, 0))]
    return [pl.BlockSpec((width, ROW_TILE), lambda g, i, sub=sub: (g, 1 + n_sub * i + sub))
            for sub in range(n_sub)]


def _query_steps(n_key_tiles, n_sub, latent):
    return (n_key_tiles - 1) // n_sub if latent else 1


def _flash_plan(n_key_tiles):
    latent = n_key_tiles - 1
    tiles_per_step = 2 if latent % 2 == 0 else 1
    n_steps = 2 if latent % (2 * tiles_per_step) == 0 else 1
    return tiles_per_step, n_steps


def _flash_scratch(n_maps, dv, plan):
    tiles_per_step, n_steps = plan
    return [pltpu.VMEM((n_steps * n_maps, tiles_per_step * ROW_TILE, ROW_TILE), F32),
            pltpu.VMEM((n_maps, 1, ROW_TILE), F32), pltpu.VMEM((n_maps, 1, ROW_TILE), F32),
            pltpu.VMEM((n_maps, dv, ROW_TILE), F32)]


GQA_GROUP = GQA_HEADS // GQA_KV_HEADS


def _gqa_kernel(latent, n_sub, plan, *refs):
    q_refs = refs[:n_sub]
    k_ref, vt_ref, o_ref, s_ref, m_ref, l_ref, acc_ref = refs[n_sub:]
    maps = _FlashMaps(
        q_maps=[q[hd * HEAD_DIM:(hd + 1) * HEAD_DIM, :] for q in q_refs for hd in range(GQA_GROUP)],
        k_of_map=[0] * (n_sub * GQA_GROUP),
        load_k=lambda src, first_row, rows: k_ref[pl.ds(first_row, rows), :],
        load_vt=lambda src, tile: vt_ref[tile])
    _flash_attend(maps, latent, vt_ref.shape[0], *plan, s_ref, m_ref, l_ref, acc_ref)
    for sub in range(n_sub):
        for hd in range(GQA_GROUP):
            j = sub * GQA_GROUP + hd
            o_t = acc_ref[j] / l_ref[j]
            o_ref[sub * ROW_TILE:(sub + 1) * ROW_TILE, hd * HEAD_DIM:(hd + 1) * HEAD_DIM] = (
                o_t.T.astype(o_ref.dtype))


def _gqa_attention(q_t, k, v_t, latent):
    n = k.shape[0]
    nt = n // ROW_TILE
    gw = GQA_GROUP * HEAD_DIM
    plan = _flash_plan(nt)
    n_sub = _query_subtiles(nt, latent)
    q_specs = _query_specs(gw, n_sub, latent)
    steps = _query_steps(nt, n_sub, latent)
    rows = len(q_specs) * ROW_TILE
    return pl.pallas_call(
        functools.partial(_gqa_kernel, latent, n_sub, plan),
        grid=(GQA_KV_HEADS, steps),
        in_specs=q_specs + [pl.BlockSpec((n, HEAD_DIM), lambda g, i: (0, g)),
                            pl.BlockSpec((nt, HEAD_DIM, ROW_TILE), lambda g, i: (0, g, 0))],
        out_specs=pl.BlockSpec((rows, gw), lambda g, i: (i, g)),
        out_shape=jax.ShapeDtypeStruct((steps * rows, GQA_Q), BF16),
        scratch_shapes=_flash_scratch(len(q_specs) * GQA_GROUP, HEAD_DIM, plan),
        compiler_params=_params("parallel", "arbitrary"),
        name="gqa_attention",
    )(*([q_t] * len(q_specs)), k, v_t)


DIFF_PAIR = 2


def _diff_kernel(latent, n_sub, plan, lam_init, *refs):
    q_refs = refs[:n_sub]
    k_ref, vt_ref, lamv_ref, g_ref, o_ref, s_ref, m_ref, l_ref, acc_ref = refs[n_sub:]
    row = lax.broadcasted_iota(jnp.int32, (LANES, ROW_TILE), 0)
    q_maps = []
    for q_ref in q_refs:
        for hd in range(DIFF_PAIR):
            q_t = q_ref[hd * LANES:(hd + 1) * LANES, :]
            zero = jnp.zeros_like(q_t)
            q_maps += [jnp.where(row < DIFF_DQK, q_t, zero), jnp.where(row >= DIFF_DQK, q_t, zero)]
    maps = _FlashMaps(
        q_maps=q_maps,
        k_of_map=[hd for _ in range(n_sub) for hd in range(DIFF_PAIR) for _ in range(2)],
        load_k=lambda src, first_row, rows: k_ref[pl.ds(first_row, rows), src * LANES:(src + 1) * LANES],
        load_vt=lambda src, tile: vt_ref[tile, src * DIFF_DV:(src + 1) * DIFF_DV, :])
    _flash_attend(maps, latent, vt_ref.shape[0], *plan, s_ref, m_ref, l_ref, acc_ref)
    lv = lamv_ref[...]
    lam = (jnp.exp(jnp.sum(lv[0:1] * lv[1:2], axis=-1, keepdims=True))
           - jnp.exp(jnp.sum(lv[2:3] * lv[3:4], axis=-1, keepdims=True)) + lam_init)
    for sub in range(n_sub):
        for hd in range(DIFF_PAIR):
            j = 2 * (sub * DIFF_PAIR + hd)
            o_t = acc_ref[j] / l_ref[j] - lam * (acc_ref[j + 1] / l_ref[j + 1])
            o = _rms(o_t.T, g_ref[...]) * (1.0 - lam_init)
            o_ref[sub * ROW_TILE:(sub + 1) * ROW_TILE, hd * DIFF_DV:(hd + 1) * DIFF_DV] = o.astype(o_ref.dtype)


def _diff_attention(q_t, k, v_t, lamv, norm_g, lam_init, latent):
    n = k.shape[0]
    nt = n // ROW_TILE
    pw = DIFF_PAIR * LANES
    plan = _flash_plan(nt)
    n_sub = _query_subtiles(nt, latent)
    q_specs = _query_specs(pw, n_sub, latent)
    steps = _query_steps(nt, n_sub, latent)
    rows = len(q_specs) * ROW_TILE
    return pl.pallas_call(
        functools.partial(_diff_kernel, latent, n_sub, plan, lam_init),
        grid=(DIFF_HEADS // DIFF_PAIR, steps),
        in_specs=q_specs + [pl.BlockSpec((n, pw), lambda h, i: (0, h)),
                            pl.BlockSpec((nt, DIFF_PAIR * DIFF_DV, ROW_TILE), lambda h, i: (0, h, 0)),
                            pl.BlockSpec((8, LANES), lambda h, i: (0, 0)),
                            pl.BlockSpec((1, DIFF_DV), lambda h, i: (0, 0))],
        out_specs=pl.BlockSpec((rows, DIFF_PAIR * DIFF_DV), lambda h, i: (i, h)),
        out_shape=jax.ShapeDtypeStruct((steps * rows, DIFF_V), BF16),
        scratch_shapes=_flash_scratch(2 * DIFF_PAIR * len(q_specs), DIFF_DV, plan),
        compiler_params=_params("parallel", "arbitrary"),
        name="diff_attention",
    )(*([q_t] * len(q_specs)), k, v_t, lamv, norm_g.reshape(1, -1))


def _outproj_kernel(s_ref, mod_ref, og_ref, gate_ref, oq_ref, od_ref, gg_ref, w_ref, o_ref):
    og = og_ref[0] + og_ref[1]
    gate = gate_ref[...]
    y = jnp.zeros(s_ref.shape, F32)
    for hd in range(GLA_HEADS):
        sl = slice(hd * GLA_DV, (hd + 1) * GLA_DV)
        oh = _rms(og[:, sl], gg_ref[...]) * _silu(gate[:, sl])
        y += jnp.dot(oh.astype(BF16), w_ref[hd * GLA_DV:(hd + 1) * GLA_DV, :], preferred_element_type=F32)
    y += jnp.dot(oq_ref[...], w_ref[GLA_V:GLA_V + GQA_Q, :], preferred_element_type=F32)
    y += jnp.dot(od_ref[...], w_ref[GLA_V + GQA_Q:, :], preferred_element_type=F32)
    o_ref[...] = s_ref[...] + mod_ref[2:3, :] * y


def _output_projection(s, mod, o_gla, gate, o_gqa, o_diff, gla_norm_g, w_out_b):
    n, d = s.shape
    nt = n // ROW_TILE
    row = lambda w: pl.BlockSpec((ROW_TILE, w), lambda i: (i, 0))
    return pl.pallas_call(
        _outproj_kernel,
        grid=(nt,),
        in_specs=[row(d),
                  pl.BlockSpec((None, 8, d), lambda i: (_row_group(i), 0, 0)),
                  pl.BlockSpec((2, ROW_TILE, GLA_V), lambda i: (0, i, 0)),
                  row(GLA_V), row(GQA_Q), row(DIFF_V),
                  _resident((1, GLA_DV)),
                  _resident((MIX_WIDTH, d))],
        out_specs=row(d),
        out_shape=jax.ShapeDtypeStruct((n, d), F32),
        compiler_params=_params("parallel"),
        name="output_projection",
    )(s, mod, o_gla, gate, o_gqa, o_diff, gla_norm_g.reshape(1, -1), w_out_b)


def _router_kernel(n_experts, s_ref, mod_ref, g_ref, rw_ref, rb_ref,
                   h_ref, eid_ref, rank_ref, w_ref, count_ref, carry_ref):
    @pl.when(pl.program_id(0) == 0)
    def _():
        carry_ref[...] = jnp.zeros_like(carry_ref)

    h = _rms(s_ref[...], g_ref[...]) * (1.0 + mod_ref[4:5, :]) + mod_ref[3:4, :]
    h_ref[...] = h
    logits = jnp.dot(h.astype(BF16), rw_ref[...], preferred_element_type=F32)
    scores = 1.0 / (1.0 + jnp.exp(-logits))
    rows = scores.shape[0]
    lane = lax.broadcasted_iota(jnp.int32, scores.shape, 1)
    lane_f = lane.astype(F32)
    cand = jnp.where(lane < n_experts, scores + rb_ref[...], -jnp.inf)
    hits = []
    for _ in range(TOP_K):
        best = jnp.max(cand, axis=-1, keepdims=True)
        first = jnp.min(jnp.where(cand == best, lane_f, float(LANES)), axis=-1, keepdims=True)
        hit = lane_f == first
        hits.append(hit)
        cand = jnp.where(hit, -jnp.inf, cand)
    chosen = functools.reduce(jnp.logical_or, hits)
    total = jnp.sum(jnp.where(chosen, scores, 0.0), axis=-1, keepdims=True)
    ri = lax.broadcasted_iota(jnp.int32, (rows, rows), 0)
    ci = lax.broadcasted_iota(jnp.int32, (rows, rows), 1)
    chosen_b = jnp.where(chosen, 1.0, 0.0).astype(BF16)
    before = jnp.dot(jnp.where(ci < ri, 1.0, 0.0).astype(BF16), chosen_b, preferred_element_type=F32)
    rank_all = before + carry_ref[...]
    for k, hit in enumerate(hits):
        pick = lambda a: jnp.sum(jnp.where(hit, a, 0.0), axis=-1, keepdims=True)
        eid_ref[:, k:k + 1] = pick(lane_f).astype(jnp.int32)
        rank_ref[:, k:k + 1] = pick(rank_all).astype(jnp.int32)
        w_ref[:, k:k + 1] = pick(scores) / total * ROUTE_SCALE
    carry_ref[...] += jnp.sum(chosen_b.astype(F32), axis=0, keepdims=True)
    count_ref[...] = carry_ref[...]


def _router(s, mod, norm_g, router_w_b, router_b, n_experts):
    n, d = s.shape
    nt = n // ROW_TILE
    row = lambda w: pl.BlockSpec((ROW_TILE, w), lambda i: (i, 0))
    return pl.pallas_call(
        functools.partial(_router_kernel, n_experts),
        grid=(nt,),
        in_specs=[row(d),
                  pl.BlockSpec((None, 8, d), lambda i: (_row_group(i), 0, 0)),
                  _resident((1, d)),
                  _resident((d, LANES)),
                  _resident((1, LANES))],
        out_specs=[row(d), row(TOP_K), row(TOP_K), row(TOP_K), pl.BlockSpec((1, LANES), lambda i: (0, 0))],
        out_shape=[jax.ShapeDtypeStruct((n, d), F32), jax.ShapeDtypeStruct((n, TOP_K), jnp.int32),
                   jax.ShapeDtypeStruct((n, TOP_K), jnp.int32), jax.ShapeDtypeStruct((n, TOP_K), F32),
                   jax.ShapeDtypeStruct((1, LANES), F32)],
        scratch_shapes=[pltpu.VMEM((1, LANES), F32)],
        compiler_params=_params("arbitrary"),
        name="router",
    )(s, mod, norm_g.reshape(1, d), router_w_b, router_b)


def _routing_plan(eid, rank, counts_f, n_experts, n_tiles):
    counts = counts_f[0, :n_experts].astype(jnp.int32)
    starts = jnp.concatenate([jnp.zeros((1,), jnp.int32), jnp.cumsum(counts)])
    experts = jnp.arange(n_experts, dtype=jnp.int32)
    dest = jnp.sum(jnp.where(eid[..., None] == experts, starts[:-1], 0), axis=-1) + rank
    first_tile = starts[:-1] // EXPERT_ROW_TILE
    last_tile = (starts[1:] - 1) // EXPERT_ROW_TILE
    visits_per_expert = jnp.where(counts > 0, last_tile - first_tile + 1, 0)
    visit_end = jnp.cumsum(visits_per_expert)
    visit_start = visit_end - visits_per_expert
    n_visits = visit_end[-1]
    v = jnp.minimum(jnp.arange(n_tiles + n_experts - 1, dtype=jnp.int32), n_visits - 1)
    visit_expert = jnp.sum((v[:, None] >= visit_end[None, :]).astype(jnp.int32), axis=-1)
    own = visit_expert[:, None] == experts[None, :]
    visit_tile = jnp.sum(jnp.where(own, (first_tile - visit_start)[None, :], 0), axis=-1) + v
    visited = visits_per_expert > 0
    later = jnp.logical_and(experts[None, :] > experts[:, None], visited[None, :])
    next_visited = jnp.min(jnp.where(later, experts[None, :], n_experts), axis=-1)
    next_visited = jnp.where(next_visited == n_experts, -1, next_visited)
    visit_next = jnp.sum(jnp.where(own, next_visited[None, :], 0), axis=-1)
    visit_first = jnp.sum(jnp.where(own, visit_start[None, :], 0), axis=-1) == v
    order = jnp.cumsum(visited.astype(jnp.int32)) - 1
    visit_slot = jnp.sum(jnp.where(own, order[None, :], 0), axis=-1) % 2
    plan = (visit_tile, visit_expert, starts, n_visits.reshape(1),
            visit_first.astype(jnp.int32), visit_slot, visit_next)
    return dest, plan


def _dispatch_kernel(dest_ref, h_ref, xs_hbm, sem):
    def body(j, carry):
        for k in range(TOP_K):
            pltpu.make_async_copy(h_ref.at[pl.ds(j, 1)], xs_hbm.at[pl.ds(dest_ref[k, j], 1)], sem).start()
        return carry

    lax.fori_loop(0, ROW_TILE, body, 0)
    all_rows = xs_hbm.at[pl.ds(0, ROW_TILE * TOP_K)]
    pltpu.make_async_copy(all_rows, all_rows, sem).wait()


def _dispatch(h, dest_tiles):
    n, d = h.shape
    nt = n // ROW_TILE
    return pl.pallas_call(
        _dispatch_kernel,
        grid=(nt,),
        in_specs=[pl.BlockSpec((None, TOP_K, ROW_TILE), lambda i: (i, 0, 0), memory_space=pltpu.SMEM),
                  pl.BlockSpec((ROW_TILE, d), lambda i: (i, 0))],
        out_specs=pl.BlockSpec(memory_space=pl.ANY),
        out_shape=jax.ShapeDtypeStruct((n * TOP_K, d), F32),
        scratch_shapes=[pltpu.SemaphoreType.DMA(())],
        compiler_params=_params("arbitrary"),
        name="moe_dispatch",
    )(dest_tiles, h)


def _grouped_kernel(layer, vt_ref, ve_ref, starts_ref, nv_ref, first_ref, slot_ref, next_ref,
                    x_ref, wg_hbm, wu_hbm, wd_hbm, y_ref,
                    wgb_ref, wub_ref, wdb_ref, sg_ref, su_ref, sd_ref, wsem):
    v = pl.program_id(0)
    e = ve_ref[v]
    t = vt_ref[v]
    prev = jnp.maximum(v - 1, 0)

    def weight_copies(expert, slot):
        return [pltpu.make_async_copy(src.at[layer, expert], dst.at[slot], wsem.at[slot])
                for src, dst in ((wg_hbm, sg_ref), (wu_hbm, su_ref), (wd_hbm, sd_ref))]

    @pl.when(v == 0)
    def _():
        for cp in weight_copies(e, 0):
            cp.start(priority=WEIGHT_DMA_PRIORITY)

    @pl.when(jnp.logical_and(v < nv_ref[0], first_ref[v] == 1))
    def _():
        slot = slot_ref[v]
        for cp in weight_copies(e, slot):
            cp.wait()
        wgb_ref[...] = sg_ref[slot].astype(BF16)
        wub_ref[...] = su_ref[slot].astype(BF16)
        wdb_ref[...] = sd_ref[slot].astype(BF16)
        nxt = next_ref[v]

        @pl.when(nxt >= 0)
        def _():
            for cp in weight_copies(nxt, 1 - slot):
                cp.start(priority=WEIGHT_DMA_PRIORITY)

    @pl.when(v < nv_ref[0])
    def _():
        x = x_ref[...].astype(BF16)
        a = jnp.dot(x, wgb_ref[...], preferred_element_type=F32)
        b = jnp.dot(x, wub_ref[...], preferred_element_type=F32)
        y = jnp.dot((_silu(a) * b).astype(BF16), wdb_ref[...], preferred_element_type=F32)
        rows = x.shape[0]
        r = t * rows + lax.broadcasted_iota(jnp.int32, (rows, 1), 0)
        y = jnp.where(jnp.logical_and(r >= starts_ref[e], r < starts_ref[e + 1]), y, 0.0)
        first_visit_of_tile = jnp.logical_or(v == 0, vt_ref[prev] != t)

        @pl.when(first_visit_of_tile)
        def _():
            y_ref[...] = y

        @pl.when(jnp.logical_not(first_visit_of_tile))
        def _():
            y_ref[...] += y


def _grouped_experts(xs, layer, w_gate, w_up, w_down, plan):
    p, d = xs.shape
    hidden = w_gate.shape[-1]
    tm = EXPERT_ROW_TILE
    tile = lambda v, vt, *_: (vt[v], 0)
    grid_spec = pltpu.PrefetchScalarGridSpec(
        num_scalar_prefetch=len(plan),
        grid=(plan[0].shape[0],),
        in_specs=[pl.BlockSpec((tm, d), tile),
                  pl.BlockSpec(memory_space=pl.ANY), pl.BlockSpec(memory_space=pl.ANY),
                  pl.BlockSpec(memory_space=pl.ANY)],
        out_specs=pl.BlockSpec((tm, d), tile),
        scratch_shapes=[pltpu.VMEM((d, hidden), BF16), pltpu.VMEM((d, hidden), BF16),
                        pltpu.VMEM((hidden, d), BF16),
                        pltpu.VMEM((2, d, hidden), F32), pltpu.VMEM((2, d, hidden), F32),
                        pltpu.VMEM((2, hidden, d), F32), pltpu.SemaphoreType.DMA((2,))])
    return pl.pallas_call(
        functools.partial(_grouped_kernel, layer),
        grid_spec=grid_spec,
        out_shape=jax.ShapeDtypeStruct((p, d), F32),
        compiler_params=_params("arbitrary"),
        name="moe_grouped_experts",
    )(*plan, xs, w_gate, w_up, w_down)


def _combine_kernel(n_ctx_rows, first_tile, final_norm, dest_ref, dest_next_ref, w_ref, h_ref, s_ref,
                    mod_ref, sg_ref, su_ref, sd_ref, fg_ref, ys_hbm, o_ref, ybuf_ref, sems):
    i = pl.program_id(0)
    n_steps = pl.num_programs(0)
    rows = h_ref.shape[0]
    slot = i % 2

    def gather(dref, to_slot):
        def body(j, carry):
            for k in range(TOP_K):
                pltpu.make_async_copy(ys_hbm.at[pl.ds(dref[k, j], 1)],
                                      ybuf_ref.at[to_slot, k, pl.ds(j, 1)], sems.at[to_slot]).start()
            return carry
        lax.fori_loop(0, rows, body, 0)

    @pl.when(i == 0)
    def _():
        gather(dest_ref, 0)

    @pl.when(i + 1 < n_steps)
    def _():
        gather(dest_next_ref, 1 - slot)

    hb = h_ref[...].astype(BF16)
    a = jnp.dot(hb, sg_ref[...], preferred_element_type=F32)
    b = jnp.dot(hb, su_ref[...], preferred_element_type=F32)
    y = jnp.dot((_silu(a) * b).astype(BF16), sd_ref[...], preferred_element_type=F32)

    pltpu.make_async_copy(ybuf_ref.at[slot], ybuf_ref.at[slot], sems.at[slot]).wait()
    w = w_ref[...]
    for k in range(TOP_K):
        y += w[:, k:k + 1] * ybuf_ref[slot, k]
    r = (first_tile + i) * rows + lax.broadcasted_iota(jnp.int32, (rows, 1), 0)
    gate2 = jnp.where(r < n_ctx_rows, mod_ref[1, 5:6, :], mod_ref[0, 5:6, :])
    out = s_ref[...] + gate2 * y
    o_ref[...] = _rms(out, fg_ref[...]) if final_norm else out


def _combine(ys, dest_tiles, w, h, s, mod, sh_gate_b, sh_up_b, sh_down_b, final_g, n_ctx_rows, last_layer):
    n, d = s.shape
    tc = COMBINE_ROW_TILE
    first_tile = n_ctx_rows // tc if last_layer else 0
    steps = n // tc - first_tile
    hidden = sh_gate_b.shape[-1]
    row = lambda width: pl.BlockSpec((tc, width), lambda i: (i + first_tile, 0))
    idx = lambda shift: pl.BlockSpec(
        (None, TOP_K, tc), lambda i: (jnp.minimum(i + first_tile + shift, n // tc - 1), 0, 0),
        memory_space=pltpu.SMEM)
    return pl.pallas_call(
        functools.partial(_combine_kernel, n_ctx_rows, first_tile, last_layer),
        grid=(steps,),
        in_specs=[idx(0), idx(1), row(TOP_K), row(d), row(d),
                  pl.BlockSpec((2, 8, d), lambda i: (0, 0, 0)),
                  _resident((d, hidden)), _resident((d, hidden)), _resident((hidden, d)),
                  _resident((1, d)),
                  pl.BlockSpec(memory_space=pl.ANY)],
        out_specs=pl.BlockSpec((tc, d), lambda i: (i, 0)),
        out_shape=jax.ShapeDtypeStruct((steps * tc, d), F32),
        scratch_shapes=[pltpu.VMEM((2, TOP_K, tc, d), F32), pltpu.SemaphoreType.DMA((2,))],
        compiler_params=_params("arbitrary"),
        name="moe_combine",
    )(dest_tiles, dest_tiles, w, h, s, mod, sh_gate_b, sh_up_b, sh_down_b, final_g.reshape(1, d), ys)


def _tile_major(dest, tile):
    n, k = dest.shape
    return dest.reshape(n // tile, tile, k).transpose(0, 2, 1)


def _rope_tables(n_ctx, n_tokens, dim):
    half = dim // 2
    inv_freq = ROPE_THETA ** (-jnp.arange(0, half, 2, dtype=F32) / half)
    t = jnp.arange(n_tokens, dtype=jnp.int32)
    ang_r = (t // GRID_W).astype(F32)[:, None] * inv_freq
    ang_c = (t % GRID_W).astype(F32)[:, None] * inv_freq
    cos = jnp.concatenate([jnp.cos(ang_r)] * 2 + [jnp.cos(ang_c)] * 2, axis=-1)
    sin = jnp.concatenate([-jnp.sin(ang_r), jnp.sin(ang_r), -jnp.sin(ang_c), jnp.sin(ang_c)], axis=-1)
    reps = LANES // dim
    cos = jnp.tile(cos, (1, reps))
    sin = jnp.tile(sin, (1, reps))
    cos = jnp.concatenate([jnp.ones((n_ctx, LANES), F32), cos], axis=0)
    sin = jnp.concatenate([jnp.zeros((n_ctx, LANES), F32), sin], axis=0)
    return cos, sin


def kernel(x, c, ctx, c_ctx, norm1_g, norm2_g, w_mod, b_mod, w_in, gla_wa_f, gla_ba_f, gla_wa_b,
           gla_ba_b, gla_norm_g, q_norm_g, k_norm_g, diff_lq1, diff_lk1, diff_lq2, diff_lk2,
           diff_norm_g, w_out, router_w, router_b, exp_w_gate, exp_w_up, exp_w_down,
           sh_w_gate, sh_w_up, sh_w_down, final_g):
    batch, n_tokens, d = x.shape
    n_ctx = ctx.shape[1]
    depth = w_mod.shape[0]
    n_experts = router_w.shape[-1]
    n_rows = n_ctx + n_tokens
    assert batch == 1 and n_ctx == ROW_TILE and n_tokens % ROW_TILE == 0
    assert TOP_K <= n_experts <= LANES and EXPERT_ROW_TILE <= ROW_TILE

    s = jnp.concatenate([ctx[0], x[0]], axis=0)
    mod_all = _modulation(c, c_ctx, w_mod, b_mod)
    tables = _rope_tables(n_ctx, n_tokens, HEAD_DIM) + _rope_tables(n_ctx, n_tokens, DIFF_DQK)

    for l in range(depth):
        need_ctx = l < depth - 1
        lam_init = 0.8 - 0.6 * math.exp(-0.3 * l)
        mod = mod_all[l]

        w_in_b = jnp.concatenate(
            [w_in[l][:, :ORIG_LR], w_in[l][:, ORIG_LR + 2 * GLA_GATE_RANK:],
             w_in[l][:, ORIG_LR:ORIG_LR + 2 * GLA_GATE_RANK],
             jnp.zeros((d, LANES - 2 * GLA_GATE_RANK), F32)], axis=1).astype(BF16)
        wa = jnp.zeros((LANES, 2 * GLA_QK), F32)
        wa = wa.at[:GLA_GATE_RANK, :GLA_QK].set(gla_wa_f[l])
        wa = wa.at[GLA_GATE_RANK:2 * GLA_GATE_RANK, GLA_QK:].set(gla_wa_b[l]).astype(BF16)
        ba = jnp.concatenate([gla_ba_f[l], gla_ba_b[l]]).reshape(1, -1)
        lamv = jnp.zeros((8, LANES), F32)
        for r, vec in enumerate((diff_lq1[l], diff_lk1[l], diff_lq2[l], diff_lk2[l])):
            lamv = lamv.at[r, :DIFF_DQK].set(vec)
        rw = jnp.pad(router_w[l], ((0, 0), (0, LANES - n_experts))).astype(BF16)
        rb = jnp.pad(router_b[l], (0, LANES - n_experts)).reshape(1, LANES)

        (glaq, glak, glav, gate, la, q_t, k, v_t, dq_t, dk, dv_t) = _input_projection(
            s, mod, norm1_g[l], w_in_b, wa, ba, q_norm_g[l], k_norm_g[l], tables)
        o_gla = _gla(glaq, glak, glav, la)
        gqa_parts = [_gqa_attention(q_t, k, v_t, latent=True)]
        diff_parts = [_diff_attention(dq_t, dk, dv_t, lamv, diff_norm_g[l], lam_init, latent=True)]
        if need_ctx:
            gqa_parts.insert(0, _gqa_attention(q_t, k, v_t, latent=False))
            diff_parts.insert(0, _diff_attention(dq_t, dk, dv_t, lamv, diff_norm_g[l], lam_init, latent=False))
        else:
            gqa_parts.insert(0, jnp.zeros((n_ctx, GQA_Q), BF16))
            diff_parts.insert(0, jnp.zeros((n_ctx, DIFF_V), BF16))
        o_gqa = jnp.concatenate(gqa_parts, axis=0)
        o_diff = jnp.concatenate(diff_parts, axis=0)
        s = _output_projection(s, mod, o_gla, gate, o_gqa, o_diff, gla_norm_g[l], w_out[l].astype(BF16))
        h2, eid, rank, w_route, counts_f = _router(s, mod, norm2_g[l], rw, rb, n_experts)
        dest, plan = _routing_plan(eid, rank, counts_f, n_experts, n_rows * TOP_K // EXPERT_ROW_TILE)
        xs = _dispatch(h2, _tile_major(dest, ROW_TILE))
        ys = _grouped_experts(xs, l, exp_w_gate, exp_w_up, exp_w_down, plan)
        s = _combine(ys, _tile_major(dest, COMBINE_ROW_TILE), w_route, h2, s, mod,
                     sh_w_gate[l].astype(BF16), sh_w_up[l].astype(BF16), sh_w_down[l].astype(BF16),
                     final_g, n_ctx, last_layer=l == depth - 1)

    return s.reshape(batch, n_tokens, d)
```

```python
import functools
import math

import jax
import jax.numpy as jnp
from jax import lax
from jax.experimental import pallas as pl
from jax.experimental.pallas import tpu as pltpu

F32 = jnp.float32
BF16 = jnp.bfloat16

GRID_W = 64
HEAD_DIM = 128
GLA_HEADS = 4
GLA_DK = 64
GLA_DV = 128
GLA_GATE_RANK = 16
GLA_TAU = 16.0
GLA_CHUNK = 64
GQA_HEADS = 8
GQA_KV_HEADS = 2
DIFF_HEADS = 4
DIFF_DQK = 64
DIFF_DV = 128
TOP_K = 8
ROUTE_SCALE = 2.5
ROPE_THETA = 10000.0
NORM_EPS = 1e-6
LOG2_E = math.log2(math.e)

LANES = 128
ROW_TILE = 256
EXPERT_ROW_TILE = 256
COMBINE_ROW_TILE = 256
WEIGHT_DMA_PRIORITY = 1
VMEM_LIMIT = 56 * 1024 * 1024

GLA_QK = GLA_HEADS * GLA_DK
GLA_V = GLA_HEADS * GLA_DV
GQA_Q = GQA_HEADS * HEAD_DIM
GQA_KV = GQA_KV_HEADS * HEAD_DIM
DIFF_QK = DIFF_HEADS * 2 * DIFF_DQK
DIFF_V = DIFF_HEADS * DIFF_DV
MIX_WIDTH = GLA_V + GQA_Q + DIFF_V
VT_PAD = 16
VT_ROWS = HEAD_DIM + VT_PAD

C_GLAQ = 0
C_GLAK = C_GLAQ + GLA_QK
C_GLAV = C_GLAK + GLA_QK
C_GATE = C_GLAV + GLA_V
C_GQAQ = C_GATE + GLA_V
C_GQAK = C_GQAQ + GQA_Q
C_GQAV = C_GQAK + GQA_KV
C_DQ = C_GQAV + GQA_KV
C_DK = C_DQ + DIFF_QK
C_DV = C_DK + DIFF_QK
C_LR = C_DV + DIFF_V
IN_COLS = C_LR + LANES
ORIG_LR = 2 * GLA_QK + 2 * GLA_V


def _params(*sem):
    return pltpu.CompilerParams(dimension_semantics=sem, vmem_limit_bytes=VMEM_LIMIT)


def _resident(shape):
    nd = len(shape)
    return pl.BlockSpec(shape, lambda *_: (0,) * nd, pipeline_mode=pl.Buffered(1))


def _silu(a):
    return a / (1.0 + jnp.exp(-a))


def _rms(x, g):
    return x * lax.rsqrt(jnp.mean(x * x, axis=-1, keepdims=True) + NORM_EPS) * g


def _row_group(i):
    return jnp.where(i == 0, 1, 0)


def _mod_kernel(a_ref, w_ref, b_ref, o_ref):
    a = _silu(a_ref[...])
    o_ref[...] = jnp.dot(a.astype(BF16), w_ref[...].astype(BF16),
                         preferred_element_type=F32) + b_ref[...]


def _modulation(c, c_ctx, w_mod, b_mod):
    depth, d, six_d = w_mod.shape
    a = jnp.zeros((8, d), F32).at[0].set(c[0]).at[1].set(c_ctx)
    tn = d // 2
    out = pl.pallas_call(
        _mod_kernel,
        grid=(depth, six_d // tn),
        in_specs=[pl.BlockSpec((8, d), lambda l, j: (0, 0)),
                  pl.BlockSpec((None, d, tn), lambda l, j: (l, 0, j)),
                  pl.BlockSpec((None, 1, tn), lambda l, j: (l, 0, j))],
        out_specs=pl.BlockSpec((None, 8, tn), lambda l, j: (l, 0, j)),
        out_shape=jax.ShapeDtypeStruct((depth, 8, six_d), F32),
        compiler_params=_params("parallel", "parallel"),
        name="modulation",
    )(a, w_mod, b_mod.reshape(depth, 1, six_d))
    m = out[:, :2].reshape(depth, 2, 6, d)
    return jnp.pad(m, ((0, 0), (0, 0), (0, 2), (0, 0)))


def _rope(xh, cos, sin, first, shift_first, shift_second):
    partner = jnp.where(first, pltpu.roll(xh, shift_first, 1), pltpu.roll(xh, shift_second, 1))
    return xh * cos + partner * sin


def _inproj_kernel(x_ref, mod_ref, g_ref, w_ref, wa_ref, ba_ref, qg_ref, kg_ref,
                   cg_ref, sg_ref, cd_ref, sd_ref,
                   glaq_ref, glak_ref, glav_ref, gate_ref, la_ref,
                   qt_ref, k_ref, vt_ref, dqt_ref, dk_ref, dvt_ref):
    x = x_ref[...]
    h = _rms(x, g_ref[...]) * (1.0 + mod_ref[1:2, :]) + mod_ref[0:1, :]
    hb = h.astype(BF16)

    def proj(start, width):
        return jnp.dot(hb, w_ref[:, start:start + width], preferred_element_type=F32)

    glaq_ref[...] = proj(C_GLAQ, GLA_QK) * (GLA_DK ** -0.5)
    glak_ref[...] = proj(C_GLAK, GLA_QK)
    glav_ref[...] = proj(C_GLAV, GLA_V)
    gate_ref[...] = proj(C_GATE, GLA_V)
    ones_row = jnp.where(lax.broadcasted_iota(jnp.int32, (VT_PAD, x.shape[0]), 0) == 0, 1.0, 0.0).astype(BF16)
    zv = proj(C_GQAV, GQA_KV)
    for hd in range(GQA_KV_HEADS):
        vt_ref[hd * VT_ROWS:hd * VT_ROWS + HEAD_DIM, :] = zv[:, hd * HEAD_DIM:(hd + 1) * HEAD_DIM].T.astype(BF16)
        vt_ref[hd * VT_ROWS + HEAD_DIM:(hd + 1) * VT_ROWS, :] = ones_row
    zdv = proj(C_DV, DIFF_V)
    for hd in range(DIFF_HEADS):
        dvt_ref[hd * VT_ROWS:hd * VT_ROWS + DIFF_DV, :] = zdv[:, hd * DIFF_DV:(hd + 1) * DIFF_DV].T.astype(BF16)
        dvt_ref[hd * VT_ROWS + DIFF_DV:(hd + 1) * VT_ROWS, :] = ones_row

    z_lr = proj(C_LR, LANES).astype(BF16)
    pre = jnp.dot(z_lr, wa_ref[...], preferred_element_type=F32) + ba_ref[...]
    log_sig = -(jnp.maximum(-pre, 0.0) + jnp.log1p(jnp.exp(-jnp.abs(pre))))
    la = log_sig * (1.0 / GLA_TAU)
    la_ref[0] = la[:, :GLA_QK]
    la_ref[1] = la[:, GLA_QK:]

    rows = x.shape[0]
    lane = lax.broadcasted_iota(jnp.int32, (rows, LANES), 1)
    first_g = (lane % 64) < 32
    first_d = (lane % 32) < 16
    cg, sg, cd, sd = cg_ref[...], sg_ref[...], cd_ref[...], sd_ref[...]
    scale_g = HEAD_DIM ** -0.5 * LOG2_E
    scale_d = DIFF_DQK ** -0.5 * LOG2_E

    zq = proj(C_GQAQ, GQA_Q)
    for hd in range(GQA_HEADS):
        sl = slice(hd * HEAD_DIM, (hd + 1) * HEAD_DIM)
        qh = _rope(_rms(zq[:, sl], qg_ref[...]), cg, sg, first_g, 96, 32) * scale_g
        qt_ref[sl, :] = qh.T.astype(BF16)
    zk = proj(C_GQAK, GQA_KV)
    for hd in range(GQA_KV_HEADS):
        kh = _rms(zk[:, hd * HEAD_DIM:(hd + 1) * HEAD_DIM], kg_ref[...])
        k_ref[:, hd * HEAD_DIM:(hd + 1) * HEAD_DIM] = _rope(kh, cg, sg, first_g, 96, 32).astype(BF16)
    zdq = proj(C_DQ, DIFF_QK)
    zdk = proj(C_DK, DIFF_QK)
    for hd in range(DIFF_HEADS):
        sl = slice(hd * LANES, (hd + 1) * LANES)
        dqt_ref[sl, :] = (_rope(zdq[:, sl], cd, sd, first_d, 112, 16) * scale_d).T.astype(BF16)
        dk_ref[:, sl] = _rope(zdk[:, sl], cd, sd, first_d, 112, 16).astype(BF16)


def _input_projection(s, mod, norm_g, w_in_b, wa, ba, q_norm_g, k_norm_g, tables):
    n, d = s.shape
    nt = n // ROW_TILE
    row = lambda w: pl.BlockSpec((ROW_TILE, w), lambda i: (i, 0))
    col = lambda w: pl.BlockSpec((w, ROW_TILE), lambda i: (0, i))
    tile_t = lambda w: pl.BlockSpec((None, w, ROW_TILE), lambda i: (i, 0, 0))
    f32o = lambda w: jax.ShapeDtypeStruct((n, w), F32)
    bfo = lambda w: jax.ShapeDtypeStruct((n, w), BF16)
    return pl.pallas_call(
        _inproj_kernel,
        grid=(nt,),
        in_specs=[row(d),
                  pl.BlockSpec((None, 8, d), lambda i: (_row_group(i), 0, 0)),
                  _resident((1, d)),
                  _resident((d, IN_COLS)),
                  _resident((LANES, 2 * GLA_QK)),
                  _resident((1, 2 * GLA_QK)),
                  _resident((1, HEAD_DIM)),
                  _resident((1, HEAD_DIM)),
                  row(LANES), row(LANES), row(LANES), row(LANES)],
        out_specs=[row(GLA_QK), row(GLA_QK), row(GLA_V), row(GLA_V),
                   pl.BlockSpec((2, ROW_TILE, GLA_QK), lambda i: (0, i, 0)),
                   col(GQA_Q), row(GQA_KV), tile_t(GQA_KV_HEADS * VT_ROWS),
                   col(DIFF_QK), row(DIFF_QK), tile_t(DIFF_HEADS * VT_ROWS)],
        out_shape=[f32o(GLA_QK), f32o(GLA_QK), f32o(GLA_V), f32o(GLA_V),
                   jax.ShapeDtypeStruct((2, n, GLA_QK), F32),
                   jax.ShapeDtypeStruct((GQA_Q, n), BF16), bfo(GQA_KV),
                   jax.ShapeDtypeStruct((nt, GQA_KV_HEADS * VT_ROWS, ROW_TILE), BF16),
                   jax.ShapeDtypeStruct((DIFF_QK, n), BF16), bfo(DIFF_QK),
                   jax.ShapeDtypeStruct((nt, DIFF_HEADS * VT_ROWS, ROW_TILE), BF16)],
        compiler_params=_params("parallel"),
        name="input_projection",
    )(s, mod, norm_g.reshape(1, d), w_in_b, wa, ba, q_norm_g.reshape(1, -1), k_norm_g.reshape(1, -1),
      *tables)


def _gla_block(reverse, q_ref, k_ref, v_ref, la_ref, o_ref, st_ref):
    rows = q_ref.shape[0]
    n_chunks = rows // GLA_CHUNK
    ri = lax.broadcasted_iota(jnp.int32, (rows, rows), 0)
    ci = lax.broadcasted_iota(jnp.int32, (rows, rows), 1)
    same_chunk = (ri // GLA_CHUNK) == (ci // GLA_CHUNK)
    allowed = same_chunk & ((ci >= ri) if reverse else (ci <= ri))
    la = la_ref[...]
    cum = jnp.dot(jnp.where(allowed, 1.0, 0.0).astype(F32), la,
                  preferred_element_type=F32, precision=lax.Precision.HIGHEST)
    tot = jnp.dot(jnp.where(same_chunk, 1.0, 0.0).astype(F32), la,
                  preferred_element_type=F32, precision=lax.Precision.HIGHEST)
    k = k_ref[...]
    q_dec = q_ref[...] * jnp.exp(cum)
    k_inv = (k * jnp.exp(-cum)).astype(BF16)
    k_end = k * jnp.exp(tot - cum)
    v = v_ref[...]
    vb = v.astype(BF16)
    v_t = v.T.astype(BF16)
    lane_head = lax.broadcasted_iota(jnp.int32, (rows, GLA_QK), 1) // GLA_DK
    row_chunk = lax.broadcasted_iota(jnp.int32, (rows, GLA_QK), 0) // GLA_CHUNK
    nt_dims = (((1,), (1,)), ((), ()))

    for hd in range(GLA_HEADS):
        qh = jnp.where(lane_head == hd, q_dec, 0.0).astype(BF16)
        a = lax.dot_general(qh, k_inv, nt_dims, preferred_element_type=F32)
        a = jnp.where(allowed, a, 0.0).astype(BF16)
        o_ref[:, hd * GLA_DV:(hd + 1) * GLA_DV] = jnp.dot(
            a, vb[:, hd * GLA_DV:(hd + 1) * GLA_DV], preferred_element_type=F32)

    chunk_order = range(n_chunks - 1, -1, -1) if reverse else range(n_chunks)
    chunk_lane_head = lax.broadcasted_iota(jnp.int32, (GLA_CHUNK, GLA_QK), 1) // GLA_DK
    for c in chunk_order:
        sl = slice(c * GLA_CHUNK, (c + 1) * GLA_CHUNK)
        state = st_ref[...]
        q4 = jnp.concatenate(
            [jnp.where(chunk_lane_head == hd, q_dec[sl], 0.0) for hd in range(GLA_HEADS)],
            axis=0).astype(BF16)
        r = lax.dot_general(q4, state.astype(BF16), nt_dims, preferred_element_type=F32)
        for hd in range(GLA_HEADS):
            o_ref[sl, hd * GLA_DV:(hd + 1) * GLA_DV] += r[hd * GLA_CHUNK:(hd + 1) * GLA_CHUNK,
                                                          hd * GLA_DV:(hd + 1) * GLA_DV]
        k_end_c = jnp.where(row_chunk == c, k_end, 0.0).astype(BF16)
        u_t = jnp.dot(v_t, k_end_c, preferred_element_type=F32)
        st_ref[...] = state * jnp.exp(tot[c * GLA_CHUNK:c * GLA_CHUNK + 1]) + u_t


def _gla_kernel(q_ref, k_ref, v_ref, la_ref, o_ref, st_ref):
    @pl.when(pl.program_id(1) == 0)
    def _():
        st_ref[...] = jnp.zeros_like(st_ref)

    @pl.when(pl.program_id(0) == 0)
    def _():
        _gla_block(False, q_ref, k_ref, v_ref, la_ref, o_ref, st_ref)

    @pl.when(pl.program_id(0) == 1)
    def _():
        _gla_block(True, q_ref, k_ref, v_ref, la_ref, o_ref, st_ref)


def _gla(glaq, glak, glav, la):
    n = glaq.shape[0]
    nt = n // ROW_TILE

    def blk(dr, j):
        return jnp.where(dr == 0, j, jnp.where(j == 0, 0, nt - j))

    row = lambda w: pl.BlockSpec((ROW_TILE, w), lambda dr, j: (blk(dr, j), 0))
    return pl.pallas_call(
        _gla_kernel,
        grid=(2, nt),
        in_specs=[row(GLA_QK), row(GLA_QK), row(GLA_V),
                  pl.BlockSpec((None, ROW_TILE, GLA_QK), lambda dr, j: (dr, blk(dr, j), 0))],
        out_specs=pl.BlockSpec((None, ROW_TILE, GLA_V), lambda dr, j: (dr, blk(dr, j), 0)),
        out_shape=jax.ShapeDtypeStruct((2, n, GLA_V), F32),
        scratch_shapes=[pltpu.VMEM((GLA_V, GLA_QK), F32)],
        compiler_params=_params("arbitrary", "arbitrary"),
        name="gla_scan",
    )(glaq, glak, glav, la)


class _FlashMaps:
    def __init__(self, q_maps, k_of_map, load_k, load_vt):
        self.q_maps, self.k_of_map, self.load_k, self.load_vt = q_maps, k_of_map, load_k, load_vt


def _flash_steps(maps, first_tile, tiles_per_step, n_steps, s_ref, m_ref, acc_ref):
    n_maps = len(maps.q_maps)
    rows = tiles_per_step * ROW_TILE
    for u in range(n_steps):
        first_row = pl.multiple_of((first_tile + u * tiles_per_step) * ROW_TILE, ROW_TILE)
        keys = {src: maps.load_k(src, first_row, rows) for src in sorted(set(maps.k_of_map))}
        for j in range(n_maps):
            s_ref[u * n_maps + j, :rows] = jnp.dot(keys[maps.k_of_map[j]], maps.q_maps[j],
                                                   preferred_element_type=F32)
    for u in range(n_steps):
        tile = first_tile + u * tiles_per_step
        for j in range(n_maps):
            s_t = s_ref[u * n_maps + j, :rows]
            m_prev = m_ref[j]
            m_new = jnp.maximum(m_prev, jnp.max(s_t, axis=0, keepdims=True))
            alpha = jnp.exp2(m_prev - m_new)
            pb = jnp.exp2(s_t - m_new).astype(BF16)
            pv = jnp.dot(maps.load_vt(maps.k_of_map[j], tile), pb[:ROW_TILE], preferred_element_type=F32)
            for r in range(1, tiles_per_step):
                pv += jnp.dot(maps.load_vt(maps.k_of_map[j], tile + r), pb[r * ROW_TILE:(r + 1) * ROW_TILE],
                              preferred_element_type=F32)
            acc_ref[j] = alpha * acc_ref[j] + pv
            m_ref[j] = m_new


def _flash_attend(maps, latent, n_key_tiles, tiles_per_step, n_steps, s_ref, m_ref, acc_ref):
    m_ref[...] = jnp.full(m_ref.shape, -jnp.inf, F32)
    acc_ref[...] = jnp.zeros(acc_ref.shape, F32)
    _flash_steps(maps, 0, 1, 1, s_ref, m_ref, acc_ref)
    if not latent:
        return
    per_iter = tiles_per_step * n_steps

    def body(it, carry):
        _flash_steps(maps, 1 + it * per_iter, tiles_per_step, n_steps, s_ref, m_ref, acc_ref)
        return carry

    lax.fori_loop(0, (n_key_tiles - 1) // per_iter, body, 0)


MAX_Q_SUBTILES = 4


def _query_subtiles(n_key_tiles, latent):
    n_sub = 1
    while latent and 2 * n_sub <= MAX_Q_SUBTILES and (n_key_tiles - 1) % (2 * n_sub) == 0:
        n_sub *= 2
    return n_sub


def _query_specs(width, n_sub, latent):
    if not latent:
        return [pl.BlockSpec((width, ROW_TILE), lambda g, i: (g, 0))]
    return [pl.BlockSpec((width, ROW_TILE), lambda g, i, sub=sub: (g, 1 + n_sub * i + sub))
            for sub in range(n_sub)]


def _query_steps(n_key_tiles, n_sub, latent):
    return (n_key_tiles - 1) // n_sub if latent else 1


def _flash_plan(n_key_tiles):
    latent = n_key_tiles - 1
    tiles_per_step = 2 if latent % 2 == 0 else 1
    n_steps = 2 if latent % (2 * tiles_per_step) == 0 else 1
    return tiles_per_step, n_steps


def _flash_scratch(n_maps, dv, plan):
    tiles_per_step, n_steps = plan
    return [pltpu.VMEM((n_steps * n_maps, tiles_per_step * ROW_TILE, ROW_TILE), F32),
            pltpu.VMEM((n_maps, 1, ROW_TILE), F32),
            pltpu.VMEM((n_maps, dv + VT_PAD, ROW_TILE), F32)]


GQA_GROUP = GQA_HEADS // GQA_KV_HEADS


def _gqa_kernel(latent, n_sub, plan, *refs):
    q_refs = refs[:n_sub]
    k_ref, vt_ref, o_ref, s_ref, m_ref, acc_ref = refs[n_sub:]
    maps = _FlashMaps(
        q_maps=[q[hd * HEAD_DIM:(hd + 1) * HEAD_DIM, :] for q in q_refs for hd in range(GQA_GROUP)],
        k_of_map=[0] * (n_sub * GQA_GROUP),
        load_k=lambda src, first_row, rows: k_ref[pl.ds(first_row, rows), :],
        load_vt=lambda src, tile: vt_ref[tile])
    _flash_attend(maps, latent, vt_ref.shape[0], *plan, s_ref, m_ref, acc_ref)
    for sub in range(n_sub):
        for hd in range(GQA_GROUP):
            j = sub * GQA_GROUP + hd
            o_t = acc_ref[j, :HEAD_DIM] / acc_ref[j, HEAD_DIM:HEAD_DIM + 1]
            o_ref[sub * ROW_TILE:(sub + 1) * ROW_TILE, hd * HEAD_DIM:(hd + 1) * HEAD_DIM] = (
                o_t.T.astype(o_ref.dtype))


def _gqa_attention(q_t, k, v_t, latent):
    n = k.shape[0]
    nt = n // ROW_TILE
    gw = GQA_GROUP * HEAD_DIM
    plan = _flash_plan(nt)
    n_sub = _query_subtiles(nt, latent)
    q_specs = _query_specs(gw, n_sub, latent)
    steps = _query_steps(nt, n_sub, latent)
    rows = len(q_specs) * ROW_TILE
    return pl.pallas_call(
        functools.partial(_gqa_kernel, latent, n_sub, plan),
        grid=(GQA_KV_HEADS, steps),
        in_specs=q_specs + [pl.BlockSpec((n, HEAD_DIM), lambda g, i: (0, g)),
                            pl.BlockSpec((nt, VT_ROWS, ROW_TILE), lambda g, i: (0, g, 0))],
        out_specs=pl.BlockSpec((rows, gw), lambda g, i: (i, g)),
        out_shape=jax.ShapeDtypeStruct((steps * rows, GQA_Q), BF16),
        scratch_shapes=_flash_scratch(len(q_specs) * GQA_GROUP, HEAD_DIM, plan),
        compiler_params=_params("parallel", "arbitrary"),
        name="gqa_attention",
    )(*([q_t] * len(q_specs)), k, v_t)


DIFF_PAIR = 2


def _diff_kernel(latent, n_sub, plan, lam_init, *refs):
    q_refs = refs[:n_sub]
    k_ref, vt_ref, lamv_ref, g_ref, o_ref, s_ref, m_ref, acc_ref = refs[n_sub:]
    row = lax.broadcasted_iota(jnp.int32, (LANES, ROW_TILE), 0)
    q_maps = []
    for q_ref in q_refs:
        for hd in range(DIFF_PAIR):
            q_t = q_ref[hd * LANES:(hd + 1) * LANES, :]
            zero = jnp.zeros_like(q_t)
            q_maps += [jnp.where(row < DIFF_DQK, q_t, zero), jnp.where(row >= DIFF_DQK, q_t, zero)]
    maps = _FlashMaps(
        q_maps=q_maps,
        k_of_map=[hd for _ in range(n_sub) for hd in range(DIFF_PAIR) for _ in range(2)],
        load_k=lambda src, first_row, rows: k_ref[pl.ds(first_row, rows), src * LANES:(src + 1) * LANES],
        load_vt=lambda src, tile: vt_ref[tile, src * VT_ROWS:(src + 1) * VT_ROWS, :])
    _flash_attend(maps, latent, vt_ref.shape[0], *plan, s_ref, m_ref, acc_ref)
    lv = lamv_ref[...]
    lam = (jnp.exp(jnp.sum(lv[0:1] * lv[1:2], axis=-1, keepdims=True))
           - jnp.exp(jnp.sum(lv[2:3] * lv[3:4], axis=-1, keepdims=True)) + lam_init)
    for sub in range(n_sub):
        for hd in range(DIFF_PAIR):
            j = 2 * (sub * DIFF_PAIR + hd)
            softmax_v = lambda m: acc_ref[m, :DIFF_DV] / acc_ref[m, DIFF_DV:DIFF_DV + 1]
            o_t = softmax_v(j) - lam * softmax_v(j + 1)
            o = _rms(o_t.T, g_ref[...]) * (1.0 - lam_init)
            o_ref[sub * ROW_TILE:(sub + 1) * ROW_TILE, hd * DIFF_DV:(hd + 1) * DIFF_DV] = o.astype(o_ref.dtype)


def _diff_attention(q_t, k, v_t, lamv, norm_g, lam_init, latent):
    n = k.shape[0]
    nt = n // ROW_TILE
    pw = DIFF_PAIR * LANES
    plan = _flash_plan(nt)
    n_sub = _query_subtiles(nt, latent)
    q_specs = _query_specs(pw, n_sub, latent)
    steps = _query_steps(nt, n_sub, latent)
    rows = len(q_specs) * ROW_TILE
    return pl.pallas_call(
        functools.partial(_diff_kernel, latent, n_sub, plan, lam_init),
        grid=(DIFF_HEADS // DIFF_PAIR, steps),
        in_specs=q_specs + [pl.BlockSpec((n, pw), lambda h, i: (0, h)),
                            pl.BlockSpec((nt, DIFF_PAIR * VT_ROWS, ROW_TILE), lambda h, i: (0, h, 0)),
                            pl.BlockSpec((8, LANES), lambda h, i: (0, 0)),
                            pl.BlockSpec((1, DIFF_DV), lambda h, i: (0, 0))],
        out_specs=pl.BlockSpec((rows, DIFF_PAIR * DIFF_DV), lambda h, i: (i, h)),
        out_shape=jax.ShapeDtypeStruct((steps * rows, DIFF_V), BF16),
        scratch_shapes=_flash_scratch(2 * DIFF_PAIR * len(q_specs), DIFF_DV, plan),
        compiler_params=_params("parallel", "arbitrary"),
        name="diff_attention",
    )(*([q_t] * len(q_specs)), k, v_t, lamv, norm_g.reshape(1, -1))


def _outproj_kernel(s_ref, mod_ref, og_ref, gate_ref, oq_ref, od_ref, gg_ref, w_ref, o_ref):
    og = og_ref[0] + og_ref[1]
    gate = gate_ref[...]
    y = jnp.zeros(s_ref.shape, F32)
    for hd in range(GLA_HEADS):
        sl = slice(hd * GLA_DV, (hd + 1) * GLA_DV)
        oh = _rms(og[:, sl], gg_ref[...]) * _silu(gate[:, sl])
        y += jnp.dot(oh.astype(BF16), w_ref[hd * GLA_DV:(hd + 1) * GLA_DV, :], preferred_element_type=F32)
    y += jnp.dot(oq_ref[...], w_ref[GLA_V:GLA_V + GQA_Q, :], preferred_element_type=F32)
    y += jnp.dot(od_ref[...], w_ref[GLA_V + GQA_Q:, :], preferred_element_type=F32)
    o_ref[...] = s_ref[...] + mod_ref[2:3, :] * y


def _output_projection(s, mod, o_gla, gate, o_gqa, o_diff, gla_norm_g, w_out_b):
    n, d = s.shape
    nt = n // ROW_TILE
    row = lambda w: pl.BlockSpec((ROW_TILE, w), lambda i: (i, 0))
    return pl.pallas_call(
        _outproj_kernel,
        grid=(nt,),
        in_specs=[row(d),
                  pl.BlockSpec((None, 8, d), lambda i: (_row_group(i), 0, 0)),
                  pl.BlockSpec((2, ROW_TILE, GLA_V), lambda i: (0, i, 0)),
                  row(GLA_V), row(GQA_Q), row(DIFF_V),
                  _resident((1, GLA_DV)),
                  _resident((MIX_WIDTH, d))],
        out_specs=row(d),
        out_shape=jax.ShapeDtypeStruct((n, d), F32),
        compiler_params=_params("parallel"),
        name="output_projection",
    )(s, mod, o_gla, gate, o_gqa, o_diff, gla_norm_g.reshape(1, -1), w_out_b)


def _router_kernel(n_experts, s_ref, mod_ref, g_ref, rw_ref, rb_ref,
                   h_ref, eid_ref, rank_ref, w_ref, count_ref, carry_ref):
    @pl.when(pl.program_id(0) == 0)
    def _():
        carry_ref[...] = jnp.zeros_like(carry_ref)

    h = _rms(s_ref[...], g_ref[...]) * (1.0 + mod_ref[4:5, :]) + mod_ref[3:4, :]
    h_ref[...] = h
    logits = jnp.dot(h.astype(BF16), rw_ref[...], preferred_element_type=F32)
    scores = 1.0 / (1.0 + jnp.exp(-logits))
    rows = scores.shape[0]
    lane = lax.broadcasted_iota(jnp.int32, scores.shape, 1)
    lane_f = lane.astype(F32)
    cand = jnp.where(lane < n_experts, scores + rb_ref[...], -jnp.inf)
    hits = []
    for _ in range(TOP_K):
        best = jnp.max(cand, axis=-1, keepdims=True)
        first = jnp.min(jnp.where(cand == best, lane_f, float(LANES)), axis=-1, keepdims=True)
        hit = lane_f == first
        hits.append(hit)
        cand = jnp.where(hit, -jnp.inf, cand)
    chosen = functools.reduce(jnp.logical_or, hits)
    total = jnp.sum(jnp.where(chosen, scores, 0.0), axis=-1, keepdims=True)
    ri = lax.broadcasted_iota(jnp.int32, (rows, rows), 0)
    ci = lax.broadcasted_iota(jnp.int32, (rows, rows), 1)
    chosen_b = jnp.where(chosen, 1.0, 0.0).astype(BF16)
    before = jnp.dot(jnp.where(ci < ri, 1.0, 0.0).astype(BF16), chosen_b, preferred_element_type=F32)
    rank_all = before + carry_ref[...]
    for k, hit in enumerate(hits):
        pick = lambda a: jnp.sum(jnp.where(hit, a, 0.0), axis=-1, keepdims=True)
        eid_ref[:, k:k + 1] = pick(lane_f).astype(jnp.int32)
        rank_ref[:, k:k + 1] = pick(rank_all).astype(jnp.int32)
        w_ref[:, k:k + 1] = pick(scores) / total * ROUTE_SCALE
    carry_ref[...] += jnp.sum(chosen_b.astype(F32), axis=0, keepdims=True)
    count_ref[...] = carry_ref[...]


def _router(s, mod, norm_g, router_w_b, router_b, n_experts):
    n, d = s.shape
    nt = n // ROW_TILE
    row = lambda w: pl.BlockSpec((ROW_TILE, w), lambda i: (i, 0))
    return pl.pallas_call(
        functools.partial(_router_kernel, n_experts),
        grid=(nt,),
        in_specs=[row(d),
                  pl.BlockSpec((None, 8, d), lambda i: (_row_group(i), 0, 0)),
                  _resident((1, d)),
                  _resident((d, LANES)),
                  _resident((1, LANES))],
        out_specs=[row(d), row(TOP_K), row(TOP_K), row(TOP_K), pl.BlockSpec((1, LANES), lambda i: (0, 0))],
        out_shape=[jax.ShapeDtypeStruct((n, d), F32), jax.ShapeDtypeStruct((n, TOP_K), jnp.int32),
                   jax.ShapeDtypeStruct((n, TOP_K), jnp.int32), jax.ShapeDtypeStruct((n, TOP_K), F32),
                   jax.ShapeDtypeStruct((1, LANES), F32)],
        scratch_shapes=[pltpu.VMEM((1, LANES), F32)],
        compiler_params=_params("arbitrary"),
        name="router",
    )(s, mod, norm_g.reshape(1, d), router_w_b, router_b)


def _routing_plan(eid, rank, counts_f, n_experts, n_tiles):
    counts = counts_f[0, :n_experts].astype(jnp.int32)
    starts = jnp.concatenate([jnp.zeros((1,), jnp.int32), jnp.cumsum(counts)])
    experts = jnp.arange(n_experts, dtype=jnp.int32)
    dest = jnp.sum(jnp.where(eid[..., None] == experts, starts[:-1], 0), axis=-1) + rank
    first_tile = starts[:-1] // EXPERT_ROW_TILE
    last_tile = (starts[1:] - 1) // EXPERT_ROW_TILE
    visits_per_expert = jnp.where(counts > 0, last_tile - first_tile + 1, 0)
    visit_end = jnp.cumsum(visits_per_expert)
    visit_start = visit_end - visits_per_expert
    n_visits = visit_end[-1]
    v = jnp.minimum(jnp.arange(n_tiles + n_experts - 1, dtype=jnp.int32), n_visits - 1)
    visit_expert = jnp.sum((v[:, None] >= visit_end[None, :]).astype(jnp.int32), axis=-1)
    own = visit_expert[:, None] == experts[None, :]
    visit_tile = jnp.sum(jnp.where(own, (first_tile - visit_start)[None, :], 0), axis=-1) + v
    visited = visits_per_expert > 0
    later = jnp.logical_and(experts[None, :] > experts[:, None], visited[None, :])
    next_visited = jnp.min(jnp.where(later, experts[None, :], n_experts), axis=-1)
    next_visited = jnp.where(next_visited == n_experts, -1, next_visited)
    visit_next = jnp.sum(jnp.where(own, next_visited[None, :], 0), axis=-1)
    visit_first = jnp.sum(jnp.where(own, visit_start[None, :], 0), axis=-1) == v
    order = jnp.cumsum(visited.astype(jnp.int32)) - 1
    visit_slot = jnp.sum(jnp.where(own, order[None, :], 0), axis=-1) % 2
    plan = (visit_tile, visit_expert, starts, n_visits.reshape(1),
            visit_first.astype(jnp.int32), visit_slot, visit_next)
    return dest, plan


def _dispatch_kernel(dest_ref, h_ref, xs_hbm, sem):
    def body(j, carry):
        for k in range(TOP_K):
            pltpu.make_async_copy(h_ref.at[pl.ds(j, 1)], xs_hbm.at[pl.ds(dest_ref[k, j], 1)], sem).start()
        return carry

    lax.fori_loop(0, ROW_TILE, body, 0)
    all_rows = xs_hbm.at[pl.ds(0, ROW_TILE * TOP_K)]
    pltpu.make_async_copy(all_rows, all_rows, sem).wait()


def _dispatch(h, dest_tiles):
    n, d = h.shape
    nt = n // ROW_TILE
    return pl.pallas_call(
        _dispatch_kernel,
        grid=(nt,),
        in_specs=[pl.BlockSpec((None, TOP_K, ROW_TILE), lambda i: (i, 0, 0), memory_space=pltpu.SMEM),
                  pl.BlockSpec((ROW_TILE, d), lambda i: (i, 0))],
        out_specs=pl.BlockSpec(memory_space=pl.ANY),
        out_shape=jax.ShapeDtypeStruct((n * TOP_K, d), F32),
        scratch_shapes=[pltpu.SemaphoreType.DMA(())],
        compiler_params=_params("arbitrary"),
        name="moe_dispatch",
    )(dest_tiles, h)


def _grouped_kernel(layer, vt_ref, ve_ref, starts_ref, nv_ref, first_ref, slot_ref, next_ref,
                    x_ref, wg_hbm, wu_hbm, wd_hbm, y_ref,
                    wgb_ref, wub_ref, wdb_ref, sg_ref, su_ref, sd_ref, wsem):
    v = pl.program_id(0)
    e = ve_ref[v]
    t = vt_ref[v]
    prev = jnp.maximum(v - 1, 0)

    def weight_copies(expert, slot):
        return [pltpu.make_async_copy(src.at[layer, expert], dst.at[slot], wsem.at[slot])
                for src, dst in ((wg_hbm, sg_ref), (wu_hbm, su_ref), (wd_hbm, sd_ref))]

    @pl.when(v == 0)
    def _():
        for cp in weight_copies(e, 0):
            cp.start(priority=WEIGHT_DMA_PRIORITY)

    @pl.when(jnp.logical_and(v < nv_ref[0], first_ref[v] == 1))
    def _():
        slot = slot_ref[v]
        for cp in weight_copies(e, slot):
            cp.wait()
        wgb_ref[...] = sg_ref[slot].astype(BF16)
        wub_ref[...] = su_ref[slot].astype(BF16)
        wdb_ref[...] = sd_ref[slot].astype(BF16)
        nxt = next_ref[v]

        @pl.when(nxt >= 0)
        def _():
            for cp in weight_copies(nxt, 1 - slot):
                cp.start(priority=WEIGHT_DMA_PRIORITY)

    @pl.when(v < nv_ref[0])
    def _():
        x = x_ref[...].astype(BF16)
        a = jnp.dot(x, wgb_ref[...], preferred_element_type=F32)
        b = jnp.dot(x, wub_ref[...], preferred_element_type=F32)
        y = jnp.dot((_silu(a) * b).astype(BF16), wdb_ref[...], preferred_element_type=F32)
        rows = x.shape[0]
        r = t * rows + lax.broadcasted_iota(jnp.int32, (rows, 1), 0)
        y = jnp.where(jnp.logical_and(r >= starts_ref[e], r < starts_ref[e + 1]), y, 0.0)
        first_visit_of_tile = jnp.logical_or(v == 0, vt_ref[prev] != t)

        @pl.when(first_visit_of_tile)
        def _():
            y_ref[...] = y

        @pl.when(jnp.logical_not(first_visit_of_tile))
        def _():
            y_ref[...] += y


def _grouped_experts(xs, layer, w_gate, w_up, w_down, plan):
    p, d = xs.shape
    hidden = w_gate.shape[-1]
    tm = EXPERT_ROW_TILE
    tile = lambda v, vt, *_: (vt[v], 0)
    grid_spec = pltpu.PrefetchScalarGridSpec(
        num_scalar_prefetch=len(plan),
        grid=(plan[0].shape[0],),
        in_specs=[pl.BlockSpec((tm, d), tile),
                  pl.BlockSpec(memory_space=pl.ANY), pl.BlockSpec(memory_space=pl.ANY),
                  pl.BlockSpec(memory_space=pl.ANY)],
        out_specs=pl.BlockSpec((tm, d), tile),
        scratch_shapes=[pltpu.VMEM((d, hidden), BF16), pltpu.VMEM((d, hidden), BF16),
                        pltpu.VMEM((hidden, d), BF16),
                        pltpu.VMEM((2, d, hidden), F32), pltpu.VMEM((2, d, hidden), F32),
                        pltpu.VMEM((2, hidden, d), F32), pltpu.SemaphoreType.DMA((2,))])
    return pl.pallas_call(
        functools.partial(_grouped_kernel, layer),
        grid_spec=grid_spec,
        out_shape=jax.ShapeDtypeStruct((p, d), F32),
        compiler_params=_params("arbitrary"),
        name="moe_grouped_experts",
    )(*plan, xs, w_gate, w_up, w_down)


def _combine_kernel(n_ctx_rows, first_tile, final_norm, dest_ref, dest_next_ref, w_ref, h_ref, s_ref,
                    mod_ref, sg_ref, su_ref, sd_ref, fg_ref, ys_hbm, o_ref, ybuf_ref, sems):
    i = pl.program_id(0)
    n_steps = pl.num_programs(0)
    rows = h_ref.shape[0]
    slot = i % 2

    def gather(dref, to_slot):
        def body(j, carry):
            for k in range(TOP_K):
                pltpu.make_async_copy(ys_hbm.at[pl.ds(dref[k, j], 1)],
                                      ybuf_ref.at[to_slot, k, pl.ds(j, 1)], sems.at[to_slot]).start()
            return carry
        lax.fori_loop(0, rows, body, 0)

    @pl.when(i == 0)
    def _():
        gather(dest_ref, 0)

    @pl.when(i + 1 < n_steps)
    def _():
        gather(dest_next_ref, 1 - slot)

    hb = h_ref[...].astype(BF16)
    a = jnp.dot(hb, sg_ref[...], preferred_element_type=F32)
    b = jnp.dot(hb, su_ref[...], preferred_element_type=F32)
    y = jnp.dot((_silu(a) * b).astype(BF16), sd_ref[...], preferred_element_type=F32)

    pltpu.make_async_copy(ybuf_ref.at[slot], ybuf_ref.at[slot], sems.at[slot]).wait()
    w = w_ref[...]
    for k in range(TOP_K):
        y += w[:, k:k + 1] * ybuf_ref[slot, k]
    r = (first_tile + i) * rows + lax.broadcasted_iota(jnp.int32, (rows, 1), 0)
    gate2 = jnp.where(r < n_ctx_rows, mod_ref[1, 5:6, :], mod_ref[0, 5:6, :])
    out = s_ref[...] + gate2 * y
    o_ref[...] = _rms(out, fg_ref[...]) if final_norm else out


def _combine(ys, dest_tiles, w, h, s, mod, sh_gate_b, sh_up_b, sh_down_b, final_g, n_ctx_rows, last_layer):
    n, d = s.shape
    tc = COMBINE_ROW_TILE
    first_tile = n_ctx_rows // tc if last_layer else 0
    steps = n // tc - first_tile
    hidden = sh_gate_b.shape[-1]
    row = lambda width: pl.BlockSpec((tc, width), lambda i: (i + first_tile, 0))
    idx = lambda shift: pl.BlockSpec(
        (None, TOP_K, tc), lambda i: (jnp.minimum(i + first_tile + shift, n // tc - 1), 0, 0),
        memory_space=pltpu.SMEM)
    return pl.pallas_call(
        functools.partial(_combine_kernel, n_ctx_rows, first_tile, last_layer),
        grid=(steps,),
        in_specs=[idx(0), idx(1), row(TOP_K), row(d), row(d),
                  pl.BlockSpec((2, 8, d), lambda i: (0, 0, 0)),
                  _resident((d, hidden)), _resident((d, hidden)), _resident((hidden, d)),
                  _resident((1, d)),
                  pl.BlockSpec(memory_space=pl.ANY)],
        out_specs=pl.BlockSpec((tc, d), lambda i: (i, 0)),
        out_shape=jax.ShapeDtypeStruct((steps * tc, d), F32),
        scratch_shapes=[pltpu.VMEM((2, TOP_K, tc, d), F32), pltpu.SemaphoreType.DMA((2,))],
        compiler_params=_params("arbitrary"),
        name="moe_combine",
    )(dest_tiles, dest_tiles, w, h, s, mod, sh_gate_b, sh_up_b, sh_down_b, final_g.reshape(1, d), ys)


def _tile_major(dest, tile):
    n, k = dest.shape
    return dest.reshape(n // tile, tile, k).transpose(0, 2, 1)


def _rope_tables(n_ctx, n_tokens, dim):
    half = dim // 2
    inv_freq = ROPE_THETA ** (-jnp.arange(0, half, 2, dtype=F32) / half)
    t = jnp.arange(n_tokens, dtype=jnp.int32)
    ang_r = (t // GRID_W).astype(F32)[:, None] * inv_freq
    ang_c = (t % GRID_W).astype(F32)[:, None] * inv_freq
    cos = jnp.concatenate([jnp.cos(ang_r)] * 2 + [jnp.cos(ang_c)] * 2, axis=-1)
    sin = jnp.concatenate([-jnp.sin(ang_r), jnp.sin(ang_r), -jnp.sin(ang_c), jnp.sin(ang_c)], axis=-1)
    reps = LANES // dim
    cos = jnp.tile(cos, (1, reps))
    sin = jnp.tile(sin, (1, reps))
    cos = jnp.concatenate([jnp.ones((n_ctx, LANES), F32), cos], axis=0)
    sin = jnp.concatenate([jnp.zeros((n_ctx, LANES), F32), sin], axis=0)
    return cos, sin


def kernel(x, c, ctx, c_ctx, norm1_g, norm2_g, w_mod, b_mod, w_in, gla_wa_f, gla_ba_f, gla_wa_b,
           gla_ba_b, gla_norm_g, q_norm_g, k_norm_g, diff_lq1, diff_lk1, diff_lq2, diff_lk2,
           diff_norm_g, w_out, router_w, router_b, exp_w_gate, exp_w_up, exp_w_down,
           sh_w_gate, sh_w_up, sh_w_down, final_g):
    batch, n_tokens, d = x.shape
    n_ctx = ctx.shape[1]
    depth = w_mod.shape[0]
    n_experts = router_w.shape[-1]
    n_rows = n_ctx + n_tokens
    assert batch == 1 and n_ctx == ROW_TILE and n_tokens % ROW_TILE == 0
    assert TOP_K <= n_experts <= LANES and EXPERT_ROW_TILE <= ROW_TILE

    s = jnp.concatenate([ctx[0], x[0]], axis=0)
    mod_all = _modulation(c, c_ctx, w_mod, b_mod)
    tables = _rope_tables(n_ctx, n_tokens, HEAD_DIM) + _rope_tables(n_ctx, n_tokens, DIFF_DQK)

    for l in range(depth):
        need_ctx = l < depth - 1
        lam_init = 0.8 - 0.6 * math.exp(-0.3 * l)
        mod = mod_all[l]

        w_in_b = jnp.concatenate(
            [w_in[l][:, :ORIG_LR], w_in[l][:, ORIG_LR + 2 * GLA_GATE_RANK:],
             w_in[l][:, ORIG_LR:ORIG_LR + 2 * GLA_GATE_RANK],
             jnp.zeros((d, LANES - 2 * GLA_GATE_RANK), F32)], axis=1).astype(BF16)
        wa = jnp.zeros((LANES, 2 * GLA_QK), F32)
        wa = wa.at[:GLA_GATE_RANK, :GLA_QK].set(gla_wa_f[l])
        wa = wa.at[GLA_GATE_RANK:2 * GLA_GATE_RANK, GLA_QK:].set(gla_wa_b[l]).astype(BF16)
        ba = jnp.concatenate([gla_ba_f[l], gla_ba_b[l]]).reshape(1, -1)
        lamv = jnp.zeros((8, LANES), F32)
        for r, vec in enumerate((diff_lq1[l], diff_lk1[l], diff_lq2[l], diff_lk2[l])):
            lamv = lamv.at[r, :DIFF_DQK].set(vec)
        rw = jnp.pad(router_w[l], ((0, 0), (0, LANES - n_experts))).astype(BF16)
        rb = jnp.pad(router_b[l], (0, LANES - n_experts)).reshape(1, LANES)

        (glaq, glak, glav, gate, la, q_t, k, v_t, dq_t, dk, dv_t) = _input_projection(
            s, mod, norm1_g[l], w_in_b, wa, ba, q_norm_g[l], k_norm_g[l], tables)
        o_gla = _gla(glaq, glak, glav, la)
        gqa_parts = [_gqa_attention(q_t, k, v_t, latent=True)]
        diff_parts = [_diff_attention(dq_t, dk, dv_t, lamv, diff_norm_g[l], lam_init, latent=True)]
        if need_ctx:
            gqa_parts.insert(0, _gqa_attention(q_t, k, v_t, latent=False))
            diff_parts.insert(0, _diff_attention(dq_t, dk, dv_t, lamv, diff_norm_g[l], lam_init, latent=False))
        else:
            gqa_parts.insert(0, jnp.zeros((n_ctx, GQA_Q), BF16))
            diff_parts.insert(0, jnp.zeros((n_ctx, DIFF_V), BF16))
        o_gqa = jnp.concatenate(gqa_parts, axis=0)
        o_diff = jnp.concatenate(diff_parts, axis=0)
        s = _output_projection(s, mod, o_gla, gate, o_gqa, o_diff, gla_norm_g[l], w_out[l].astype(BF16))
        h2, eid, rank, w_route, counts_f = _router(s, mod, norm2_g[l], rw, rb, n_experts)
        dest, plan = _routing_plan(eid, rank, counts_f, n_experts, n_rows * TOP_K // EXPERT_ROW_TILE)
        xs = _dispatch(h2, _tile_major(dest, ROW_TILE))
        ys = _grouped_experts(xs, l, exp_w_gate, exp_w_up, exp_w_down, plan)
        s = _combine(ys, _tile_major(dest, COMBINE_ROW_TILE), w_route, h2, s, mod,
                     sh_w_gate[l].astype(BF16), sh_w_up[l].astype(BF16), sh_w_down[l].astype(BF16),
                     final_g, n_ctx, last_layer=l == depth - 1)

    return s.reshape(batch, n_tokens, d)
```

```python
import functools
import math

import jax
import jax.numpy as jnp
from jax import lax
from jax.experimental import pallas as pl
from jax.experimental.pallas import tpu as pltpu

F32 = jnp.float32
BF16 = jnp.bfloat16

GRID_W = 64
HEAD_DIM = 128
GLA_HEADS = 4
GLA_DK = 64
GLA_DV = 128
GLA_GATE_RANK = 16
GLA_TAU = 16.0
GLA_CHUNK = 64
GQA_HEADS = 8
GQA_KV_HEADS = 2
DIFF_HEADS = 4
DIFF_DQK = 64
DIFF_DV = 128
TOP_K = 8
ROUTE_SCALE = 2.5
ROPE_THETA = 10000.0
NORM_EPS = 1e-6
LOG2_E = math.log2(math.e)

LANES = 128
ROW_TILE = 256
EXPERT_ROW_TILE = 256
COMBINE_ROW_TILE = 256
WEIGHT_DMA_PRIORITY = 1
VMEM_LIMIT = 56 * 1024 * 1024

GLA_QK = GLA_HEADS * GLA_DK
GLA_V = GLA_HEADS * GLA_DV
GQA_Q = GQA_HEADS * HEAD_DIM
GQA_KV = GQA_KV_HEADS * HEAD_DIM
DIFF_QK = DIFF_HEADS * 2 * DIFF_DQK
DIFF_V = DIFF_HEADS * DIFF_DV
MIX_WIDTH = GLA_V + GQA_Q + DIFF_V
VT_PAD = 16
VT_ROWS = HEAD_DIM + VT_PAD

C_GLAQ = 0
C_GLAK = C_GLAQ + GLA_QK
C_GLAV = C_GLAK + GLA_QK
C_GATE = C_GLAV + GLA_V
C_GQAQ = C_GATE + GLA_V
C_GQAK = C_GQAQ + GQA_Q
C_GQAV = C_GQAK + GQA_KV
C_DQ = C_GQAV + GQA_KV
C_DK = C_DQ + DIFF_QK
C_DV = C_DK + DIFF_QK
C_LR = C_DV + DIFF_V
IN_COLS = C_LR + LANES
ORIG_LR = 2 * GLA_QK + 2 * GLA_V


def _params(*sem):
    return pltpu.CompilerParams(dimension_semantics=sem, vmem_limit_bytes=VMEM_LIMIT)


def _resident(shape):
    nd = len(shape)
    return pl.BlockSpec(shape, lambda *_: (0,) * nd, pipeline_mode=pl.Buffered(1))


def _silu(a):
    return a / (1.0 + jnp.exp(-a))


def _rms(x, g):
    return x * lax.rsqrt(jnp.mean(x * x, axis=-1, keepdims=True) + NORM_EPS) * g


def _row_group(i):
    return jnp.where(i == 0, 1, 0)


def _mod_kernel(a_ref, w_ref, b_ref, o_ref):
    a = _silu(a_ref[...])
    o_ref[...] = jnp.dot(a.astype(BF16), w_ref[...].astype(BF16),
                         preferred_element_type=F32) + b_ref[...]


def _modulation(c, c_ctx, w_mod, b_mod):
    depth, d, six_d = w_mod.shape
    a = jnp.zeros((8, d), F32).at[0].set(c[0]).at[1].set(c_ctx)
    tn = d // 2
    out = pl.pallas_call(
        _mod_kernel,
        grid=(depth, six_d // tn),
        in_specs=[pl.BlockSpec((8, d), lambda l, j: (0, 0)),
                  pl.BlockSpec((None, d, tn), lambda l, j: (l, 0, j)),
                  pl.BlockSpec((None, 1, tn), lambda l, j: (l, 0, j))],
        out_specs=pl.BlockSpec((None, 8, tn), lambda l, j: (l, 0, j)),
        out_shape=jax.ShapeDtypeStruct((depth, 8, six_d), F32),
        compiler_params=_params("parallel", "parallel"),
        name="modulation",
    )(a, w_mod, b_mod.reshape(depth, 1, six_d))
    m = out[:, :2].reshape(depth, 2, 6, d)
    return jnp.pad(m, ((0, 0), (0, 0), (0, 2), (0, 0)))


def _rope(xh, cos, sin, first, shift_first, shift_second):
    partner = jnp.where(first, pltpu.roll(xh, shift_first, 1), pltpu.roll(xh, shift_second, 1))
    return xh * cos + partner * sin


def _inproj_kernel(x_ref, mod_ref, g_ref, w_ref, wa_ref, ba_ref, qg_ref, kg_ref,
                   cg_ref, sg_ref, cd_ref, sd_ref,
                   glaq_ref, glak_ref, glav_ref, gate_ref, la_ref,
                   qt_ref, k_ref, vt_ref, dqt_ref, dk_ref, dvt_ref):
    x = x_ref[...]
    h = _rms(x, g_ref[...]) * (1.0 + mod_ref[1:2, :]) + mod_ref[0:1, :]
    hb = h.astype(BF16)

    def proj(start, width):
        return jnp.dot(hb, w_ref[:, start:start + width], preferred_element_type=F32)

    glaq_ref[...] = proj(C_GLAQ, GLA_QK) * (GLA_DK ** -0.5)
    glak_ref[...] = proj(C_GLAK, GLA_QK)
    glav_ref[...] = proj(C_GLAV, GLA_V)
    gate_ref[...] = proj(C_GATE, GLA_V)
    ones_row = jnp.where(lax.broadcasted_iota(jnp.int32, (VT_PAD, x.shape[0]), 0) == 0, 1.0, 0.0).astype(BF16)
    zv = proj(C_GQAV, GQA_KV)
    for hd in range(GQA_KV_HEADS):
        vt_ref[hd * VT_ROWS:hd * VT_ROWS + HEAD_DIM, :] = zv[:, hd * HEAD_DIM:(hd + 1) * HEAD_DIM].T.astype(BF16)
        vt_ref[hd * VT_ROWS + HEAD_DIM:(hd + 1) * VT_ROWS, :] = ones_row
    zdv = proj(C_DV, DIFF_V)
    for hd in range(DIFF_HEADS):
        dvt_ref[hd * VT_ROWS:hd * VT_ROWS + DIFF_DV, :] = zdv[:, hd * DIFF_DV:(hd + 1) * DIFF_DV].T.astype(BF16)
        dvt_ref[hd * VT_ROWS + DIFF_DV:(hd + 1) * VT_ROWS, :] = ones_row

    z_lr = proj(C_LR, LANES).astype(BF16)
    pre = jnp.dot(z_lr, wa_ref[...], preferred_element_type=F32) + ba_ref[...]
    log_sig = -(jnp.maximum(-pre, 0.0) + jnp.log1p(jnp.exp(-jnp.abs(pre))))
    la = log_sig * (1.0 / GLA_TAU)
    la_ref[0] = la[:, :GLA_QK]
    la_ref[1] = la[:, GLA_QK:]

    rows = x.shape[0]
    lane = lax.broadcasted_iota(jnp.int32, (rows, LANES), 1)
    first_g = (lane % 64) < 32
    first_d = (lane % 32) < 16
    cg, sg, cd, sd = cg_ref[...], sg_ref[...], cd_ref[...], sd_ref[...]
    scale_g = HEAD_DIM ** -0.5 * LOG2_E
    scale_d = DIFF_DQK ** -0.5 * LOG2_E

    zq = proj(C_GQAQ, GQA_Q)
    for hd in range(GQA_HEADS):
        sl = slice(hd * HEAD_DIM, (hd + 1) * HEAD_DIM)
        qh = _rope(_rms(zq[:, sl], qg_ref[...]), cg, sg, first_g, 96, 32) * scale_g
        qt_ref[sl, :] = qh.T.astype(BF16)
    zk = proj(C_GQAK, GQA_KV)
    for hd in range(GQA_KV_HEADS):
        kh = _rms(zk[:, hd * HEAD_DIM:(hd + 1) * HEAD_DIM], kg_ref[...])
        k_ref[:, hd * HEAD_DIM:(hd + 1) * HEAD_DIM] = _rope(kh, cg, sg, first_g, 96, 32).astype(BF16)
    zdq = proj(C_DQ, DIFF_QK)
    zdk = proj(C_DK, DIFF_QK)
    for hd in range(DIFF_HEADS):
        sl = slice(hd * LANES, (hd + 1) * LANES)
        dqt_ref[sl, :] = (_rope(zdq[:, sl], cd, sd, first_d, 112, 16) * scale_d).T.astype(BF16)
        dk_ref[:, sl] = _rope(zdk[:, sl], cd, sd, first_d, 112, 16).astype(BF16)


def _input_projection(s, mod, norm_g, w_in_b, wa, ba, q_norm_g, k_norm_g, tables):
    n, d = s.shape
    nt = n // ROW_TILE
    row = lambda w: pl.BlockSpec((ROW_TILE, w), lambda i: (i, 0))
    col = lambda w: pl.BlockSpec((w, ROW_TILE), lambda i: (0, i))
    tile_t = lambda w: pl.BlockSpec((None, w, ROW_TILE), lambda i: (i, 0, 0))
    f32o = lambda w: jax.ShapeDtypeStruct((n, w), F32)
    bfo = lambda w: jax.ShapeDtypeStruct((n, w), BF16)
    return pl.pallas_call(
        _inproj_kernel,
        grid=(nt,),
        in_specs=[row(d),
                  pl.BlockSpec((None, 8, d), lambda i: (_row_group(i), 0, 0)),
                  _resident((1, d)),
                  _resident((d, IN_COLS)),
                  _resident((LANES, 2 * GLA_QK)),
                  _resident((1, 2 * GLA_QK)),
                  _resident((1, HEAD_DIM)),
                  _resident((1, HEAD_DIM)),
                  row(LANES), row(LANES), row(LANES), row(LANES)],
        out_specs=[row(GLA_QK), row(GLA_QK), row(GLA_V), row(GLA_V),
                   pl.BlockSpec((2, ROW_TILE, GLA_QK), lambda i: (0, i, 0)),
                   col(GQA_Q), row(GQA_KV), tile_t(GQA_KV_HEADS * VT_ROWS),
                   col(DIFF_QK), row(DIFF_QK), tile_t(DIFF_HEADS * VT_ROWS)],
        out_shape=[f32o(GLA_QK), f32o(GLA_QK), f32o(GLA_V), f32o(GLA_V),
                   jax.ShapeDtypeStruct((2, n, GLA_QK), F32),
                   jax.ShapeDtypeStruct((GQA_Q, n), BF16), bfo(GQA_KV),
                   jax.ShapeDtypeStruct((nt, GQA_KV_HEADS * VT_ROWS, ROW_TILE), BF16),
                   jax.ShapeDtypeStruct((DIFF_QK, n), BF16), bfo(DIFF_QK),
                   jax.ShapeDtypeStruct((nt, DIFF_HEADS * VT_ROWS, ROW_TILE), BF16)],
        compiler_params=_params("parallel"),
        name="input_projection",
    )(s, mod, norm_g.reshape(1, d), w_in_b, wa, ba, q_norm_g.reshape(1, -1), k_norm_g.reshape(1, -1),
      *tables)


def _gla_block(reverse, q_ref, k_ref, v_ref, la_ref, o_ref, st_ref):
    rows = q_ref.shape[0]
    n_chunks = rows // GLA_CHUNK
    ri = lax.broadcasted_iota(jnp.int32, (rows, rows), 0)
    ci = lax.broadcasted_iota(jnp.int32, (rows, rows), 1)
    same_chunk = (ri // GLA_CHUNK) == (ci // GLA_CHUNK)
    allowed = same_chunk & ((ci >= ri) if reverse else (ci <= ri))
    la = la_ref[...]
    cum = jnp.dot(jnp.where(allowed, 1.0, 0.0).astype(F32), la,
                  preferred_element_type=F32, precision=lax.Precision.HIGHEST)
    tot = jnp.dot(jnp.where(same_chunk, 1.0, 0.0).astype(F32), la,
                  preferred_element_type=F32, precision=lax.Precision.HIGHEST)
    k = k_ref[...]
    q_dec = q_ref[...] * jnp.exp(cum)
    k_inv = (k * jnp.exp(-cum)).astype(BF16)
    k_end = k * jnp.exp(tot - cum)
    v = v_ref[...]
    vb = v.astype(BF16)
    v_t = v.T.astype(BF16)
    lane_head = lax.broadcasted_iota(jnp.int32, (rows, GLA_QK), 1) // GLA_DK
    row_chunk = lax.broadcasted_iota(jnp.int32, (rows, GLA_QK), 0) // GLA_CHUNK
    nt_dims = (((1,), (1,)), ((), ()))

    for hd in range(GLA_HEADS):
        qh = jnp.where(lane_head == hd, q_dec, 0.0).astype(BF16)
        a = lax.dot_general(qh, k_inv, nt_dims, preferred_element_type=F32)
        a = jnp.where(allowed, a, 0.0).astype(BF16)
        o_ref[:, hd * GLA_DV:(hd + 1) * GLA_DV] = jnp.dot(
            a, vb[:, hd * GLA_DV:(hd + 1) * GLA_DV], preferred_element_type=F32)

    chunk_order = range(n_chunks - 1, -1, -1) if reverse else range(n_chunks)
    chunk_lane_head = lax.broadcasted_iota(jnp.int32, (GLA_CHUNK, GLA_QK), 1) // GLA_DK
    for c in chunk_order:
        sl = slice(c * GLA_CHUNK, (c + 1) * GLA_CHUNK)
        state = st_ref[...]
        q4 = jnp.concatenate(
            [jnp.where(chunk_lane_head == hd, q_dec[sl], 0.0) for hd in range(GLA_HEADS)],
            axis=0).astype(BF16)
        r = lax.dot_general(q4, state.astype(BF16), nt_dims, preferred_element_type=F32)
        for hd in range(GLA_HEADS):
            o_ref[sl, hd * GLA_DV:(hd + 1) * GLA_DV] += r[hd * GLA_CHUNK:(hd + 1) * GLA_CHUNK,
                                                          hd * GLA_DV:(hd + 1) * GLA_DV]
        k_end_c = jnp.where(row_chunk == c, k_end, 0.0).astype(BF16)
        u_t = jnp.dot(v_t, k_end_c, preferred_element_type=F32)
        st_ref[...] = state * jnp.exp(tot[c * GLA_CHUNK:c * GLA_CHUNK + 1]) + u_t


def _gla_kernel(q_ref, k_ref, v_ref, la_ref, o_ref, st_ref):
    @pl.when(pl.program_id(1) == 0)
    def _():
        st_ref[...] = jnp.zeros_like(st_ref)

    @pl.when(pl.program_id(0) == 0)
    def _():
        _gla_block(False, q_ref, k_ref, v_ref, la_ref, o_ref, st_ref)

    @pl.when(pl.program_id(0) == 1)
    def _():
        _gla_block(True, q_ref, k_ref, v_ref, la_ref, o_ref, st_ref)


def _gla(glaq, glak, glav, la):
    n = glaq.shape[0]
    nt = n // ROW_TILE

    def blk(dr, j):
        return jnp.where(dr == 0, j, jnp.where(j == 0, 0, nt - j))

    row = lambda w: pl.BlockSpec((ROW_TILE, w), lambda dr, j: (blk(dr, j), 0))
    return pl.pallas_call(
        _gla_kernel,
        grid=(2, nt),
        in_specs=[row(GLA_QK), row(GLA_QK), row(GLA_V),
                  pl.BlockSpec((None, ROW_TILE, GLA_QK), lambda dr, j: (dr, blk(dr, j), 0))],
        out_specs=pl.BlockSpec((None, ROW_TILE, GLA_V), lambda dr, j: (dr, blk(dr, j), 0)),
        out_shape=jax.ShapeDtypeStruct((2, n, GLA_V), F32),
        scratch_shapes=[pltpu.VMEM((GLA_V, GLA_QK), F32)],
        compiler_params=_params("arbitrary", "arbitrary"),
        name="gla_scan",
    )(glaq, glak, glav, la)


class _FlashMaps:
    def __init__(self, q_maps, k_of_map, load_k, load_vt):
        self.q_maps, self.k_of_map, self.load_k, self.load_vt = q_maps, k_of_map, load_k, load_vt


def _flash_steps(maps, first_tile, tiles_per_step, n_steps, s_ref, m_ref, acc_ref):
    n_maps = len(maps.q_maps)
    rows = tiles_per_step * ROW_TILE
    for u in range(n_steps):
        first_row = pl.multiple_of((first_tile + u * tiles_per_step) * ROW_TILE, ROW_TILE)
        keys = {src: maps.load_k(src, first_row, rows) for src in sorted(set(maps.k_of_map))}
        for j in range(n_maps):
            s_ref[u * n_maps + j, :rows] = jnp.dot(keys[maps.k_of_map[j]], maps.q_maps[j],
                                                   preferred_element_type=F32)
    for u in range(n_steps):
        tile = first_tile + u * tiles_per_step
        for j in range(n_maps):
            s_t = s_ref[u * n_maps + j, :rows]
            m_prev = m_ref[j]
            m_new = jnp.maximum(m_prev, jnp.max(s_t, axis=0, keepdims=True))
            alpha = jnp.exp2(m_prev - m_new)
            pb = jnp.exp2(s_t - m_new).astype(BF16)
            pv = jnp.dot(maps.load_vt(maps.k_of_map[j], tile), pb[:ROW_TILE], preferred_element_type=F32)
            for r in range(1, tiles_per_step):
                pv += jnp.dot(maps.load_vt(maps.k_of_map[j], tile + r), pb[r * ROW_TILE:(r + 1) * ROW_TILE],
                              preferred_element_type=F32)
            acc_ref[j] = alpha * acc_ref[j] + pv
            m_ref[j] = m_new


def _flash_attend(maps, latent, n_key_tiles, tiles_per_step, n_steps, s_ref, m_ref, acc_ref):
    m_ref[...] = jnp.full(m_ref.shape, -jnp.inf, F32)
    acc_ref[...] = jnp.zeros(acc_ref.shape, F32)
    _flash_steps(maps, 0, 1, 1, s_ref, m_ref, acc_ref)
    if not latent:
        return
    per_iter = tiles_per_step * n_steps

    def body(it, carry):
        _flash_steps(maps, 1 + it * per_iter, tiles_per_step, n_steps, s_ref, m_ref, acc_ref)
        return carry

    lax.fori_loop(0, (n_key_tiles - 1) // per_iter, body, 0)


MAX_Q_SUBTILES = 4


def _query_subtiles(n_key_tiles, latent):
    n_sub = 1
    while latent and 2 * n_sub <= MAX_Q_SUBTILES and (n_key_tiles - 1) % (2 * n_sub) == 0:
        n_sub *= 2
    return n_sub


def _query_specs(width, n_sub, latent):
    if not latent:
        return [pl.BlockSpec((width, ROW_TILE), lambda g, i: (g, 0))]
    return [pl.BlockSpec((width, ROW_TILE), lambda g, i, sub=sub: (g, 1 + n_sub * i + sub))
            for sub in range(n_sub)]


def _query_steps(n_key_tiles, n_sub, latent):
    return (n_key_tiles - 1) // n_sub if latent else 1


def _flash_plan(n_key_tiles):
    latent = n_key_tiles - 1
    tiles_per_step = 2 if latent % 2 == 0 else 1
    n_steps = 2 if latent % (2 * tiles_per_step) == 0 else 1
    return tiles_per_step, n_steps


def _flash_scratch(n_maps, dv, plan):
    tiles_per_step, n_steps = plan
    return [pltpu.VMEM((n_steps * n_maps, tiles_per_step * ROW_TILE, ROW_TILE), F32),
            pltpu.VMEM((n_maps, 1, ROW_TILE), F32),
            pltpu.VMEM((n_maps, dv + VT_PAD, ROW_TILE), F32)]


GQA_GROUP = GQA_HEADS // GQA_KV_HEADS


def _gqa_kernel(latent, n_sub, plan, *refs):
    q_refs = refs[:n_sub]
    k_ref, vt_ref, o_ref, s_ref, m_ref, acc_ref = refs[n_sub:]
    maps = _FlashMaps(
        q_maps=[q[hd * HEAD_DIM:(hd + 1) * HEAD_DIM, :] for q in q_refs for hd in range(GQA_GROUP)],
        k_of_map=[0] * (n_sub * GQA_GROUP),
        load_k=lambda src, first_row, rows: k_ref[pl.ds(first_row, rows), :],
        load_vt=lambda src, tile: vt_ref[tile])
    _flash_attend(maps, latent, vt_ref.shape[0], *plan, s_ref, m_ref, acc_ref)
    for sub in range(n_sub):
        for hd in range(GQA_GROUP):
            j = sub * GQA_GROUP + hd
            o_t = acc_ref[j, :HEAD_DIM] / acc_ref[j, HEAD_DIM:HEAD_DIM + 1]
            o_ref[sub * ROW_TILE:(sub + 1) * ROW_TILE, hd * HEAD_DIM:(hd + 1) * HEAD_DIM] = (
                o_t.T.astype(o_ref.dtype))


def _gqa_attention(q_t, k, v_t, latent):
    n = k.shape[0]
    nt = n // ROW_TILE
    gw = GQA_GROUP * HEAD_DIM
    plan = _flash_plan(nt)
    n_sub = _query_subtiles(nt, latent)
    q_specs = _query_specs(gw, n_sub, latent)
    steps = _query_steps(nt, n_sub, latent)
    rows = len(q_specs) * ROW_TILE
    return pl.pallas_call(
        functools.partial(_gqa_kernel, latent, n_sub, plan),
        grid=(GQA_KV_HEADS, steps),
        in_specs=q_specs + [pl.BlockSpec((n, HEAD_DIM), lambda g, i: (0, g)),
                            pl.BlockSpec((nt, VT_ROWS, ROW_TILE), lambda g, i: (0, g, 0))],
        out_specs=pl.BlockSpec((rows, gw), lambda g, i: (i, g)),
        out_shape=jax.ShapeDtypeStruct((steps * rows, GQA_Q), BF16),
        scratch_shapes=_flash_scratch(len(q_specs) * GQA_GROUP, HEAD_DIM, plan),
        compiler_params=_params("parallel", "arbitrary"),
        name="gqa_attention",
    )(*([q_t] * len(q_specs)), k, v_t)


DIFF_PAIR = 2


def _diff_kernel(latent, n_sub, plan, lam_init, *refs):
    q_refs = refs[:n_sub]
    k_ref, vt_ref, lamv_ref, g_ref, o_ref, s_ref, m_ref, acc_ref = refs[n_sub:]
    row = lax.broadcasted_iota(jnp.int32, (LANES, ROW_TILE), 0)
    q_maps = []
    for q_ref in q_refs:
        for hd in range(DIFF_PAIR):
            q_t = q_ref[hd * LANES:(hd + 1) * LANES, :]
            zero = jnp.zeros_like(q_t)
            q_maps += [jnp.where(row < DIFF_DQK, q_t, zero), jnp.where(row >= DIFF_DQK, q_t, zero)]
    maps = _FlashMaps(
        q_maps=q_maps,
        k_of_map=[hd for _ in range(n_sub) for hd in range(DIFF_PAIR) for _ in range(2)],
        load_k=lambda src, first_row, rows: k_ref[pl.ds(first_row, rows), src * LANES:(src + 1) * LANES],
        load_vt=lambda src, tile: vt_ref[tile, src * VT_ROWS:(src + 1) * VT_ROWS, :])
    _flash_attend(maps, latent, vt_ref.shape[0], *plan, s_ref, m_ref, acc_ref)
    lv = lamv_ref[...]
    lam = (jnp.exp(jnp.sum(lv[0:1] * lv[1:2], axis=-1, keepdims=True))
           - jnp.exp(jnp.sum(lv[2:3] * lv[3:4], axis=-1, keepdims=True)) + lam_init)
    for sub in range(n_sub):
        for hd in range(DIFF_PAIR):
            j = 2 * (sub * DIFF_PAIR + hd)
            softmax_v = lambda m: acc_ref[m, :DIFF_DV] / acc_ref[m, DIFF_DV:DIFF_DV + 1]
            o_t = softmax_v(j) - lam * softmax_v(j + 1)
            o = _rms(o_t.T, g_ref[...]) * (1.0 - lam_init)
            o_ref[sub * ROW_TILE:(sub + 1) * ROW_TILE, hd * DIFF_DV:(hd + 1) * DIFF_DV] = o.astype(o_ref.dtype)


def _diff_attention(q_t, k, v_t, lamv, norm_g, lam_init, latent):
    n = k.shape[0]
    nt = n // ROW_TILE
    pw = DIFF_PAIR * LANES
    plan = _flash_plan(nt)
    n_sub = _query_subtiles(nt, latent)
    q_specs = _query_specs(pw, n_sub, latent)
    steps = _query_steps(nt, n_sub, latent)
    rows = len(q_specs) * ROW_TILE
    return pl.pallas_call(
        functools.partial(_diff_kernel, latent, n_sub, plan, lam_init),
        grid=(DIFF_HEADS // DIFF_PAIR, steps),
        in_specs=q_specs + [pl.BlockSpec((n, pw), lambda h, i: (0, h)),
                            pl.BlockSpec((nt, DIFF_PAIR * VT_ROWS, ROW_TILE), lambda h, i: (0, h, 0)),
                            pl.BlockSpec((8, LANES), lambda h, i: (0, 0)),
                            pl.BlockSpec((1, DIFF_DV), lambda h, i: (0, 0))],
        out_specs=pl.BlockSpec((rows, DIFF_PAIR * DIFF_DV), lambda h, i: (i, h)),
        out_shape=jax.ShapeDtypeStruct((steps * rows, DIFF_V), BF16),
        scratch_shapes=_flash_scratch(2 * DIFF_PAIR * len(q_specs), DIFF_DV, plan),
        compiler_params=_params("parallel", "arbitrary"),
        name="diff_attention",
    )(*([q_t] * len(q_specs)), k, v_t, lamv, norm_g.reshape(1, -1))


def _outproj_kernel(s_ref, mod_ref, og_ref, gate_ref, oq_ref, od_ref, gg_ref, w_ref, o_ref):
    og = og_ref[0] + og_ref[1]
    gate = gate_ref[...]
    y = jnp.zeros(s_ref.shape, F32)
    for hd in range(GLA_HEADS):
        sl = slice(hd * GLA_DV, (hd + 1) * GLA_DV)
        oh = _rms(og[:, sl], gg_ref[...]) * _silu(gate[:, sl])
        y += jnp.dot(oh.astype(BF16), w_ref[hd * GLA_DV:(hd + 1) * GLA_DV, :], preferred_element_type=F32)
    y += jnp.dot(oq_ref[...], w_ref[GLA_V:GLA_V + GQA_Q, :], preferred_element_type=F32)
    y += jnp.dot(od_ref[...], w_ref[GLA_V + GQA_Q:, :], preferred_element_type=F32)
    o_ref[...] = s_ref[...] + mod_ref[2:3, :] * y


def _output_projection(s, mod, o_gla, gate, o_gqa, o_diff, gla_norm_g, w_out_b):
    n, d = s.shape
    nt = n // ROW_TILE
    row = lambda w: pl.BlockSpec((ROW_TILE, w), lambda i: (i, 0))
    return pl.pallas_call(
        _outproj_kernel,
        grid=(nt,),
        in_specs=[row(d),
                  pl.BlockSpec((None, 8, d), lambda i: (_row_group(i), 0, 0)),
                  pl.BlockSpec((2, ROW_TILE, GLA_V), lambda i: (0, i, 0)),
                  row(GLA_V), row(GQA_Q), row(DIFF_V),
                  _resident((1, GLA_DV)),
                  _resident((MIX_WIDTH, d))],
        out_specs=row(d),
        out_shape=jax.ShapeDtypeStruct((n, d), F32),
        compiler_params=_params("parallel"),
        name="output_projection",
    )(s, mod, o_gla, gate, o_gqa, o_diff, gla_norm_g.reshape(1, -1), w_out_b)


def _router_kernel(n_experts, s_ref, mod_ref, g_ref, rw_ref, rb_ref,
                   h_ref, eid_ref, rank_ref, w_ref, count_ref, carry_ref):
    @pl.when(pl.program_id(0) == 0)
    def _():
        carry_ref[...] = jnp.zeros_like(carry_ref)

    h = _rms(s_ref[...], g_ref[...]) * (1.0 + mod_ref[4:5, :]) + mod_ref[3:4, :]
    h_ref[...] = h
    logits = jnp.dot(h.astype(BF16), rw_ref[...], preferred_element_type=F32)
    scores = 1.0 / (1.0 + jnp.exp(-logits))
    rows = scores.shape[0]
    lane = lax.broadcasted_iota(jnp.int32, scores.shape, 1)
    lane_f = lane.astype(F32)
    cand = jnp.where(lane < n_experts, scores + rb_ref[...], -jnp.inf)
    hits = []
    for _ in range(TOP_K):
        best = jnp.max(cand, axis=-1, keepdims=True)
        first = jnp.min(jnp.where(cand == best, lane_f, float(LANES)), axis=-1, keepdims=True)
        hit = lane_f == first
        hits.append(hit)
        cand = jnp.where(hit, -jnp.inf, cand)
    chosen = functools.reduce(jnp.logical_or, hits)
    total = jnp.sum(jnp.where(chosen, scores, 0.0), axis=-1, keepdims=True)
    ri = lax.broadcasted_iota(jnp.int32, (rows, rows), 0)
    ci = lax.broadcasted_iota(jnp.int32, (rows, rows), 1)
    chosen_b = jnp.where(chosen, 1.0, 0.0).astype(BF16)
    before = jnp.dot(jnp.where(ci < ri, 1.0, 0.0).astype(BF16), chosen_b, preferred_element_type=F32)
    rank_all = before + carry_ref[...]
    for k, hit in enumerate(hits):
        pick = lambda a: jnp.sum(jnp.where(hit, a, 0.0), axis=-1, keepdims=True)
        eid_ref[:, k:k + 1] = pick(lane_f).astype(jnp.int32)
        rank_ref[:, k:k + 1] = pick(rank_all).astype(jnp.int32)
        w_ref[:, k:k + 1] = pick(scores) / total * ROUTE_SCALE
    carry_ref[...] += jnp.sum(chosen_b.astype(F32), axis=0, keepdims=True)
    count_ref[...] = carry_ref[...]


def _router(s, mod, norm_g, router_w_b, router_b, n_experts):
    n, d = s.shape
    nt = n // ROW_TILE
    row = lambda w: pl.BlockSpec((ROW_TILE, w), lambda i: (i, 0))
    return pl.pallas_call(
        functools.partial(_router_kernel, n_experts),
        grid=(nt,),
        in_specs=[row(d),
                  pl.BlockSpec((None, 8, d), lambda i: (_row_group(i), 0, 0)),
                  _resident((1, d)),
                  _resident((d, LANES)),
                  _resident((1, LANES))],
        out_specs=[row(d), row(TOP_K), row(TOP_K), row(TOP_K), pl.BlockSpec((1, LANES), lambda i: (0, 0))],
        out_shape=[jax.ShapeDtypeStruct((n, d), F32), jax.ShapeDtypeStruct((n, TOP_K), jnp.int32),
                   jax.ShapeDtypeStruct((n, TOP_K), jnp.int32), jax.ShapeDtypeStruct((n, TOP_K), F32),
                   jax.ShapeDtypeStruct((1, LANES), F32)],
        scratch_shapes=[pltpu.VMEM((1, LANES), F32)],
        compiler_params=_params("arbitrary"),
        name="router",
    )(s, mod, norm_g.reshape(1, d), router_w_b, router_b)


def _routing_plan(eid, rank, counts_f, n_experts, n_tiles):
    counts = counts_f[0, :n_experts].astype(jnp.int32)
    starts = jnp.concatenate([jnp.zeros((1,), jnp.int32), jnp.cumsum(counts)])
    experts = jnp.arange(n_experts, dtype=jnp.int32)
    dest = jnp.sum(jnp.where(eid[..., None] == experts, starts[:-1], 0), axis=-1) + rank
    first_tile = starts[:-1] // EXPERT_ROW_TILE
    last_tile = (starts[1:] - 1) // EXPERT_ROW_TILE
    visits_per_expert = jnp.where(counts > 0, last_tile - first_tile + 1, 0)
    visit_end = jnp.cumsum(visits_per_expert)
    visit_start = visit_end - visits_per_expert
    n_visits = visit_end[-1]
    v = jnp.minimum(jnp.arange(n_tiles + n_experts - 1, dtype=jnp.int32), n_visits - 1)
    visit_expert = jnp.sum((v[:, None] >= visit_end[None, :]).astype(jnp.int32), axis=-1)
    own = visit_expert[:, None] == experts[None, :]
    visit_tile = jnp.sum(jnp.where(own, (first_tile - visit_start)[None, :], 0), axis=-1) + v
    visited = visits_per_expert > 0
    later = jnp.logical_and(experts[None, :] > experts[:, None], visited[None, :])
    next_visited = jnp.min(jnp.where(later, experts[None, :], n_experts), axis=-1)
    next_visited = jnp.where(next_visited == n_experts, -1, next_visited)
    visit_next = jnp.sum(jnp.where(own, next_visited[None, :], 0), axis=-1)
    visit_first = jnp.sum(jnp.where(own, visit_start[None, :], 0), axis=-1) == v
    order = jnp.cumsum(visited.astype(jnp.int32)) - 1
    visit_slot = jnp.sum(jnp.where(own, order[None, :], 0), axis=-1) % 2
    plan = (visit_tile, visit_expert, starts, n_visits.reshape(1),
            visit_first.astype(jnp.int32), visit_slot, visit_next)
    return dest, plan


def _dispatch_kernel(dest_ref, h_ref, xs_hbm, sem):
    def body(j, carry):
        for k in range(TOP_K):
            pltpu.make_async_copy(h_ref.at[pl.ds(j, 1)], xs_hbm.at[pl.ds(dest_ref[k, j], 1)],
                                  sem).start(priority=k % 2)
        return carry

    lax.fori_loop(0, ROW_TILE, body, 0)
    all_rows = xs_hbm.at[pl.ds(0, ROW_TILE * TOP_K)]
    pltpu.make_async_copy(all_rows, all_rows, sem).wait()


def _dispatch(h, dest_tiles):
    n, d = h.shape
    nt = n // ROW_TILE
    return pl.pallas_call(
        _dispatch_kernel,
        grid=(nt,),
        in_specs=[pl.BlockSpec((None, TOP_K, ROW_TILE), lambda i: (i, 0, 0), memory_space=pltpu.SMEM),
                  pl.BlockSpec((ROW_TILE, d), lambda i: (i, 0))],
        out_specs=pl.BlockSpec(memory_space=pl.ANY),
        out_shape=jax.ShapeDtypeStruct((n * TOP_K, d), F32),
        scratch_shapes=[pltpu.SemaphoreType.DMA(())],
        compiler_params=_params("arbitrary"),
        name="moe_dispatch",
    )(dest_tiles, h)


def _grouped_kernel(layer, vt_ref, ve_ref, starts_ref, nv_ref, first_ref, slot_ref, next_ref,
                    x_ref, wg_hbm, wu_hbm, wd_hbm, y_ref,
                    wgb_ref, wub_ref, wdb_ref, sg_ref, su_ref, sd_ref, wsem):
    v = pl.program_id(0)
    e = ve_ref[v]
    t = vt_ref[v]
    prev = jnp.maximum(v - 1, 0)

    def weight_copies(expert, slot):
        return [pltpu.make_async_copy(src.at[layer, expert], dst.at[slot], wsem.at[slot])
                for src, dst in ((wg_hbm, sg_ref), (wu_hbm, su_ref), (wd_hbm, sd_ref))]

    @pl.when(v == 0)
    def _():
        for cp in weight_copies(e, 0):
            cp.start(priority=WEIGHT_DMA_PRIORITY)

    @pl.when(jnp.logical_and(v < nv_ref[0], first_ref[v] == 1))
    def _():
        slot = slot_ref[v]
        for cp in weight_copies(e, slot):
            cp.wait()
        wgb_ref[...] = sg_ref[slot].astype(BF16)
        wub_ref[...] = su_ref[slot].astype(BF16)
        wdb_ref[...] = sd_ref[slot].astype(BF16)
        nxt = next_ref[v]

        @pl.when(nxt >= 0)
        def _():
            for cp in weight_copies(nxt, 1 - slot):
                cp.start(priority=WEIGHT_DMA_PRIORITY)

    @pl.when(v < nv_ref[0])
    def _():
        x = x_ref[...].astype(BF16)
        a = jnp.dot(x, wgb_ref[...], preferred_element_type=F32)
        b = jnp.dot(x, wub_ref[...], preferred_element_type=F32)
        y = jnp.dot((_silu(a) * b).astype(BF16), wdb_ref[...], preferred_element_type=F32)
        rows = x.shape[0]
        r = t * rows + lax.broadcasted_iota(jnp.int32, (rows, 1), 0)
        y = jnp.where(jnp.logical_and(r >= starts_ref[e], r < starts_ref[e + 1]), y, 0.0)
        first_visit_of_tile = jnp.logical_or(v == 0, vt_ref[prev] != t)

        @pl.when(first_visit_of_tile)
        def _():
            y_ref[...] = y

        @pl.when(jnp.logical_not(first_visit_of_tile))
        def _():
            y_ref[...] += y


def _grouped_experts(xs, layer, w_gate, w_up, w_down, plan):
    p, d = xs.shape
    hidden = w_gate.shape[-1]
    tm = EXPERT_ROW_TILE
    tile = lambda v, vt, *_: (vt[v], 0)
    grid_spec = pltpu.PrefetchScalarGridSpec(
        num_scalar_prefetch=len(plan),
        grid=(plan[0].shape[0],),
        in_specs=[pl.BlockSpec((tm, d), tile),
                  pl.BlockSpec(memory_space=pl.ANY), pl.BlockSpec(memory_space=pl.ANY),
                  pl.BlockSpec(memory_space=pl.ANY)],
        out_specs=pl.BlockSpec((tm, d), tile),
        scratch_shapes=[pltpu.VMEM((d, hidden), BF16), pltpu.VMEM((d, hidden), BF16),
                        pltpu.VMEM((hidden, d), BF16),
                        pltpu.VMEM((2, d, hidden), F32), pltpu.VMEM((2, d, hidden), F32),
                        pltpu.VMEM((2, hidden, d), F32), pltpu.SemaphoreType.DMA((2,))])
    return pl.pallas_call(
        functools.partial(_grouped_kernel, layer),
        grid_spec=grid_spec,
        out_shape=jax.ShapeDtypeStruct((p, d), F32),
        compiler_params=_params("arbitrary"),
        name="moe_grouped_experts",
    )(*plan, xs, w_gate, w_up, w_down)


def _combine_kernel(n_ctx_rows, first_tile, final_norm, dest_ref, dest_next_ref, w_ref, h_ref, s_ref,
                    mod_ref, sg_ref, su_ref, sd_ref, fg_ref, ys_hbm, o_ref, ybuf_ref, sems):
    i = pl.program_id(0)
    n_steps = pl.num_programs(0)
    rows = h_ref.shape[0]
    slot = i % 2

    def gather(dref, to_slot):
        def body(j, carry):
            for k in range(TOP_K):
                pltpu.make_async_copy(ys_hbm.at[pl.ds(dref[k, j], 1)],
                                      ybuf_ref.at[to_slot, k, pl.ds(j, 1)], sems.at[to_slot]).start(priority=k % 2)
            return carry
        lax.fori_loop(0, rows, body, 0)

    @pl.when(i == 0)
    def _():
        gather(dest_ref, 0)

    @pl.when(i + 1 < n_steps)
    def _():
        gather(dest_next_ref, 1 - slot)

    hb = h_ref[...].astype(BF16)
    a = jnp.dot(hb, sg_ref[...], preferred_element_type=F32)
    b = jnp.dot(hb, su_ref[...], preferred_element_type=F32)
    y = jnp.dot((_silu(a) * b).astype(BF16), sd_ref[...], preferred_element_type=F32)

    pltpu.make_async_copy(ybuf_ref.at[slot], ybuf_ref.at[slot], sems.at[slot]).wait()
    w = w_ref[...]
    for k in range(TOP_K):
        y += w[:, k:k + 1] * ybuf_ref[slot, k]
    r = (first_tile + i) * rows + lax.broadcasted_iota(jnp.int32, (rows, 1), 0)
    gate2 = jnp.where(r < n_ctx_rows, mod_ref[1, 5:6, :], mod_ref[0, 5:6, :])
    out = s_ref[...] + gate2 * y
    o_ref[...] = _rms(out, fg_ref[...]) if final_norm else out


def _combine(ys, dest_tiles, w, h, s, mod, sh_gate_b, sh_up_b, sh_down_b, final_g, n_ctx_rows, last_layer):
    n, d = s.shape
    tc = COMBINE_ROW_TILE
    first_tile = n_ctx_rows // tc if last_layer else 0
    steps = n // tc - first_tile
    hidden = sh_gate_b.shape[-1]
    row = lambda width: pl.BlockSpec((tc, width), lambda i: (i + first_tile, 0))
    idx = lambda shift: pl.BlockSpec(
        (None, TOP_K, tc), lambda i: (jnp.minimum(i + first_tile + shift, n // tc - 1), 0, 0),
        memory_space=pltpu.SMEM)
    return pl.pallas_call(
        functools.partial(_combine_kernel, n_ctx_rows, first_tile, last_layer),
        grid=(steps,),
        in_specs=[idx(0), idx(1), row(TOP_K), row(d), row(d),
                  pl.BlockSpec((2, 8, d), lambda i: (0, 0, 0)),
                  _resident((d, hidden)), _resident((d, hidden)), _resident((hidden, d)),
                  _resident((1, d)),
                  pl.BlockSpec(memory_space=pl.ANY)],
        out_specs=pl.BlockSpec((tc, d), lambda i: (i, 0)),
        out_shape=jax.ShapeDtypeStruct((steps * tc, d), F32),
        scratch_shapes=[pltpu.VMEM((2, TOP_K, tc, d), F32), pltpu.SemaphoreType.DMA((2,))],
        compiler_params=_params("arbitrary"),
        name="moe_combine",
    )(dest_tiles, dest_tiles, w, h, s, mod, sh_gate_b, sh_up_b, sh_down_b, final_g.reshape(1, d), ys)


def _tile_major(dest, tile):
    n, k = dest.shape
    return dest.reshape(n // tile, tile, k).transpose(0, 2, 1)


def _rope_tables(n_ctx, n_tokens, dim):
    half = dim // 2
    inv_freq = ROPE_THETA ** (-jnp.arange(0, half, 2, dtype=F32) / half)
    t = jnp.arange(n_tokens, dtype=jnp.int32)
    ang_r = (t // GRID_W).astype(F32)[:, None] * inv_freq
    ang_c = (t % GRID_W).astype(F32)[:, None] * inv_freq
    cos = jnp.concatenate([jnp.cos(ang_r)] * 2 + [jnp.cos(ang_c)] * 2, axis=-1)
    sin = jnp.concatenate([-jnp.sin(ang_r), jnp.sin(ang_r), -jnp.sin(ang_c), jnp.sin(ang_c)], axis=-1)
    reps = LANES // dim
    cos = jnp.tile(cos, (1, reps))
    sin = jnp.tile(sin, (1, reps))
    cos = jnp.concatenate([jnp.ones((n_ctx, LANES), F32), cos], axis=0)
    sin = jnp.concatenate([jnp.zeros((n_ctx, LANES), F32), sin], axis=0)
    return cos, sin


def kernel(x, c, ctx, c_ctx, norm1_g, norm2_g, w_mod, b_mod, w_in, gla_wa_f, gla_ba_f, gla_wa_b,
           gla_ba_b, gla_norm_g, q_norm_g, k_norm_g, diff_lq1, diff_lk1, diff_lq2, diff_lk2,
           diff_norm_g, w_out, router_w, router_b, exp_w_gate, exp_w_up, exp_w_down,
           sh_w_gate, sh_w_up, sh_w_down, final_g):
    batch, n_tokens, d = x.shape
    n_ctx = ctx.shape[1]
    depth = w_mod.shape[0]
    n_experts = router_w.shape[-1]
    n_rows = n_ctx + n_tokens
    assert batch == 1 and n_ctx == ROW_TILE and n_tokens % ROW_TILE == 0
    assert TOP_K <= n_experts <= LANES and EXPERT_ROW_TILE <= ROW_TILE

    s = jnp.concatenate([ctx[0], x[0]], axis=0)
    mod_all = _modulation(c, c_ctx, w_mod, b_mod)
    tables = _rope_tables(n_ctx, n_tokens, HEAD_DIM) + _rope_tables(n_ctx, n_tokens, DIFF_DQK)

    for l in range(depth):
        need_ctx = l < depth - 1
        lam_init = 0.8 - 0.6 * math.exp(-0.3 * l)
        mod = mod_all[l]

        w_in_b = jnp.concatenate(
            [w_in[l][:, :ORIG_LR], w_in[l][:, ORIG_LR + 2 * GLA_GATE_RANK:],
             w_in[l][:, ORIG_LR:ORIG_LR + 2 * GLA_GATE_RANK],
             jnp.zeros((d, LANES - 2 * GLA_GATE_RANK), F32)], axis=1).astype(BF16)
        wa = jnp.zeros((LANES, 2 * GLA_QK), F32)
        wa = wa.at[:GLA_GATE_RANK, :GLA_QK].set(gla_wa_f[l])
        wa = wa.at[GLA_GATE_RANK:2 * GLA_GATE_RANK, GLA_QK:].set(gla_wa_b[l]).astype(BF16)
        ba = jnp.concatenate([gla_ba_f[l], gla_ba_b[l]]).reshape(1, -1)
        lamv = jnp.zeros((8, LANES), F32)
        for r, vec in enumerate((diff_lq1[l], diff_lk1[l], diff_lq2[l], diff_lk2[l])):
            lamv = lamv.at[r, :DIFF_DQK].set(vec)
        rw = jnp.pad(router_w[l], ((0, 0), (0, LANES - n_experts))).astype(BF16)
        rb = jnp.pad(router_b[l], (0, LANES - n_experts)).reshape(1, LANES)

        (glaq, glak, glav, gate, la, q_t, k, v_t, dq_t, dk, dv_t) = _input_projection(
            s, mod, norm1_g[l], w_in_b, wa, ba, q_norm_g[l], k_norm_g[l], tables)
        o_gla = _gla(glaq, glak, glav, la)
        gqa_parts = [_gqa_attention(q_t, k, v_t, latent=True)]
        diff_parts = [_diff_attention(dq_t, dk, dv_t, lamv, diff_norm_g[l], lam_init, latent=True)]
        if need_ctx:
            gqa_parts.insert(0, _gqa_attention(q_t, k, v_t, latent=False))
            diff_parts.insert(0, _diff_attention(dq_t, dk, dv_t, lamv, diff_norm_g[l], lam_init, latent=False))
        else:
            gqa_parts.insert(0, jnp.zeros((n_ctx, GQA_Q), BF16))
            diff_parts.insert(0, jnp.zeros((n_ctx, DIFF_V), BF16))
        o_gqa = jnp.concatenate(gqa_parts, axis=0)
        o_diff = jnp.concatenate(diff_parts, axis=0)
        s = _output_projection(s, mod, o_gla, gate, o_gqa, o_diff, gla_norm_g[l], w_out[l].astype(BF16))
        h2, eid, rank, w_route, counts_f = _router(s, mod, norm2_g[l], rw, rb, n_experts)
        dest, plan = _routing_plan(eid, rank, counts_f, n_experts, n_rows * TOP_K // EXPERT_ROW_TILE)
        xs = _dispatch(h2, _tile_major(dest, ROW_TILE))
        ys = _grouped_experts(xs, l, exp_w_gate, exp_w_up, exp_w_down, plan)
        s = _combine(ys, _tile_major(dest, COMBINE_ROW_TILE), w_route, h2, s, mod,
                     sh_w_gate[l].astype(BF16), sh_w_up[l].astype(BF16), sh_w_down[l].astype(BF16),
                     final_g, n_ctx, last_layer=l == depth - 1)

    return s.reshape(batch, n_tokens, d)
```
